```python
import math
import jax, jax.numpy as jnp
from jax import lax
import numpy as np

D_MODEL = 2048
BATCH = 4
SEQ = 4096
DEPTH = 2

GRID_W = 64
CTX_LEN = 256
CONV_W = 4

ATT_HEADS = 8
ATT_QK_DIM = 64
ATT_V_DIM = 128
ATT_WIDTH = ATT_HEADS * ATT_V_DIM
Q_BLOCK = 128
ROPE_THETA = 10000.0

SSD_HEADS = 16
SSD_HEAD_DIM = 64
SSD_WIDTH = SSD_HEADS * SSD_HEAD_DIM
SSD_GROUPS = 4
SSD_STATE = 128
SSD_CHUNK = 128
SSD_XBC = SSD_WIDTH + 2 * SSD_GROUPS * SSD_STATE

LRU_WIDTH = 1024
LRU_BLOCKS = 8
LRU_BLOCK_DIM = LRU_WIDTH // LRU_BLOCKS
LRU_C = 8.0

N_BRANCHES = 3
BRANCH_WIDTH = 1024

N_EXPERTS = 16
N_EXPERT_GROUPS = 4
EXPERTS_PER_GROUP = N_EXPERTS // N_EXPERT_GROUPS
TOP_K = 2
D_EXPERT = 1408

DN_ALPHA = (2 * DEPTH) ** 0.25
DN_BETA = (8 * DEPTH) ** -0.25
LN_EPS = 1e-6

IN_SPLITS = (LRU_WIDTH, LRU_WIDTH, SSD_WIDTH, SSD_XBC, 2 * SSD_HEADS,
             ATT_HEADS * 2 * ATT_QK_DIM, ATT_HEADS * 2 * ATT_QK_DIM, ATT_WIDTH,
             N_BRANCHES * D_MODEL)
D_IN = sum(IN_SPLITS)

kernel_name = "hybrid_lru_ssd_diffattn_moe_dit_block"


def _layernorm(x, g=None, b=None):
    xf = x.astype(jnp.float32)
    mu = jnp.mean(xf, axis=-1, keepdims=True)
    var = jnp.mean(jnp.square(xf - mu), axis=-1, keepdims=True)
    y = (xf - mu) * lax.rsqrt(var + LN_EPS)
    if g is not None:
        y = y * g.astype(jnp.float32) + b.astype(jnp.float32)
    return y.astype(x.dtype)


def _rmsnorm(x, w):
    xf = x.astype(jnp.float32)
    y = xf * lax.rsqrt(jnp.mean(xf * xf, axis=-1, keepdims=True) + LN_EPS)
    return (y * w.astype(jnp.float32)).astype(x.dtype)


def _split_cols(p):
    idx = [int(i) for i in np.cumsum(IN_SPLITS)[:-1]]
    return jnp.split(p, idx, axis=-1)


def _dwconv(x, w, b):
    y = lax.conv_general_dilated(
        x, w[:, None, :].astype(x.dtype), window_strides=(1,),
        padding=[(CONV_W // 2, CONV_W - 1 - CONV_W // 2)],
        dimension_numbers=('NWC', 'WIO', 'NWC'), feature_group_count=x.shape[-1])
    return y + b.astype(x.dtype)


def _axial_rope_tables(rows):
    row = jnp.repeat(jnp.arange(rows, dtype=jnp.float32), GRID_W)
    col = jnp.tile(jnp.arange(GRID_W, dtype=jnp.float32), rows)
    axis_dim = ATT_QK_DIM // 2
    freqs = ROPE_THETA ** (-jnp.arange(0, axis_dim, 2, dtype=jnp.float32) / axis_dim)
    ang = jnp.stack([row[:, None] * freqs, col[:, None] * freqs], axis=1)
    return jnp.cos(ang), jnp.sin(ang)


def _apply_axial_rope(x, cos, sin):
    xa = x.reshape(x.shape[:-1] + (2, ATT_QK_DIM // 2))
    half = ATT_QK_DIM // 4
    x1, x2 = xa[..., :half], xa[..., half:]
    c = cos[None, :, None, None].astype(x.dtype)
    s = sin[None, :, None, None].astype(x.dtype)
    out = jnp.concatenate([x1 * c - x2 * s, x1 * s + x2 * c], axis=-1)
    return out.reshape(x.shape)


def _linear_combine(left, right):
    a_l, b_l = left
    a_r, b_r = right
    return a_l * a_r, a_r * b_l + b_r


def _rglru_scan(x, w_rg, b_rg, lam, h0, reverse):
    Bsz, L, W = x.shape
    xb = x.reshape(Bsz, L, LRU_BLOCKS, LRU_BLOCK_DIM)
    gl = jnp.einsum('blnc,gncd->gblnd', xb, w_rg.astype(x.dtype)).reshape(2, Bsz, L, W)
    gl = gl.astype(jnp.float32) + b_rg.astype(jnp.float32)[:, None, None, :]
    r_gate = jax.nn.sigmoid(gl[0])
    i_gate = jax.nn.sigmoid(gl[1])
    log_a = -LRU_C * r_gate * jax.nn.softplus(-lam.astype(jnp.float32))
    a = jnp.exp(log_a)
    b = jnp.sqrt(-jnp.expm1(2.0 * log_a)) * (i_gate * x.astype(jnp.float32))
    if reverse:
        a, b = jnp.flip(a, 1), jnp.flip(b, 1)
    if h0 is not None:
        b = b.at[:, 0].add(a[:, 0] * h0)
    _, h = lax.associative_scan(_linear_combine, (a, b), axis=1)
    h_last = h[:, -1]
    if reverse:
        h = jnp.flip(h, 1)
    return h.astype(x.dtype), h_last


def _rglru_branch(x_c, g_c, x_l, g_l, conv_w, conv_b, gate_w, gate_b, lam, need_ctx):
    xc = _dwconv(x_c, conv_w, conv_b)
    xl = _dwconv(x_l, conv_w, conv_b)
    y_c = jnp.zeros_like(xc)
    y_l = jnp.zeros_like(xl)
    for d in range(2):
        rev = d == 1
        h_c, s_c = _rglru_scan(xc, gate_w[d], gate_b[d], lam[d], None, rev)
        h_l, _ = _rglru_scan(xl, gate_w[d], gate_b[d], lam[d], s_c, rev)
        y_c = y_c + h_c
        y_l = y_l + h_l
    out_l = y_l * jax.nn.gelu(g_l)
    out_c = y_c * jax.nn.gelu(g_c) if need_ctx else None
    return out_c, out_l


def _segsum(a):
    cs = jnp.cumsum(a, axis=-1)
    diff = cs[..., :, None] - cs[..., None, :]
    T = a.shape[-1]
    mask = jnp.tril(jnp.ones((T, T), dtype=bool))
    return jnp.where(mask, diff, -jnp.inf)


def _ssd_scan(x, dt, A, Bm, Cm, h0):
    Bsz, L, H, P = x.shape
    G, N = Bm.shape[2], Bm.shape[3]
    R = H // G
    T = SSD_CHUNK
    nc = L // T
    x = x.reshape(Bsz, nc, T, G, R, P)
    dt = dt.reshape(Bsz, nc, T, G, R)
    Bm = Bm.reshape(Bsz, nc, T, G, N)
    Cm = Cm.reshape(Bsz, nc, T, G, N)
    xdt = x * dt[..., None]
    a = jnp.transpose(dt * A.reshape(G, R), (0, 3, 4, 1, 2))
    a_cs = jnp.cumsum(a, axis=-1)
    cb = jnp.einsum('bclgn,bcsgn->bgcls', Cm, Bm)
    w_diag = cb[:, :, None] * jnp.exp(_segsum(a))
    y_diag = jnp.einsum('bgrcls,bcsgrp->bclgrp', w_diag, xdt)
    decay_states = jnp.exp(a_cs[..., -1:] - a_cs)
    states = jnp.einsum('bcsgn,bgrcs,bcsgrp->bcgrpn', Bm, decay_states, xdt)
    states = jnp.concatenate([h0.reshape(Bsz, 1, G, R, P, N), states], axis=1)
    chunk_tot = jnp.pad(a_cs[..., -1], ((0, 0), (0, 0), (0, 0), (1, 0)))
    decay_chunk = jnp.exp(_segsum(chunk_tot))
    states = jnp.einsum('bgrzc,bcgrpn->bzgrpn', decay_chunk, states)
    prev_states, final = states[:, :-1], states[:, -1]
    y_off = jnp.einsum('bclgn,bcgrpn,bgrcl->bclgrp', Cm, prev_states, jnp.exp(a_cs))
    y = (y_diag + y_off).reshape(Bsz, L, H, P)
    return y, final.reshape(Bsz, H, P, N)


def _ssd_direction(xs, dt, A, Bm, Cm, h0, reverse):
    if reverse:
        xs, dt, Bm, Cm = jnp.flip(xs, 1), jnp.flip(dt, 1), jnp.flip(Bm, 1), jnp.flip(Cm, 1)
    y, h_last = _ssd_scan(xs, dt, A, Bm, Cm, h0)
    if reverse:
        y = jnp.flip(y, 1)
    return y, h_last


def _ssd_branch(z_c, xbc_c, dt_c, z_l, xbc_l, dt_l, conv_w, conv_b, a_log, dt_bias, d_skip,
                norm_w, need_ctx):
    def prep(xbc, dt_raw):
        xbc = jax.nn.silu(_dwconv(xbc, conv_w, conv_b)).astype(jnp.float32)
        xs, Bm, Cm = jnp.split(xbc, [SSD_WIDTH, SSD_WIDTH + SSD_GROUPS * SSD_STATE], axis=-1)
        Bsz, L = xs.shape[:2]
        xs = xs.reshape(Bsz, L, SSD_HEADS, SSD_HEAD_DIM)
        Bm = Bm.reshape(Bsz, L, SSD_GROUPS, SSD_STATE)
        Cm = Cm.reshape(Bsz, L, SSD_GROUPS, SSD_STATE)
        dt = jax.nn.softplus(dt_raw.astype(jnp.float32).reshape(Bsz, L, 2, SSD_HEADS)
                             + dt_bias.astype(jnp.float32))
        return xs, Bm, Cm, dt
    xs_c, B_c, C_c, dtd_c = prep(xbc_c, dt_c)
    xs_l, B_l, C_l, dtd_l = prep(xbc_l, dt_l)
    dsk = d_skip.astype(jnp.float32)[:, None]
    y_c = dsk * xs_c
    y_l = dsk * xs_l
    h0 = jnp.zeros((xs_c.shape[0], SSD_HEADS, SSD_HEAD_DIM, SSD_STATE), jnp.float32)
    for d in range(2):
        rev = d == 1
        A = -jnp.exp(a_log[d].astype(jnp.float32))
        yc, s_c = _ssd_direction(xs_c, dtd_c[:, :, d], A, B_c, C_c, h0, rev)
        yl, _ = _ssd_direction(xs_l, dtd_l[:, :, d], A, B_l, C_l, s_c, rev)
        y_c = y_c + yc
        y_l = y_l + yl

    def gate_out(y, z):
        y = y.reshape(y.shape[:2] + (SSD_WIDTH,)).astype(z.dtype)
        return _rmsnorm(y * jax.nn.silu(z), norm_w)
    out_l = gate_out(y_l, z_l)
    out_c = gate_out(y_c, z_c) if need_ctx else None
    return out_c, out_l


def _diff_attn_branch(q_c, k_c, v_c, q_l, k_l, v_l, rope_cos, rope_sin, lam_vecs, lam_init,
                      norm_w, need_ctx):
    def heads_qk(t):
        return t.reshape(t.shape[:2] + (ATT_HEADS, 2, ATT_QK_DIM))

    def heads_v(t):
        return t.reshape(t.shape[:2] + (ATT_HEADS, ATT_V_DIM))
    q_c, k_c, v_c = heads_qk(q_c), heads_qk(k_c), heads_v(v_c)
    q_l = _apply_axial_rope(heads_qk(q_l), rope_cos, rope_sin)
    k_l = _apply_axial_rope(heads_qk(k_l), rope_cos, rope_sin)
    v_l = heads_v(v_l)
    lv = lam_vecs.astype(jnp.float32)
    lam = jnp.exp(jnp.sum(lv[0] * lv[1])) - jnp.exp(jnp.sum(lv[2] * lv[3])) + lam_init
    k_all = jnp.concatenate([k_c, k_l], axis=1)
    v_all = jnp.concatenate([v_c, v_l], axis=1)
    scale = ATT_QK_DIM ** -0.5

    def attend(q, k, v):
        s = jnp.einsum('bqhmd,bkhmd->bhmqk', q, k, preferred_element_type=jnp.float32) * scale
        p = jax.nn.softmax(s, axis=-1)
        w = (p[:, :, 0] - lam * p[:, :, 1]).astype(v.dtype)
        return jnp.einsum('bhqk,bkhd->bqhd', w, v)

    def finish(o):
        o = _rmsnorm(o, norm_w) * (1.0 - lam_init)
        return o.reshape(o.shape[:2] + (ATT_WIDTH,))
    Bsz, S = q_l.shape[:2]
    n_blocks = S // Q_BLOCK
    q_blocks = jnp.moveaxis(q_l.reshape(Bsz, n_blocks, Q_BLOCK, ATT_HEADS, 2, ATT_QK_DIM), 1, 0)
    o_l = lax.map(lambda qb: attend(qb, k_all, v_all), q_blocks)
    o_l = jnp.moveaxis(o_l, 0, 1).reshape(Bsz, S, ATT_HEADS, ATT_V_DIM)
    out_c = finish(attend(q_c, k_c, v_c)) if need_ctx else None
    return out_c, finish(o_l)


def _mixer_sublayer(h_c, h_l, rope_cos, rope_sin, lam_init, need_ctx, w_in, lru_conv_w,
                    lru_conv_b, lru_gate_w, lru_gate_b, lru_lambda, ssd_conv_w, ssd_conv_b,
                    ssd_a_log, ssd_dt_bias, ssd_d, ssd_norm_w, att_lambda, att_norm_w,
                    w_branch, w_out):
    pc = _split_cols(h_c @ w_in)
    pl = _split_cols(h_l @ w_in)
    lru_c, lru_l = _rglru_branch(pc[0], pc[1], pl[0], pl[1], lru_conv_w, lru_conv_b,
                                 lru_gate_w, lru_gate_b, lru_lambda, need_ctx)
    ssd_c, ssd_l = _ssd_branch(pc[2], pc[3], pc[4], pl[2], pl[3], pl[4], ssd_conv_w, ssd_conv_b,
                               ssd_a_log, ssd_dt_bias, ssd_d, ssd_norm_w, need_ctx)
    att_c, att_l = _diff_attn_branch(pc[5], pc[6], pc[7], pl[5], pl[6], pl[7], rope_cos,
                                     rope_sin, att_lambda, lam_init, att_norm_w, need_ctx)

    def merge(branches, gate_cols):
        g = jax.nn.sigmoid(gate_cols.astype(jnp.float32)).astype(gate_cols.dtype)
        g = g.reshape(g.shape[:-1] + (N_BRANCHES, D_MODEL))
        m = g[..., 0, :] * (branches[0] @ w_branch[0])
        for k in range(1, N_BRANCHES):
            m = m + g[..., k, :] * (branches[k] @ w_branch[k])
        return m @ w_out
    out_l = merge([lru_l, ssd_l, att_l], pl[8])
    out_c = merge([lru_c, ssd_c, att_c], pc[8]) if need_ctx else None
    return out_c, out_l


def _moe(h, w_router, router_bias, w_gate, w_up, w_down):
    T = h.shape[0]
    scores = jax.nn.softmax((h @ w_router).astype(jnp.float32), axis=-1)
    sel = (scores + router_bias.astype(jnp.float32)).reshape(T, N_EXPERT_GROUPS, EXPERTS_PER_GROUP)
    group_score = lax.top_k(sel, TOP_K)[0].sum(-1)
    g_idx = jnp.argmax(group_score, axis=-1)
    in_group = jnp.take_along_axis(sel, g_idx[:, None, None], axis=1)[:, 0]
    _, local = lax.top_k(in_group, TOP_K)
    e_idx = g_idx[:, None] * EXPERTS_PER_GROUP + local
    w = jnp.take_along_axis(scores, e_idx, axis=1)
    w = w / jnp.sum(w, axis=-1, keepdims=True)
    combine = jnp.sum(jax.nn.one_hot(e_idx, N_EXPERTS, dtype=jnp.float32) * w[..., None],
                      axis=1).astype(h.dtype)
    out = jnp.zeros_like(h)
    for e in range(N_EXPERTS):
        act = jax.nn.silu(h @ w_gate[e]) * (h @ w_up[e])
        out = out + combine[:, e:e + 1] * (act @ w_down[e])
    return out


def setup_inputs(seed: int = 0) -> dict:
    key = jax.random.key(seed)
    ks = jax.random.split(key, 32)
    f32 = jnp.float32
    D = D_MODEL

    def nrm(k, shape, scale):
        return jax.random.normal(k, shape, f32) * scale
    a0 = jax.random.uniform(ks[11], (DEPTH, 2, LRU_WIDTH), f32, 0.9, 0.999)
    a_root = a0 ** (1.0 / LRU_C)
    dt0 = jnp.exp(jax.random.uniform(ks[15], (DEPTH, 2, SSD_HEADS), f32,
                                     math.log(1e-3), math.log(1e-1)))
    return {
        "x": nrm(ks[0], (BATCH, SEQ, D), 1.0),
        "c": nrm(ks[1], (BATCH, D), 1.0),
        "ctx": nrm(ks[2], (BATCH, CTX_LEN, D), 1.0),
        "c_ctx": nrm(ks[3], (D,), 1.0),
        "w_ada": nrm(ks[4], (DEPTH, D, 6 * D), 0.5 * D ** -0.5),
        "b_ada": nrm(ks[5], (DEPTH, 6 * D), 0.02),
        "w_in": nrm(ks[6], (DEPTH, D, D_IN), D ** -0.5),
        "lru_conv_w": nrm(ks[7], (DEPTH, CONV_W, LRU_WIDTH), CONV_W ** -0.5),
        "lru_conv_b": nrm(ks[8], (DEPTH, LRU_WIDTH), 0.01),
        "lru_gate_w": nrm(ks[9], (DEPTH, 2, 2, LRU_BLOCKS, LRU_BLOCK_DIM, LRU_BLOCK_DIM),
                          LRU_BLOCK_DIM ** -0.5),
        "lru_gate_b": nrm(ks[10], (DEPTH, 2, 2, LRU_WIDTH), 0.01),
        "lru_lambda": jnp.log(a_root) - jnp.log1p(-a_root),
        "ssd_conv_w": nrm(ks[12], (DEPTH, CONV_W, SSD_XBC), CONV_W ** -0.5),
        "ssd_conv_b": nrm(ks[13], (DEPTH, SSD_XBC), 0.01),
        "ssd_a_log": jnp.log(jax.random.uniform(ks[14], (DEPTH, 2, SSD_HEADS), f32, 1.0, 16.0)),
        "ssd_dt_bias": dt0 + jnp.log(-jnp.expm1(-dt0)),
        "ssd_d": 1.0 + nrm(ks[16], (DEPTH, SSD_HEADS), 0.1),
        "ssd_norm_w": 1.0 + nrm(ks[17], (DEPTH, SSD_WIDTH), 0.01),
        "att_lambda": nrm(ks[18], (DEPTH, 4, ATT_QK_DIM), 0.1),
        "att_norm_w": 1.0 + nrm(ks[19], (DEPTH, ATT_V_DIM), 0.01),
        "w_branch": nrm(ks[20], (DEPTH, N_BRANCHES, BRANCH_WIDTH, D), BRANCH_WIDTH ** -0.5),
        "w_out": nrm(ks[21], (DEPTH, D, D), DN_BETA * D ** -0.5),
        "ln_g": 1.0 + nrm(ks[22], (DEPTH, 2, D), 0.01),
        "ln_b": nrm(ks[23], (DEPTH, 2, D), 0.01),
        "w_router": nrm(ks[24], (D, N_EXPERTS), D ** -0.5),
        "router_bias": nrm(ks[25], (N_EXPERTS,), 0.01),
        "w_gate_e": nrm(ks[26], (DEPTH, N_EXPERTS, D, D_EXPERT), D ** -0.5),
        "w_up_e": nrm(ks[27], (DEPTH, N_EXPERTS, D, D_EXPERT), D ** -0.5),
        "w_down_e": nrm(ks[28], (DEPTH, N_EXPERTS, D_EXPERT, D), DN_BETA * D_EXPERT ** -0.5),
    }


def reference(x, c, ctx, c_ctx, w_ada, b_ada, w_in, lru_conv_w, lru_conv_b, lru_gate_w,
              lru_gate_b, lru_lambda, ssd_conv_w, ssd_conv_b, ssd_a_log, ssd_dt_bias, ssd_d,
              ssd_norm_w, att_lambda, att_norm_w, w_branch, w_out, ln_g, ln_b, w_router,
              router_bias, w_gate_e, w_up_e, w_down_e):
    S = x.shape[1]
    rows = S // GRID_W
    rope_cos, rope_sin = _axial_rope_tables(rows)
    xc = ctx
    for l in range(DEPTH):
        need_ctx = l < DEPTH - 1
        lam_init = 0.8 - 0.6 * math.exp(-0.3 * l)
        mod_l = jnp.split(jax.nn.silu(c) @ w_ada[l] + b_ada[l], 6, axis=-1)
        mod_c = jnp.split(jax.nn.silu(c_ctx) @ w_ada[l] + b_ada[l], 6, axis=-1)
        sh1, sc1, g1, sh2, sc2, g2 = [m[:, None, :] for m in mod_l]
        csh1, csc1, cg1, csh2, csc2, cg2 = mod_c
        h_l = _layernorm(x) * (1.0 + sc1) + sh1
        h_c = _layernorm(xc) * (1.0 + csc1) + csh1
        m_c, m_l = _mixer_sublayer(h_c, h_l, rope_cos, rope_sin, lam_init, need_ctx, w_in[l],
                                   lru_conv_w[l], lru_conv_b[l], lru_gate_w[l], lru_gate_b[l],
                                   lru_lambda[l], ssd_conv_w[l], ssd_conv_b[l], ssd_a_log[l],
                                   ssd_dt_bias[l], ssd_d[l], ssd_norm_w[l], att_lambda[l],
                                   att_norm_w[l], w_branch[l], w_out[l])
        x = _layernorm(DN_ALPHA * x + g1 * m_l, ln_g[l, 0], ln_b[l, 0])
        h_l = _layernorm(x) * (1.0 + sc2) + sh2
        if need_ctx:
            xc = _layernorm(DN_ALPHA * xc + cg1 * m_c, ln_g[l, 0], ln_b[l, 0])
            h_c = _layernorm(xc) * (1.0 + csc2) + csh2
            n_ctx_tok = xc.shape[0] * xc.shape[1]
            tokens = jnp.concatenate([h_c.reshape(-1, D_MODEL), h_l.reshape(-1, D_MODEL)], axis=0)
        else:
            n_ctx_tok = 0
            tokens = h_l.reshape(-1, D_MODEL)
        f = _moe(tokens, w_router, router_bias, w_gate_e[l], w_up_e[l], w_down_e[l])
        x = _layernorm(DN_ALPHA * x + g2 * f[n_ctx_tok:].reshape(x.shape), ln_g[l, 1], ln_b[l, 1])
        if need_ctx:
            xc = _layernorm(DN_ALPHA * xc + cg2 * f[:n_ctx_tok].reshape(xc.shape),
                            ln_g[l, 1], ln_b[l, 1])
    return x
```

```python
import functools
import math

import jax
import jax.numpy as jnp
from jax import lax
from jax.experimental import pallas as pl
from jax.experimental.pallas import tpu as pltpu

F32 = jnp.float32
BF16 = jnp.bfloat16

GRID_W = 64
ATT_HEADS = 8
ATT_QK_DIM = 64
ATT_V_DIM = 128
ROPE_THETA = 10000.0
SSD_HEADS = 16
SSD_HEAD_DIM = 64
SSD_GROUPS = 4
SSD_STATE = 128
SSD_WIDTH = SSD_HEADS * SSD_HEAD_DIM
LRU_WIDTH = 1024
LRU_BLOCKS = 8
LRU_BLOCK_DIM = LRU_WIDTH // LRU_BLOCKS
LRU_C = 8.0
N_EXPERTS = 16
N_EXPERT_GROUPS = 4
EXPERTS_PER_GROUP = 4
LN_EPS = 1e-6

LANES = 128
SLAB = 16
SEQ_CHUNK = 256
HALO = 16
MOE_TILE = 256
NEG_BIG = -1e30
VMEM_LIMIT = 56 * 2 ** 20

COL_XBC, COL_LRU_X, COL_LRU_G, COL_Z, COL_Q, COL_K, COL_V, COL_GATES = 0, 2, 3, 4, 5, 6, 7, 8


def _cparams(*sem):
    return pltpu.CompilerParams(dimension_semantics=sem, vmem_limit_bytes=VMEM_LIMIT)


def _dot(a, b):
    return jnp.dot(a, b, preferred_element_type=F32)


def _dot_nt(a, b):
    return lax.dot_general(a, b, (((1,), (1,)), ((), ())), preferred_element_type=F32)


def _split3(v):
    hi = v.astype(BF16)
    r = v - hi.astype(F32)
    mid = r.astype(BF16)
    lo = (r - mid.astype(F32)).astype(BF16)
    return hi, mid, lo


def _split2(v):
    hi = v.astype(BF16)
    lo = (v - hi.astype(F32)).astype(BF16)
    return hi, lo


def _ln(x):
    mu = jnp.mean(x, axis=-1, keepdims=True)
    xc = x - mu
    var = jnp.mean(xc * xc, axis=-1, keepdims=True)
    return xc * lax.rsqrt(var + LN_EPS)


def _sigmoid(x):
    return jax.nn.sigmoid(x)


def _softplus(x):
    return jnp.maximum(x, 0.0) + jnp.log1p(jnp.exp(-jnp.abs(x)))


def _ada_kernel(c_ref, w_ref, b_ref, o_ref):
    c = c_ref[...]
    a = c * _sigmoid(c)
    a_hi, a_lo = _split2(a)
    w_hi, w_lo = _split2(w_ref[...])
    o_ref[...] = _dot(a_hi, w_hi) + _dot(a_lo, w_hi) + _dot(a_hi, w_lo) + b_ref[...]


def _ada(cvec, w, b):
    R, D = cvec.shape
    N = w.shape[1]
    tn = 1024
    return pl.pallas_call(
        _ada_kernel,
        grid=(N // tn,),
        in_specs=[pl.BlockSpec((R, D), lambda j: (0, 0)),
                  pl.BlockSpec((D, tn), lambda j: (0, j)),
                  pl.BlockSpec((1, tn), lambda j: (0, j))],
        out_specs=pl.BlockSpec((R, tn), lambda j: (0, j)),
        out_shape=jax.ShapeDtypeStruct((R, N), F32),
        compiler_params=_cparams("arbitrary"),
        name="ada",
    )(cvec, w, b.reshape(1, N))


def _inproj_kernel(x_ref, mod_ref, w_ref, wdt_ref, o_ref, dt_ref, h_ref, *, tm, rb):
    @pl.when(pl.program_id(1) == 0)
    def _():
        sh = mod_ref[0, 0:1, :]
        sc = mod_ref[0, 1:2, :]

        def body(r, carry):
            rows = pl.ds(pl.multiple_of(r * rb, rb), rb)
            h = _ln(x_ref[rows, :]) * (1.0 + sc) + sh
            hb = h.astype(BF16)
            h_ref[rows, :] = hb
            dt_ref[rows, :] = _dot(hb, wdt_ref[...])
            return carry

        lax.fori_loop(0, tm // rb, body, 0)

    o_ref[...] = _dot(h_ref[...], w_ref[...]).astype(o_ref.dtype)


def _inproj(x, mod, w_main, w_dt, *, tm, tn, n_lat, S):
    M, D = x.shape
    N = w_main.shape[1]

    def midx(i):
        return jnp.where(i * tm < n_lat, 1 + (i * tm) // S, 0)

    return pl.pallas_call(
        functools.partial(_inproj_kernel, tm=tm, rb=min(tm, 256)),
        grid=(M // tm, N // tn),
        in_specs=[pl.BlockSpec((tm, D), lambda i, j: (i, 0)),
                  pl.BlockSpec((1, 6, D), lambda i, j: (midx(i), 0, 0)),
                  pl.BlockSpec((D, tn), lambda i, j: (0, j)),
                  pl.BlockSpec((D, LANES), lambda i, j: (0, 0))],
        out_specs=[pl.BlockSpec((tm, tn), lambda i, j: (i, j)),
                   pl.BlockSpec((tm, LANES), lambda i, j: (i, 0))],
        out_shape=[jax.ShapeDtypeStruct((M, N), BF16), jax.ShapeDtypeStruct((M, LANES), F32)],
        scratch_shapes=[pltpu.VMEM((tm, D), BF16)],
        compiler_params=_cparams("parallel", "arbitrary"),
        name="inproj",
    )(x, mod, w_main, w_dt)


def _chunk_of_step(i, ncc, nlc, rev):
    if not rev:
        return i
    return jnp.where(i < ncc, ncc - 1 - i, ncc + nlc - 1 - (i - ncc))


def _block_of_chunk(b, c, B, ncc, nlc):
    return jnp.where(c < ncc, B * nlc + b * ncc + c, b * nlc + (c - ncc))


def _seq_specs(width, col, *, B, ncc, nlc, rev, n_rows):
    hb = SEQ_CHUNK // HALO
    last = n_rows // HALO - 1

    def blk(b, i):
        return _block_of_chunk(b, _chunk_of_step(i, ncc, nlc, rev), B, ncc, nlc)

    main = pl.BlockSpec((SEQ_CHUNK, width), lambda b, i: (blk(b, i), col))
    prev = pl.BlockSpec((HALO, width), lambda b, i: (jnp.maximum(blk(b, i) * hb - 1, 0), col))
    nxt = pl.BlockSpec((HALO, width), lambda b, i: (jnp.minimum((blk(b, i) + 1) * hb, last), col))
    return blk, main, prev, nxt


def _dwconv_chunk(x_ref, xp_ref, xn_ref, cw_ref, cb_ref, c, ncc, nch):
    T = SEQ_CHUNK
    x = x_ref[...].astype(F32)
    has_prev = jnp.where((c == 0) | (c == ncc), 0.0, 1.0)
    has_next = jnp.where((c == ncc - 1) | (c == nch - 1), 0.0, 1.0)
    p6 = xp_ref[HALO - 2:HALO - 1, :].astype(F32) * has_prev
    p7 = xp_ref[HALO - 1:HALO, :].astype(F32) * has_prev
    n0 = xn_ref[0:1, :].astype(F32) * has_next
    row = lax.broadcasted_iota(jnp.int32, (T, 1), 0)
    xm1 = jnp.where(row == 0, p7, pltpu.roll(x, 1, 0))
    xm2 = jnp.where(row == 0, p6, jnp.where(row == 1, p7, pltpu.roll(x, 2, 0)))
    xp1 = jnp.where(row == T - 1, n0, pltpu.roll(x, T - 1, 0))
    return (cw_ref[0:1, :] * xm2 + cw_ref[1:2, :] * xm1 + cw_ref[2:3, :] * x
            + cw_ref[3:4, :] * xp1 + cb_ref[...])


def _linear_scan(a, b, rev):
    T = a.shape[0]
    row = lax.broadcasted_iota(jnp.int32, (T, 1), 0)
    s = 1
    while s < T:
        if rev:
            valid = row < T - s
            shift = T - s
        else:
            valid = row >= s
            shift = s
        a_p = jnp.where(valid, pltpu.roll(a, shift, 0), 1.0)
        b_p = jnp.where(valid, pltpu.roll(b, shift, 0), 0.0)
        b = a * b_p + b
        a = a * a_p
        s *= 2
    return a, b


def _lru_kernel(x_ref, xp_ref, xn_ref, cw_ref, cb_ref, wg_ref, gb_ref, lam_ref, o_ref, carry_ref,
                *, rev, ncc, nlc):
    T = SEQ_CHUNK
    i = pl.program_id(1)
    c = _chunk_of_step(i, ncc, nlc, rev)

    @pl.when(i == 0)
    def _():
        carry_ref[...] = jnp.zeros_like(carry_ref)

    xc = _dwconv_chunk(x_ref, xp_ref, xn_ref, cw_ref, cb_ref, c, ncc, ncc + nlc)
    sp = _softplus(-lam_ref[...])
    for n in range(LRU_BLOCKS):
        sl = slice(n * LRU_BLOCK_DIM, (n + 1) * LRU_BLOCK_DIM)
        xb = xc[:, sl]
        gl = _dot(xb.astype(BF16), wg_ref[n])
        r_gate = _sigmoid(gl[:, :LRU_BLOCK_DIM] + gb_ref[0:1, sl])
        i_gate = _sigmoid(gl[:, LRU_BLOCK_DIM:] + gb_ref[1:2, sl])
        log_a = -LRU_C * r_gate * sp[:, sl]
        a = jnp.exp(log_a)
        bb = jnp.sqrt(1.0 - a * a) * (i_gate * xb)
        a_s, b_s = _linear_scan(a, bb, rev)
        h = b_s + a_s * carry_ref[0:1, sl]
        o_ref[:, sl] = h.astype(o_ref.dtype)
        carry_ref[0:1, sl] = h[0:1, :] if rev else h[T - 1:T, :]


def _lru_dir(P, conv_w, conv_b, wg, gb, lam, *, rev, B, ncc, nlc):
    M = P.shape[0]
    W = LRU_WIDTH
    blk, main, prev, nxt = _seq_specs(W, COL_LRU_X, B=B, ncc=ncc, nlc=nlc, rev=rev, n_rows=M)
    full2 = lambda shape: pl.BlockSpec(shape, lambda b, i: (0, 0))
    return pl.pallas_call(
        functools.partial(_lru_kernel, rev=rev, ncc=ncc, nlc=nlc),
        grid=(B, ncc + nlc),
        in_specs=[main, prev, nxt, full2((4, W)), full2((1, W)),
                  pl.BlockSpec((LRU_BLOCKS, LRU_BLOCK_DIM, 2 * LRU_BLOCK_DIM), lambda b, i: (0, 0, 0)),
                  full2((2, W)), full2((1, W))],
        out_specs=pl.BlockSpec((SEQ_CHUNK, W), lambda b, i: (blk(b, i), 0)),
        out_shape=jax.ShapeDtypeStruct((M, W), BF16),
        scratch_shapes=[pltpu.VMEM((8, W), F32)],
        compiler_params=_cparams("parallel", "arbitrary"),
        name="lru_bwd" if rev else "lru_fwd",
    )(P, P, P, conv_w, conv_b, wg, gb, lam)


def _ssd_kernel(x_ref, xp_ref, xn_ref, dt_ref, cw_ref, cb_ref, sp_ref, o_ref, st_ref,
                *, rev, d, ncc, nlc, add_skip):
    T = SEQ_CHUNK
    N = SSD_STATE
    HPG = SSD_HEADS // SSD_GROUPS
    GW = HPG * SSD_HEAD_DIM
    i = pl.program_id(1)
    c = _chunk_of_step(i, ncc, nlc, rev)

    @pl.when(i == 0)
    def _():
        st_ref[...] = jnp.zeros_like(st_ref)

    xc = _dwconv_chunk(x_ref, xp_ref, xn_ref, cw_ref, cb_ref, c, ncc, ncc + nlc)
    xbc = xc * _sigmoid(xc)
    xs = xbc[:, :SSD_WIDTH]
    Bm = xbc[:, SSD_WIDTH:SSD_WIDTH + SSD_GROUPS * N]
    Cm = xbc[:, SSD_WIDTH + SSD_GROUPS * N:]

    lane = lax.broadcasted_iota(jnp.int32, (1, LANES), 1)
    in_dir = (lane >= SSD_HEADS * d) & (lane < SSD_HEADS * (d + 1))
    dt = jnp.where(in_dir, _softplus(dt_ref[...] + sp_ref[0:1, :]), 0.0)
    a_neg = jnp.where(in_dir, -jnp.exp(sp_ref[1:2, :]), 0.0)
    a = dt * a_neg

    r_i = lax.broadcasted_iota(jnp.int32, (T, T), 0)
    c_i = lax.broadcasted_iota(jnp.int32, (T, T), 1)
    tri = (c_i >= r_i) if rev else (c_i <= r_i)
    tri_b = jnp.where(tri, 1.0, 0.0).astype(BF16)
    cs = sum(_dot(tri_b, p) for p in _split3(a))
    e_r = lax.broadcasted_iota(jnp.int32, (LANES, LANES), 0)
    e_c = lax.broadcasted_iota(jnp.int32, (LANES, LANES), 1)
    eye = jnp.where(e_r == e_c, 1.0, 0.0).astype(BF16)
    cs_row = sum(_dot_nt(eye, p) for p in _split3(cs))
    cs_end = cs[0:1, :] if rev else cs[T - 1:T, :]

    x_r = lax.broadcasted_iota(jnp.int32, (LANES, SSD_WIDTH), 0)
    x_c = lax.broadcasted_iota(jnp.int32, (LANES, SSD_WIDTH), 1)
    expand_m = jnp.where(x_r == SSD_HEADS * d + x_c // SSD_HEAD_DIM, 1.0, 0.0).astype(BF16)

    def expand(v):
        return sum(_dot(p, expand_m) for p in _split3(v))

    dt_e = expand(dt)
    dec_out_e = expand(jnp.exp(cs))
    dec_st_e = expand(jnp.exp(cs_end - cs))
    row8 = lax.broadcasted_iota(jnp.int32, (8, LANES), 0)
    small = jnp.where(row8 == 0, jnp.exp(cs_end), jnp.where(row8 == 1, sp_ref[2:3, :], 0.0))
    small_e = expand(small)
    dec_ch_e = small_e[0:1, :]
    dskip_e = small_e[1:2, :]

    xdt = xs * dt_e
    xdt_b = xdt.astype(BF16)
    z_st = (xdt * dec_st_e).astype(BF16)
    lo_half = lax.broadcasted_iota(jnp.int32, (1, LANES), 1) < SSD_HEAD_DIM

    for g in range(SSD_GROUPS):
        Bg = Bm[:, g * N:(g + 1) * N].astype(BF16)
        Cg = Cm[:, g * N:(g + 1) * N].astype(BF16)
        CB = _dot_nt(Cg, Bg)
        gs = slice(g * GW, (g + 1) * GW)
        Sg = st_ref[:, gs]
        y_off = _dot(Cg, Sg.astype(BF16)) * dec_out_e[:, gs]
        for p in range(HPG // 2):
            col = g * GW + p * LANES
            x_pair = xdt_b[:, col:col + LANES]
            y_pair = y_off[:, p * LANES:(p + 1) * LANES]
            for q in range(2):
                hh = SSD_HEADS * d + g * HPG + 2 * p + q
                L = jnp.exp(jnp.where(tri, cs[:, hh:hh + 1] - cs_row[hh:hh + 1, :], NEG_BIG))
                Wm = (CB * L).astype(BF16)
                xm = jnp.where(lo_half if q == 0 else jnp.logical_not(lo_half), x_pair,
                               jnp.zeros_like(x_pair))
                y_pair = y_pair + _dot(Wm, xm)
            if add_skip:
                y_pair = y_pair + dskip_e[:, col:col + LANES] * xs[:, col:col + LANES]
            o_ref[:, col:col + LANES] = y_pair.astype(o_ref.dtype)
        BgT = _dot_nt(eye, Bg).astype(BF16)
        st_ref[:, gs] = dec_ch_e[:, gs] * Sg + _dot(BgT, z_st[:, gs])


def _ssd_dir(P, dtp, conv_w, conv_b, sp, *, rev, d, B, ncc, nlc, add_skip):
    M = P.shape[0]
    XW = SSD_WIDTH + 2 * SSD_GROUPS * SSD_STATE
    blk, main, prev, nxt = _seq_specs(XW, COL_XBC, B=B, ncc=ncc, nlc=nlc, rev=rev, n_rows=M)
    full2 = lambda shape: pl.BlockSpec(shape, lambda b, i: (0, 0))
    return pl.pallas_call(
        functools.partial(_ssd_kernel, rev=rev, d=d, ncc=ncc, nlc=nlc, add_skip=add_skip),
        grid=(B, ncc + nlc),
        in_specs=[main, prev, nxt,
                  pl.BlockSpec((SEQ_CHUNK, LANES), lambda b, i: (blk(b, i), 0)),
                  full2((4, XW)), full2((1, XW)), full2((8, LANES))],
        out_specs=pl.BlockSpec((SEQ_CHUNK, SSD_WIDTH), lambda b, i: (blk(b, i), 0)),
        out_shape=jax.ShapeDtypeStruct((M, SSD_WIDTH), BF16),
        scratch_shapes=[pltpu.VMEM((SSD_STATE, SSD_WIDTH), F32)],
        compiler_params=_cparams("parallel", "arbitrary"),
        name="ssd_bwd" if rev else "ssd_fwd",
    )(P, P, P, dtp, conv_w, conv_b, sp)


def _rope_kernel(q_ref, k_ref, c_ref, sa_ref, sb_ref, qo_ref, ko_ref, *, scale):
    W = q_ref.shape[1]
    reps = W // LANES
    cos = jnp.concatenate([c_ref[...]] * reps, axis=1)
    sa = jnp.concatenate([sa_ref[...]] * reps, axis=1)
    sb = jnp.concatenate([sb_ref[...]] * reps, axis=1)
    quarter = ATT_QK_DIM // 4

    def rot(x):
        return x * cos + pltpu.roll(x, W - quarter, 1) * sa + pltpu.roll(x, quarter, 1) * sb

    qo_ref[...] = (rot(q_ref[...].astype(F32)) * scale).astype(qo_ref.dtype)
    ko_ref[...] = rot(k_ref[...].astype(F32)).astype(ko_ref.dtype)


def _rope(P, cos, sa, sb, *, tm):
    M = P.shape[0]
    W = ATT_HEADS * 2 * ATT_QK_DIM
    tab = pl.BlockSpec((tm, LANES), lambda i: (i, 0))
    return pl.pallas_call(
        functools.partial(_rope_kernel, scale=ATT_QK_DIM ** -0.5),
        grid=(M // tm,),
        in_specs=[pl.BlockSpec((tm, W), lambda i: (i, COL_Q)),
                  pl.BlockSpec((tm, W), lambda i: (i, COL_K)), tab, tab, tab],
        out_specs=[pl.BlockSpec((tm, W), lambda i: (i, 0))] * 2,
        out_shape=[jax.ShapeDtypeStruct((M, W), BF16)] * 2,
        compiler_params=_cparams("parallel"),
        name="rope",
    )(P, P, cos, sa, sb)


def _attn_kernel(*refs, tq, tk, n_ctx, n_lat, lam_init):
    if n_lat:
        (q_ref, kc_ref, vc_ref, kl_ref, vl_ref, lv_ref, nw_ref, o_ref, m_ref, l_ref, acc_ref) = refs
    else:
        (q_ref, kc_ref, vc_ref, lv_ref, nw_ref, o_ref, m_ref, l_ref, acc_ref) = refs
        kl_ref = vl_ref = None

    q = q_ref[...]
    lane = lax.broadcasted_iota(jnp.int32, (1, LANES), 1)
    zero = jnp.zeros_like(q)
    qq = jnp.concatenate([jnp.where(lane < ATT_QK_DIM, q, zero),
                          jnp.where(lane >= ATT_QK_DIM, q, zero)], axis=0)

    m_ref[...] = jnp.full(m_ref.shape, NEG_BIG, F32)
    l_ref[...] = jnp.zeros_like(l_ref)
    acc_ref[...] = jnp.zeros_like(acc_ref)

    def chunk(kc, vc):
        s = _dot_nt(qq, kc)
        m_prev = m_ref[...]
        m_new = jnp.maximum(m_prev, jnp.max(s, axis=1, keepdims=True))
        alpha = jnp.exp(m_prev - m_new)
        p = jnp.exp(s - m_new[:, 0:1])
        l_ref[...] = alpha * l_ref[...] + jnp.sum(p, axis=1, keepdims=True)
        acc_ref[...] = alpha * acc_ref[...] + _dot(p.astype(BF16), vc)
        m_ref[...] = m_new

    for j in range(n_ctx // tk):
        chunk(kc_ref[j * tk:(j + 1) * tk, :], vc_ref[j * tk:(j + 1) * tk, :])
    if n_lat:
        def body(j, carry):
            rows = pl.ds(pl.multiple_of(j * tk, tk), tk)
            chunk(kl_ref[rows, :], vl_ref[rows, :])
            return carry
        lax.fori_loop(0, n_lat // tk, body, 0)

    lv = lv_ref[...]
    s01 = jnp.sum(lv[0:1, :] * lv[1:2, :], axis=1, keepdims=True)
    s23 = jnp.sum(lv[2:3, :] * lv[3:4, :], axis=1, keepdims=True)
    lam = jnp.exp(s01) - jnp.exp(s23) + lam_init
    acc = acc_ref[...]
    l = l_ref[...]
    o = acc[:tq, :] / l[:tq, :] - lam * (acc[tq:, :] / l[tq:, :])
    o = o * lax.rsqrt(jnp.mean(o * o, axis=-1, keepdims=True) + LN_EPS) * nw_ref[...]
    o_ref[...] = (o * (1.0 - lam_init)).astype(o_ref.dtype)


def _attention(qr, kr, P, lam_vecs, norm_w, out_prev, *, B, S, CTX, lam_init, ctx_queries):
    M = qr.shape[0]
    H = ATT_HEADS
    tk = min(CTX, 256)
    vcol = COL_V * (1024 // LANES)
    n_lat_blk = (B * S) // CTX
    if ctx_queries:
        tq = min(CTX, 256)
        nq = CTX // tq
        q_spec = pl.BlockSpec((tq, LANES), lambda b, h, i: ((B * S + b * CTX) // tq + i, h))
    else:
        tq = 256
        nq = S // tq
        q_spec = pl.BlockSpec((tq, LANES), lambda b, h, i: (b * nq + i, h))
    in_specs = [q_spec,
                pl.BlockSpec((CTX, LANES), lambda b, h, i: (n_lat_blk + b, h)),
                pl.BlockSpec((CTX, LANES), lambda b, h, i: (n_lat_blk + b, vcol + h))]
    args = [qr, kr, P]
    if not ctx_queries:
        in_specs += [pl.BlockSpec((S, LANES), lambda b, h, i: (b, h)),
                     pl.BlockSpec((S, LANES), lambda b, h, i: (b, vcol + h))]
        args += [kr, P]
    in_specs += [pl.BlockSpec((4, ATT_QK_DIM), lambda b, h, i: (0, 0)),
                 pl.BlockSpec((1, LANES), lambda b, h, i: (0, 0))]
    args += [lam_vecs, norm_w]
    kwargs = {}
    if out_prev is not None:
        in_specs.append(pl.BlockSpec(memory_space=pl.ANY))
        args.append(out_prev)
        kwargs["input_output_aliases"] = {len(args) - 1: 0}

    kernel = functools.partial(_attn_kernel, tq=tq, tk=tk, n_ctx=CTX, n_lat=0 if ctx_queries else S,
                               lam_init=lam_init)
    if out_prev is not None:
        base = kernel
        kernel = lambda *refs: base(*refs[:len(args) - 1], *refs[len(args):])
    return pl.pallas_call(
        kernel,
        grid=(B, H, nq),
        in_specs=in_specs,
        out_specs=q_spec,
        out_shape=jax.ShapeDtypeStruct((M, H * ATT_V_DIM), BF16),
        scratch_shapes=[pltpu.VMEM((2 * tq, LANES), F32)] * 3,
        compiler_params=_cparams("parallel", "parallel", "arbitrary"),
        name="attn_ctx" if ctx_queries else "attn_lat",
        **kwargs,
    )(*args)


def _merge_kernel(hf_ref, hb_ref, gl_ref, yf_ref, yb_ref, z_ref, att_ref, g0_ref, g1_ref, g2_ref,
                  wb_ref, nw_ref, o_ref, br_ref):
    @pl.when(pl.program_id(1) == 0)
    def _():
        y = hf_ref[...].astype(F32) + hb_ref[...].astype(F32)
        g = gl_ref[...].astype(F32)
        gelu = 0.5 * g * (1.0 + jnp.tanh(0.7978845608028654 * (g + 0.044715 * (g * g * g))))
        br_ref[0] = (y * gelu).astype(BF16)
        ys = yf_ref[...].astype(F32) + yb_ref[...].astype(F32)
        zz = z_ref[...].astype(F32)
        t = ys * (zz * _sigmoid(zz))
        t = t * lax.rsqrt(jnp.mean(t * t, axis=-1, keepdims=True) + LN_EPS) * nw_ref[...]
        br_ref[1] = t.astype(BF16)
        br_ref[2] = att_ref[...]

    acc = _sigmoid(g0_ref[...].astype(F32)) * _dot(br_ref[0], wb_ref[0])
    acc = acc + _sigmoid(g1_ref[...].astype(F32)) * _dot(br_ref[1], wb_ref[1])
    acc = acc + _sigmoid(g2_ref[...].astype(F32)) * _dot(br_ref[2], wb_ref[2])
    o_ref[...] = acc.astype(o_ref.dtype)


def _merge(hf, hb, yf, yb, att, P, wb, ssd_norm_w, *, n_rows, tm, tn):
    M = P.shape[0]
    D = wb.shape[2]
    BW = wb.shape[1]
    row = lambda col: pl.BlockSpec((tm, BW), lambda i, j: (i, col))
    gcol = COL_GATES * 1024 // tn
    gate = lambda k: pl.BlockSpec((tm, tn), lambda i, j: (i, gcol + k * (D // tn) + j))
    return pl.pallas_call(
        _merge_kernel,
        grid=(n_rows // tm, D // tn),
        in_specs=[row(0), row(0), row(COL_LRU_G), row(0), row(0), row(COL_Z), row(0),
                  gate(0), gate(1), gate(2),
                  pl.BlockSpec((3, BW, tn), lambda i, j: (0, 0, j)),
                  pl.BlockSpec((1, BW), lambda i, j: (0, 0))],
        out_specs=pl.BlockSpec((tm, tn), lambda i, j: (i, j)),
        out_shape=jax.ShapeDtypeStruct((M, D), BF16),
        scratch_shapes=[pltpu.VMEM((3, tm, BW), BF16)],
        compiler_params=_cparams("parallel", "arbitrary"),
        name="merge",
    )(hf, hb, P, yf, yb, P, att, P, P, P, wb, ssd_norm_w)


def _route(logits, rbias):
    G, E = N_EXPERT_GROUPS, EXPERTS_PER_GROUP
    mx = jnp.max(logits, axis=0, keepdims=True)
    ex = jnp.exp(logits - mx)
    sc = ex / jnp.sum(ex, axis=0, keepdims=True)
    sel = sc + rbias
    srow = [sel[r:r + 1, :] for r in range(G * E)]
    prow = [sc[r:r + 1, :] for r in range(G * E)]
    gscore = []
    for g in range(G):
        v = srow[g * E:(g + 1) * E]
        best = v[0] + v[1]
        for a in range(E):
            for b in range(a + 1, E):
                if (a, b) != (0, 1):
                    best = jnp.maximum(best, v[a] + v[b])
        gscore.append(best)
    gmax = functools.reduce(jnp.maximum, gscore)
    gidx = jnp.full(gmax.shape, G - 1, jnp.int32)
    for g in range(G - 2, -1, -1):
        gidx = jnp.where(gscore[g] == gmax, g, gidx)

    def pick(rows, k):
        out = rows[(G - 1) * E + k]
        for g in range(G - 2, -1, -1):
            out = jnp.where(gidx == g, rows[g * E + k], out)
        return out

    v = [pick(srow, k) for k in range(E)]
    p = [pick(prow, k) for k in range(E)]
    chosen = []
    for k in range(E):
        rank = jnp.zeros(gmax.shape, jnp.int32)
        for j in range(E):
            if j == k:
                continue
            ahead = (v[j] > v[k]) | ((v[j] == v[k]) & (j < k)) if j < k else (v[j] > v[k])
            rank = rank + jnp.where(ahead, 1, 0)
        chosen.append(rank < 2)
    first = jnp.full(gmax.shape, E, jnp.int32)
    second = jnp.full(gmax.shape, -1, jnp.int32)
    for k in range(E):
        first = jnp.minimum(first, jnp.where(chosen[k], k, E))
        second = jnp.maximum(second, jnp.where(chosen[k], k, -1))
    p_first = jnp.zeros(gmax.shape, F32)
    p_second = jnp.zeros(gmax.shape, F32)
    for k in range(E):
        p_first = p_first + jnp.where(first == k, p[k], 0.0)
        p_second = p_second + jnp.where(second == k, p[k], 0.0)
    tot = p_first + p_second
    return gidx * E + first, gidx * E + second, p_first / tot, p_second / tot


def _outproj_kernel(m_ref, w_ref, x_ref, mod_ref, lng_ref, lnb_ref, wr_ref, rb_ref,
                    x1_ref, h2_ref, ids_ref, wts_ref, *, tm, alpha):
    o = _dot(m_ref[...], w_ref[...])
    g1 = mod_ref[0, 2:3, :]
    sh2 = mod_ref[0, 3:4, :]
    sc2 = mod_ref[0, 4:5, :]
    x1 = _ln(alpha * x_ref[...] + g1 * o) * lng_ref[...] + lnb_ref[...]
    x1_ref[...] = x1
    h2 = _ln(x1) * (1.0 + sc2) + sh2
    for k in range(SLAB):
        h2_ref[pl.ds(k, tm, stride=SLAB), :] = h2[:, k * LANES:(k + 1) * LANES]
    h_hi, h_lo = _split2(h2)
    w_hi, w_lo = _split2(wr_ref[...])
    logits = _dot_nt(w_hi, h_hi) + _dot_nt(w_lo, h_hi) + _dot_nt(w_hi, h_lo)
    e0, e1, w0, w1 = _route(logits, rb_ref[...])
    row = lax.broadcasted_iota(jnp.int32, (8, tm), 0)
    ids_ref[...] = jnp.where(row == 0, e0, jnp.where(row == 1, e1, 0))
    wts_ref[...] = jnp.where(row == 0, w0, jnp.where(row == 1, w1, 0.0))


def _outproj(m, w_out, x, mod, ln_g, ln_b, wr_t, rbias, *, n_rows, tm, n_lat, S, alpha):
    M, D = x.shape

    def midx(i):
        return jnp.where(i * tm < n_lat, 1 + (i * tm) // S, 0)

    vec = pl.BlockSpec((1, D), lambda i: (0, 0))
    return pl.pallas_call(
        functools.partial(_outproj_kernel, tm=tm, alpha=alpha),
        grid=(n_rows // tm,),
        in_specs=[pl.BlockSpec((tm, D), lambda i: (i, 0)),
                  pl.BlockSpec((D, D), lambda i: (0, 0)),
                  pl.BlockSpec((tm, D), lambda i: (i, 0)),
                  pl.BlockSpec((1, 6, D), lambda i: (midx(i), 0, 0)),
                  vec, vec,
                  pl.BlockSpec((N_EXPERTS, D), lambda i: (0, 0)),
                  pl.BlockSpec((N_EXPERTS, 1), lambda i: (0, 0))],
        out_specs=[pl.BlockSpec((tm, D), lambda i: (i, 0)),
                   pl.BlockSpec((tm * SLAB, LANES), lambda i: (i, 0)),
                   pl.BlockSpec((8, tm), lambda i: (0, i)),
                   pl.BlockSpec((8, tm), lambda i: (0, i))],
        out_shape=[jax.ShapeDtypeStruct((M, D), F32),
                   jax.ShapeDtypeStruct((n_rows * SLAB, LANES), F32),
                   jax.ShapeDtypeStruct((8, n_rows), jnp.int32),
                   jax.ShapeDtypeStruct((8, n_rows), F32)],
        compiler_params=_cparams("parallel"),
        name="outproj",
    )(m, w_out, x, mod, ln_g, ln_b, wr_t, rbias)


def _gather_kernel(idx_ref, nv_ref, src_ref, out_ref, sem, *, rows):
    base = pl.program_id(0) * rows

    def copy(src_tok, dst_tok):
        return pltpu.make_async_copy(
            src_ref.at[pl.ds(pl.multiple_of(src_tok * SLAB, SLAB), SLAB)],
            out_ref.at[pl.ds(pl.multiple_of(dst_tok * SLAB, SLAB), SLAB)], sem)

    @pl.when(base < nv_ref[0])
    def _():
        def issue(r, carry):
            copy(idx_ref[base + r], base + r).start()
            return carry
        lax.fori_loop(0, rows, issue, 0)

        def drain(r, carry):
            copy(0, base + r).wait()
            return carry
        lax.fori_loop(0, rows, drain, 0)


def _gather(src, idx, n_valid, *, rows=256):
    n_out = idx.shape[0]
    return pl.pallas_call(
        functools.partial(_gather_kernel, rows=rows),
        grid_spec=pltpu.PrefetchScalarGridSpec(
            num_scalar_prefetch=2,
            grid=(n_out // rows,),
            in_specs=[pl.BlockSpec(memory_space=pl.ANY)],
            out_specs=pl.BlockSpec(memory_space=pl.ANY),
            scratch_shapes=[pltpu.SemaphoreType.DMA(())]),
        out_shape=jax.ShapeDtypeStruct((n_out * SLAB, LANES), src.dtype),
        compiler_params=_cparams("arbitrary"),
        name="gather",
    )(idx, n_valid, src)


def _ffn_kernel(te_ref, tv_ref, xs_ref, ws_ref, wg_ref, wu_ref, wd_ref, ys_ref, xb_ref, *, tm):
    @pl.when(tv_ref[pl.program_id(0)] != 0)
    def _():
        for k in range(SLAB):
            xb_ref[:, k * LANES:(k + 1) * LANES] = xs_ref[pl.ds(k, tm, stride=SLAB), :].astype(BF16)
        x = xb_ref[...]
        g = _dot(x, wg_ref[0])
        u = _dot(x, wu_ref[0])
        act = (g * _sigmoid(g) * u).astype(BF16)
        y = _dot(act, wd_ref[0]) * ws_ref[...]
        for k in range(SLAB):
            ys_ref[pl.ds(k, tm, stride=SLAB), :] = y[:, k * LANES:(k + 1) * LANES]


def _expert_ffn(xs, w_sorted, tile_e, tile_v, wg, wu, wd, *, tm):
    n_tiles = tile_e.shape[0]
    E, D, F = wg.shape
    return pl.pallas_call(
        functools.partial(_ffn_kernel, tm=tm),
        grid_spec=pltpu.PrefetchScalarGridSpec(
            num_scalar_prefetch=2,
            grid=(n_tiles,),
            in_specs=[pl.BlockSpec((tm * SLAB, LANES), lambda i, te, tv: (i, 0)),
                      pl.BlockSpec((tm, 1), lambda i, te, tv: (i, 0)),
                      pl.BlockSpec((1, D, F), lambda i, te, tv: (te[i], 0, 0)),
                      pl.BlockSpec((1, D, F), lambda i, te, tv: (te[i], 0, 0)),
                      pl.BlockSpec((1, F, D), lambda i, te, tv: (te[i], 0, 0))],
            out_specs=pl.BlockSpec((tm * SLAB, LANES), lambda i, te, tv: (i, 0)),
            scratch_shapes=[pltpu.VMEM((tm, D), BF16)]),
        out_shape=jax.ShapeDtypeStruct((n_tiles * tm * SLAB, LANES), F32),
        compiler_params=_cparams("arbitrary"),
        name="expert_ffn",
    )(tile_e, tile_v, xs, w_sorted, wg, wu, wd)


def _final_kernel(x_ref, y0_ref, y1_ref, mod_ref, lng_ref, lnb_ref, o_ref, f_ref, *, tm, alpha):
    for k in range(SLAB):
        rows = pl.ds(k, tm, stride=SLAB)
        f_ref[:, k * LANES:(k + 1) * LANES] = y0_ref[rows, :] + y1_ref[rows, :]
    g2 = mod_ref[0, 5:6, :]
    o_ref[...] = _ln(alpha * x_ref[...] + g2 * f_ref[...]) * lng_ref[...] + lnb_ref[...]


def _final(x1, yg, mod, ln_g, ln_b, *, n_rows, tm, n_lat, S, alpha):
    D = x1.shape[1]

    def midx(i):
        return jnp.where(i * tm < n_lat, 1 + (i * tm) // S, 0)

    nt = n_rows // tm
    vec = pl.BlockSpec((1, D), lambda i: (0, 0))
    return pl.pallas_call(
        functools.partial(_final_kernel, tm=tm, alpha=alpha),
        grid=(nt,),
        in_specs=[pl.BlockSpec((tm, D), lambda i: (i, 0)),
                  pl.BlockSpec((tm * SLAB, LANES), lambda i: (i, 0)),
                  pl.BlockSpec((tm * SLAB, LANES), lambda i: (nt + i, 0)),
                  pl.BlockSpec((1, 6, D), lambda i: (midx(i), 0, 0)),
                  vec, vec],
        out_specs=pl.BlockSpec((tm, D), lambda i: (i, 0)),
        out_shape=jax.ShapeDtypeStruct((n_rows, D), F32),
        scratch_shapes=[pltpu.VMEM((tm, D), F32)],
        compiler_params=_cparams("parallel"),
        name="final",
    )(x1, yg, yg, mod, ln_g, ln_b)


def _rope_tables(B, S, CTX):
    t = jnp.arange(S)
    row = (t // GRID_W).astype(F32)
    col = (t % GRID_W).astype(F32)
    axis_dim = ATT_QK_DIM // 2
    freqs = ROPE_THETA ** (-jnp.arange(0, axis_dim, 2, dtype=F32) / axis_dim)
    ang = jnp.stack([row[:, None] * freqs, col[:, None] * freqs], axis=1)
    cos, sin = jnp.cos(ang), jnp.sin(ang)
    zero = jnp.zeros_like(sin)
    c64 = jnp.concatenate([cos, cos], axis=-1).reshape(S, ATT_QK_DIM)
    sa64 = jnp.concatenate([-sin, zero], axis=-1).reshape(S, ATT_QK_DIM)
    sb64 = jnp.concatenate([zero, sin], axis=-1).reshape(S, ATT_QK_DIM)

    def full(tab, ctx_val):
        lat = jnp.tile(jnp.concatenate([tab, tab], axis=-1), (B, 1))
        ctx = jnp.full((B * CTX, LANES), ctx_val, F32)
        return jnp.concatenate([lat, ctx], axis=0)

    return full(c64, 1.0), full(sa64, 0.0), full(sb64, 0.0)


def _moe_plan(ids, wts, tm):
    n_tok = ids.shape[1]
    e_flat = ids.reshape(-1)
    w_flat = wts.reshape(-1)
    onehot = (e_flat[:, None] == jnp.arange(N_EXPERTS, dtype=jnp.int32)[None, :]).astype(jnp.int32)
    csum = jnp.cumsum(onehot, axis=0)
    counts = csum[-1]
    rank = jnp.take_along_axis(csum, e_flat[:, None], axis=1)[:, 0] - 1
    padded = ((counts + tm - 1) // tm) * tm
    ends = jnp.cumsum(padded)
    starts = ends - padded
    pos = (starts[e_flat] + rank).astype(jnp.int32)
    n_slots = 2 * n_tok + N_EXPERTS * tm
    tok = jnp.tile(jnp.arange(n_tok, dtype=jnp.int32), 2)
    src_tok = jnp.zeros((n_slots,), jnp.int32).at[pos].set(tok)
    w_sorted = jnp.zeros((n_slots,), F32).at[pos].set(w_flat).reshape(n_slots, 1)
    tile_start = jnp.arange(n_slots // tm, dtype=jnp.int32) * tm
    tile_e = jnp.minimum(jnp.searchsorted(ends, tile_start, side="right"), N_EXPERTS - 1).astype(jnp.int32)
    tile_v = (tile_start < ends[-1]).astype(jnp.int32)
    return pos, src_tok, w_sorted, tile_e, tile_v, ends[-1:].astype(jnp.int32)


def _row_tile(limit, *dims):
    t = limit
    while any(d % t for d in dims):
        t //= 2
    return t


def kernel(x, c, ctx, c_ctx, w_ada, b_ada, w_in, lru_conv_w, lru_conv_b, lru_gate_w, lru_gate_b, lru_lambda, ssd_conv_w, ssd_conv_b, ssd_a_log, ssd_dt_bias, ssd_d, ssd_norm_w, att_lambda, att_norm_w, w_branch, w_out, ln_g, ln_b, w_router, router_bias, w_gate_e, w_up_e, w_down_e):
    B, S, D = x.shape
    CTX = ctx.shape[1]
    depth = w_ada.shape[0]
    alpha = (2 * depth) ** 0.25
    n_lat = B * S
    n_ctx = B * CTX
    M = n_lat + n_ctx
    assert D == SLAB * LANES and CTX % SEQ_CHUNK == 0 and S % SEQ_CHUNK == 0 and S % CTX == 0
    ncc, nlc = CTX // SEQ_CHUNK, S // SEQ_CHUNK

    xs = jnp.concatenate([x.reshape(n_lat, D), ctx.reshape(n_ctx, D)], axis=0)
    cvec = jnp.concatenate([c_ctx[None, :], c, jnp.zeros((8 - 1 - B, D), F32)], axis=0)
    cos_t, sa_t, sb_t = _rope_tables(B, S, CTX)
    wr_t = w_router.T
    rbias = router_bias.reshape(N_EXPERTS, 1)

    tm_big = _row_tile(1024, S, n_ctx)
    tm_mid = _row_tile(512, S, n_ctx)
    tm_small = _row_tile(256, S, n_ctx)

    i0, i1, i2, i3, i4, i5, i6, i7, i8 = [int(v) for v in
                                          (0, 1024, 2048, 3072, 5120, 5152, 6176, 7200, 8224)]
    for l in range(depth):
        need_ctx = l < depth - 1
        lam_init = 0.8 - 0.6 * math.exp(-0.3 * l)
        n_rows = M if need_ctx else n_lat

        wl = w_in[l]
        w_main = jnp.concatenate([wl[:, i3:i4], wl[:, i0:i3], wl[:, i5:]], axis=1).astype(BF16)
        w_dt = jnp.pad(wl[:, i4:i5], ((0, 0), (0, LANES - (i5 - i4)))).astype(BF16)
        gw = lru_gate_w[l]
        wg_lru = jnp.concatenate([gw[:, 0], gw[:, 1]], axis=-1).astype(BF16)

        mod = _ada(cvec, w_ada[l], b_ada[l]).reshape(8, 6, D)
        P, dtp = _inproj(xs, mod, w_main, w_dt, tm=tm_big, tn=1024, n_lat=n_lat, S=S)

        h_dirs, y_dirs = [], []
        for d in range(2):
            rev = d == 1
            h_dirs.append(_lru_dir(P, lru_conv_w[l], lru_conv_b[l].reshape(1, -1), wg_lru[d],
                                   lru_gate_b[l, d], lru_lambda[l, d].reshape(1, -1),
                                   rev=rev, B=B, ncc=ncc, nlc=nlc))
            pad = ((0, 0), (SSD_HEADS * d, LANES - SSD_HEADS * (d + 1)))
            sp = jnp.concatenate([jnp.pad(ssd_dt_bias[l, d][None, :], pad),
                                  jnp.pad(ssd_a_log[l, d][None, :], pad),
                                  jnp.pad(ssd_d[l][None, :], pad),
                                  jnp.zeros((5, LANES), F32)], axis=0)
            y_dirs.append(_ssd_dir(P, dtp, ssd_conv_w[l], ssd_conv_b[l].reshape(1, -1), sp,
                                   rev=rev, d=d, B=B, ncc=ncc, nlc=nlc, add_skip=(d == 0)))

        qr, kr = _rope(P, cos_t, sa_t, sb_t, tm=tm_mid)
        nw_att = att_norm_w[l].reshape(1, LANES)
        att = _attention(qr, kr, P, att_lambda[l], nw_att, None, B=B, S=S, CTX=CTX,
                         lam_init=lam_init, ctx_queries=False)
        if need_ctx:
            att = _attention(qr, kr, P, att_lambda[l], nw_att, att, B=B, S=S, CTX=CTX,
                             lam_init=lam_init, ctx_queries=True)

        m = _merge(h_dirs[0], h_dirs[1], y_dirs[0], y_dirs[1], att, P, w_branch[l].astype(BF16),
                   ssd_norm_w[l].reshape(1, -1), n_rows=n_rows, tm=tm_mid, tn=512)
        x1, h2, ids, wts = _outproj(m, w_out[l].astype(BF16), xs, mod, ln_g[l, 0].reshape(1, D),
                                    ln_b[l, 0].reshape(1, D), wr_t, rbias, n_rows=n_rows,
                                    tm=tm_small, n_lat=n_lat, S=S, alpha=alpha)

        pos, src_tok, w_sorted, tile_e, tile_v, n_valid = _moe_plan(ids[:2], wts[:2], MOE_TILE)
        x_sorted = _gather(h2, src_tok, n_valid)
        y_sorted = _expert_ffn(x_sorted, w_sorted, tile_e, tile_v, w_gate_e[l].astype(BF16),
                               w_up_e[l].astype(BF16), w_down_e[l].astype(BF16), tm=MOE_TILE)
        yg = _gather(y_sorted, pos, jnp.full((1,), 2 * n_rows, jnp.int32))
        xs = _final(x1, yg, mod, ln_g[l, 1].reshape(1, D), ln_b[l, 1].reshape(1, D), n_rows=n_rows,
                    tm=tm_small, n_lat=n_lat, S=S, alpha=alpha)

    return xs[:n_lat].reshape(B, S, D)
```

```python
import functools
import math

import jax
import jax.numpy as jnp
from jax import lax
from jax.experimental import pallas as pl
from jax.experimental.pallas import tpu as pltpu

F32 = jnp.float32
BF16 = jnp.bfloat16

GRID_W = 64
ATT_HEADS = 8
ATT_QK_DIM = 64
ATT_V_DIM = 128
ROPE_THETA = 10000.0
SSD_HEADS = 16
SSD_HEAD_DIM = 64
SSD_GROUPS = 4
SSD_STATE = 128
SSD_WIDTH = SSD_HEADS * SSD_HEAD_DIM
LRU_WIDTH = 1024
LRU_BLOCKS = 8
LRU_BLOCK_DIM = LRU_WIDTH // LRU_BLOCKS
LRU_C = 8.0
N_EXPERTS = 16
N_EXPERT_GROUPS = 4
EXPERTS_PER_GROUP = 4
LN_EPS = 1e-6

LANES = 128
SLAB = 16
SEQ_CHUNK = 256
HALO = 16
MOE_TILE = 256
NEG_BIG = -1e30
VMEM_LIMIT = 56 * 2 ** 20

COL_XBC, COL_LRU_X, COL_LRU_G, COL_Z, COL_Q, COL_K, COL_V, COL_GATES = 0, 2, 3, 4, 5, 6, 7, 8


def _cparams(*sem):
    return pltpu.CompilerParams(dimension_semantics=sem, vmem_limit_bytes=VMEM_LIMIT)


def _dot(a, b):
    return jnp.dot(a, b, preferred_element_type=F32)


def _dot_nt(a, b):
    return lax.dot_general(a, b, (((1,), (1,)), ((), ())), preferred_element_type=F32)


def _split3(v):
    hi = v.astype(BF16)
    r = v - hi.astype(F32)
    mid = r.astype(BF16)
    lo = (r - mid.astype(F32)).astype(BF16)
    return hi, mid, lo


def _split2(v):
    hi = v.astype(BF16)
    lo = (v - hi.astype(F32)).astype(BF16)
    return hi, lo


def _ln(x):
    mu = jnp.mean(x, axis=-1, keepdims=True)
    xc = x - mu
    var = jnp.mean(xc * xc, axis=-1, keepdims=True)
    return xc * lax.rsqrt(var + LN_EPS)


def _sigmoid(x):
    return jax.nn.sigmoid(x)


def _softplus(x):
    return jnp.maximum(x, 0.0) + jnp.log1p(jnp.exp(-jnp.abs(x)))


def _ada_kernel(c_ref, w_ref, b_ref, o_ref):
    c = c_ref[...]
    a = c * _sigmoid(c)
    a_hi, a_lo = _split2(a)
    w_hi, w_lo = _split2(w_ref[...])
    o_ref[...] = _dot(a_hi, w_hi) + _dot(a_lo, w_hi) + _dot(a_hi, w_lo) + b_ref[...]


def _ada(cvec, w, b):
    R, D = cvec.shape
    N = w.shape[1]
    tn = 1024
    return pl.pallas_call(
        _ada_kernel,
        grid=(N // tn,),
        in_specs=[pl.BlockSpec((R, D), lambda j: (0, 0)),
                  pl.BlockSpec((D, tn), lambda j: (0, j)),
                  pl.BlockSpec((1, tn), lambda j: (0, j))],
        out_specs=pl.BlockSpec((R, tn), lambda j: (0, j)),
        out_shape=jax.ShapeDtypeStruct((R, N), F32),
        compiler_params=_cparams("arbitrary"),
        name="ada",
    )(cvec, w, b.reshape(1, N))


def _inproj_kernel(x_ref, mod_ref, w_ref, wdt_ref, o_ref, dt_ref, h_ref, *, tm, rb):
    @pl.when(pl.program_id(1) == 0)
    def _():
        sh = mod_ref[0, 0:1, :]
        sc = mod_ref[0, 1:2, :]

        def body(r, carry):
            rows = pl.ds(pl.multiple_of(r * rb, rb), rb)
            h = _ln(x_ref[rows, :]) * (1.0 + sc) + sh
            hb = h.astype(BF16)
            h_ref[rows, :] = hb
            dt_ref[rows, :] = _dot(hb, wdt_ref[...])
            return carry

        lax.fori_loop(0, tm // rb, body, 0)

    o_ref[...] = _dot(h_ref[...], w_ref[...]).astype(o_ref.dtype)


def _inproj(x, mod, w_main, w_dt, *, tm, tn, n_lat, S):
    M, D = x.shape
    N = w_main.shape[1]

    def midx(i):
        return jnp.where(i * tm < n_lat, 1 + (i * tm) // S, 0)

    return pl.pallas_call(
        functools.partial(_inproj_kernel, tm=tm, rb=min(tm, 256)),
        grid=(M // tm, N // tn),
        in_specs=[pl.BlockSpec((tm, D), lambda i, j: (i, 0)),
                  pl.BlockSpec((1, 6, D), lambda i, j: (midx(i), 0, 0)),
                  pl.BlockSpec((D, tn), lambda i, j: (0, j)),
                  pl.BlockSpec((D, LANES), lambda i, j: (0, 0))],
        out_specs=[pl.BlockSpec((tm, tn), lambda i, j: (i, j)),
                   pl.BlockSpec((tm, LANES), lambda i, j: (i, 0))],
        out_shape=[jax.ShapeDtypeStruct((M, N), BF16), jax.ShapeDtypeStruct((M, LANES), F32)],
        scratch_shapes=[pltpu.VMEM((tm, D), BF16)],
        compiler_params=_cparams("parallel", "arbitrary"),
        name="inproj",
    )(x, mod, w_main, w_dt)


def _chunk_of_step(i, ncc, nlc, rev):
    if not rev:
        return i
    return jnp.where(i < ncc, ncc - 1 - i, ncc + nlc - 1 - (i - ncc))


def _block_of_chunk(b, c, B, ncc, nlc):
    return jnp.where(c < ncc, B * nlc + b * ncc + c, b * nlc + (c - ncc))


def _seq_specs(width, col, *, B, ncc, nlc, rev, n_rows):
    hb = SEQ_CHUNK // HALO
    last = n_rows // HALO - 1

    def blk(b, i):
        return _block_of_chunk(b, _chunk_of_step(i, ncc, nlc, rev), B, ncc, nlc)

    main = pl.BlockSpec((SEQ_CHUNK, width), lambda b, i: (blk(b, i), col))
    prev = pl.BlockSpec((HALO, width), lambda b, i: (jnp.maximum(blk(b, i) * hb - 1, 0), col))
    nxt = pl.BlockSpec((HALO, width), lambda b, i: (jnp.minimum((blk(b, i) + 1) * hb, last), col))
    return blk, main, prev, nxt


def _dwconv_chunk(x_ref, xp_ref, xn_ref, cw_ref, cb_ref, c, ncc, nch):
    T = SEQ_CHUNK
    x = x_ref[...].astype(F32)
    has_prev = jnp.where((c == 0) | (c == ncc), 0.0, 1.0)
    has_next = jnp.where((c == ncc - 1) | (c == nch - 1), 0.0, 1.0)
    p6 = xp_ref[HALO - 2:HALO - 1, :].astype(F32) * has_prev
    p7 = xp_ref[HALO - 1:HALO, :].astype(F32) * has_prev
    n0 = xn_ref[0:1, :].astype(F32) * has_next
    row = lax.broadcasted_iota(jnp.int32, (T, 1), 0)
    xm1 = jnp.where(row == 0, p7, pltpu.roll(x, 1, 0))
    xm2 = jnp.where(row == 0, p6, jnp.where(row == 1, p7, pltpu.roll(x, 2, 0)))
    xp1 = jnp.where(row == T - 1, n0, pltpu.roll(x, T - 1, 0))
    return (cw_ref[0:1, :] * xm2 + cw_ref[1:2, :] * xm1 + cw_ref[2:3, :] * x
            + cw_ref[3:4, :] * xp1 + cb_ref[...])


def _linear_scan(a, b, rev):
    T = a.shape[0]
    row = lax.broadcasted_iota(jnp.int32, (T, 1), 0)
    s = 1
    while s < T:
        if rev:
            valid = row < T - s
            shift = T - s
        else:
            valid = row >= s
            shift = s
        a_p = jnp.where(valid, pltpu.roll(a, shift, 0), 1.0)
        b_p = jnp.where(valid, pltpu.roll(b, shift, 0), 0.0)
        b = a * b_p + b
        a = a * a_p
        s *= 2
    return a, b


def _lru_kernel(x_ref, xp_ref, xn_ref, cw_ref, cb_ref, wg_ref, gb_ref, lam_ref, o_ref, carry_ref,
                *, rev, ncc, nlc):
    T = SEQ_CHUNK
    i = pl.program_id(1)
    c = _chunk_of_step(i, ncc, nlc, rev)

    @pl.when(i == 0)
    def _():
        carry_ref[...] = jnp.zeros_like(carry_ref)

    xc = _dwconv_chunk(x_ref, xp_ref, xn_ref, cw_ref, cb_ref, c, ncc, ncc + nlc)
    sp = _softplus(-lam_ref[...])
    for n in range(LRU_BLOCKS):
        sl = slice(n * LRU_BLOCK_DIM, (n + 1) * LRU_BLOCK_DIM)
        xb = xc[:, sl]
        gl = _dot(xb.astype(BF16), wg_ref[n])
        r_gate = _sigmoid(gl[:, :LRU_BLOCK_DIM] + gb_ref[0:1, sl])
        i_gate = _sigmoid(gl[:, LRU_BLOCK_DIM:] + gb_ref[1:2, sl])
        log_a = -LRU_C * r_gate * sp[:, sl]
        a = jnp.exp(log_a)
        bb = jnp.sqrt(1.0 - a * a) * (i_gate * xb)
        a_s, b_s = _linear_scan(a, bb, rev)
        h = b_s + a_s * carry_ref[0:1, sl]
        o_ref[:, sl] = h.astype(o_ref.dtype)
        carry_ref[0:1, sl] = h[0:1, :] if rev else h[T - 1:T, :]


def _lru_dir(P, conv_w, conv_b, wg, gb, lam, *, rev, B, ncc, nlc):
    M = P.shape[0]
    W = LRU_WIDTH
    blk, main, prev, nxt = _seq_specs(W, COL_LRU_X, B=B, ncc=ncc, nlc=nlc, rev=rev, n_rows=M)
    full2 = lambda shape: pl.BlockSpec(shape, lambda b, i: (0, 0))
    return pl.pallas_call(
        functools.partial(_lru_kernel, rev=rev, ncc=ncc, nlc=nlc),
        grid=(B, ncc + nlc),
        in_specs=[main, prev, nxt, full2((4, W)), full2((1, W)),
                  pl.BlockSpec((LRU_BLOCKS, LRU_BLOCK_DIM, 2 * LRU_BLOCK_DIM), lambda b, i: (0, 0, 0)),
                  full2((2, W)), full2((1, W))],
        out_specs=pl.BlockSpec((SEQ_CHUNK, W), lambda b, i: (blk(b, i), 0)),
        out_shape=jax.ShapeDtypeStruct((M, W), BF16),
        scratch_shapes=[pltpu.VMEM((8, W), F32)],
        compiler_params=_cparams("parallel", "arbitrary"),
        name="lru_bwd" if rev else "lru_fwd",
    )(P, P, P, conv_w, conv_b, wg, gb, lam)


def _ssd_kernel(x_ref, xp_ref, xn_ref, dt_ref, cw_ref, cb_ref, sp_ref, o_ref, st_ref,
                *, rev, d, ncc, nlc, add_skip):
    T = SEQ_CHUNK
    N = SSD_STATE
    HPG = SSD_HEADS // SSD_GROUPS
    GW = HPG * SSD_HEAD_DIM
    i = pl.program_id(1)
    c = _chunk_of_step(i, ncc, nlc, rev)

    @pl.when(i == 0)
    def _():
        st_ref[...] = jnp.zeros_like(st_ref)

    xc = _dwconv_chunk(x_ref, xp_ref, xn_ref, cw_ref, cb_ref, c, ncc, ncc + nlc)
    xbc = xc * _sigmoid(xc)
    xs = xbc[:, :SSD_WIDTH]
    Bm = xbc[:, SSD_WIDTH:SSD_WIDTH + SSD_GROUPS * N]
    Cm = xbc[:, SSD_WIDTH + SSD_GROUPS * N:]

    lane = lax.broadcasted_iota(jnp.int32, (1, LANES), 1)
    in_dir = (lane >= SSD_HEADS * d) & (lane < SSD_HEADS * (d + 1))
    dt = jnp.where(in_dir, _softplus(dt_ref[...] + sp_ref[0:1, :]), 0.0)
    a_neg = jnp.where(in_dir, -jnp.exp(sp_ref[1:2, :]), 0.0)
    a = dt * a_neg

    r_i = lax.broadcasted_iota(jnp.int32, (T, T), 0)
    c_i = lax.broadcasted_iota(jnp.int32, (T, T), 1)
    tri = (c_i >= r_i) if rev else (c_i <= r_i)
    tri_b = jnp.where(tri, 1.0, 0.0).astype(BF16)
    cs = sum(_dot(tri_b, p) for p in _split3(a))
    e_r = lax.broadcasted_iota(jnp.int32, (LANES, LANES), 0)
    e_c = lax.broadcasted_iota(jnp.int32, (LANES, LANES), 1)
    eye = jnp.where(e_r == e_c, 1.0, 0.0).astype(BF16)
    cs_row = sum(_dot_nt(eye, p) for p in _split3(cs))
    cs_end = cs[0:1, :] if rev else cs[T - 1:T, :]

    x_r = lax.broadcasted_iota(jnp.int32, (LANES, SSD_WIDTH), 0)
    x_c = lax.broadcasted_iota(jnp.int32, (LANES, SSD_WIDTH), 1)
    expand_m = jnp.where(x_r == SSD_HEADS * d + x_c // SSD_HEAD_DIM, 1.0, 0.0).astype(BF16)

    def expand(v):
        return sum(_dot(p, expand_m) for p in _split3(v))

    dt_e = expand(dt)
    dec_out_e = expand(jnp.exp(cs))
    dec_st_e = expand(jnp.exp(cs_end - cs))
    row8 = lax.broadcasted_iota(jnp.int32, (8, LANES), 0)
    small = jnp.where(row8 == 0, jnp.exp(cs_end), jnp.where(row8 == 1, sp_ref[2:3, :], 0.0))
    small_e = expand(small)
    dec_ch_e = small_e[0:1, :]
    dskip_e = small_e[1:2, :]

    xdt = xs * dt_e
    xdt_b = xdt.astype(BF16)
    z_st = (xdt * dec_st_e).astype(BF16)
    lo_half = lax.broadcasted_iota(jnp.int32, (1, LANES), 1) < SSD_HEAD_DIM

    for g in range(SSD_GROUPS):
        Bg = Bm[:, g * N:(g + 1) * N].astype(BF16)
        Cg = Cm[:, g * N:(g + 1) * N].astype(BF16)
        CB = _dot_nt(Cg, Bg)
        gs = slice(g * GW, (g + 1) * GW)
        Sg = st_ref[:, gs]
        y_off = _dot(Cg, Sg.astype(BF16)) * dec_out_e[:, gs]
        for p in range(HPG // 2):
            col = g * GW + p * LANES
            x_pair = xdt_b[:, col:col + LANES]
            y_pair = y_off[:, p * LANES:(p + 1) * LANES]
            for q in range(2):
                hh = SSD_HEADS * d + g * HPG + 2 * p + q
                L = jnp.exp(jnp.where(tri, cs[:, hh:hh + 1] - cs_row[hh:hh + 1, :], NEG_BIG))
                Wm = (CB * L).astype(BF16)
                xm = jnp.where(lo_half if q == 0 else jnp.logical_not(lo_half), x_pair,
                               jnp.zeros_like(x_pair))
                y_pair = y_pair + _dot(Wm, xm)
            if add_skip:
                y_pair = y_pair + dskip_e[:, col:col + LANES] * xs[:, col:col + LANES]
            o_ref[:, col:col + LANES] = y_pair.astype(o_ref.dtype)
        BgT = _dot_nt(eye, Bg).astype(BF16)
        st_ref[:, gs] = dec_ch_e[:, gs] * Sg + _dot(BgT, z_st[:, gs])


def _ssd_dir(P, dtp, conv_w, conv_b, sp, *, rev, d, B, ncc, nlc, add_skip):
    M = P.shape[0]
    XW = SSD_WIDTH + 2 * SSD_GROUPS * SSD_STATE
    blk, main, prev, nxt = _seq_specs(XW, COL_XBC, B=B, ncc=ncc, nlc=nlc, rev=rev, n_rows=M)
    full2 = lambda shape: pl.BlockSpec(shape, lambda b, i: (0, 0))
    return pl.pallas_call(
        functools.partial(_ssd_kernel, rev=rev, d=d, ncc=ncc, nlc=nlc, add_skip=add_skip),
        grid=(B, ncc + nlc),
        in_specs=[main, prev, nxt,
                  pl.BlockSpec((SEQ_CHUNK, LANES), lambda b, i: (blk(b, i), 0)),
                  full2((4, XW)), full2((1, XW)), full2((8, LANES))],
        out_specs=pl.BlockSpec((SEQ_CHUNK, SSD_WIDTH), lambda b, i: (blk(b, i), 0)),
        out_shape=jax.ShapeDtypeStruct((M, SSD_WIDTH), BF16),
        scratch_shapes=[pltpu.VMEM((SSD_STATE, SSD_WIDTH), F32)],
        compiler_params=_cparams("parallel", "arbitrary"),
        name="ssd_bwd" if rev else "ssd_fwd",
    )(P, P, P, dtp, conv_w, conv_b, sp)


def _rope_kernel(q_ref, k_ref, c_ref, sa_ref, sb_ref, qo_ref, ko_ref, *, scale):
    W = q_ref.shape[1]
    reps = W // LANES
    cos = jnp.concatenate([c_ref[...]] * reps, axis=1)
    sa = jnp.concatenate([sa_ref[...]] * reps, axis=1)
    sb = jnp.concatenate([sb_ref[...]] * reps, axis=1)
    quarter = ATT_QK_DIM // 4

    def rot(x):
        return x * cos + pltpu.roll(x, W - quarter, 1) * sa + pltpu.roll(x, quarter, 1) * sb

    qo_ref[...] = (rot(q_ref[...].astype(F32)) * scale).astype(qo_ref.dtype)
    ko_ref[...] = rot(k_ref[...].astype(F32)).astype(ko_ref.dtype)


def _rope(P, cos, sa, sb, *, tm):
    M = P.shape[0]
    W = ATT_HEADS * 2 * ATT_QK_DIM
    tab = pl.BlockSpec((tm, LANES), lambda i: (i, 0))
    return pl.pallas_call(
        functools.partial(_rope_kernel, scale=ATT_QK_DIM ** -0.5),
        grid=(M // tm,),
        in_specs=[pl.BlockSpec((tm, W), lambda i: (i, COL_Q)),
                  pl.BlockSpec((tm, W), lambda i: (i, COL_K)), tab, tab, tab],
        out_specs=[pl.BlockSpec((tm, W), lambda i: (i, 0))] * 2,
        out_shape=[jax.ShapeDtypeStruct((M, W), BF16)] * 2,
        compiler_params=_cparams("parallel"),
        name="rope",
    )(P, P, cos, sa, sb)


def _attn_kernel(*refs, tq, tk, n_ctx, n_lat, lam_init):
    if n_lat:
        (q_ref, kc_ref, vc_ref, kl_ref, vl_ref, lv_ref, nw_ref, o_ref, m_ref, l_ref, acc_ref) = refs
    else:
        (q_ref, kc_ref, vc_ref, lv_ref, nw_ref, o_ref, m_ref, l_ref, acc_ref) = refs
        kl_ref = vl_ref = None

    q = q_ref[...]
    lane = lax.broadcasted_iota(jnp.int32, (1, LANES), 1)
    zero = jnp.zeros_like(q)
    qq = jnp.concatenate([jnp.where(lane < ATT_QK_DIM, q, zero),
                          jnp.where(lane >= ATT_QK_DIM, q, zero)], axis=0)

    m_ref[...] = jnp.full(m_ref.shape, NEG_BIG, F32)
    l_ref[...] = jnp.zeros_like(l_ref)
    acc_ref[...] = jnp.zeros_like(acc_ref)

    def lane_blocks(s):
        return [s[:, j * LANES:(j + 1) * LANES] for j in range(s.shape[1] // LANES)]

    def over_keys(fn):
        tc = min(n_ctx, tk)
        for j in range(n_ctx // tc):
            fn(kc_ref[j * tc:(j + 1) * tc, :], vc_ref[j * tc:(j + 1) * tc, :])
        if n_lat:
            def body(j, carry):
                rows = pl.ds(pl.multiple_of(j * tk, tk), tk)
                fn(kl_ref[rows, :], vl_ref[rows, :])
                return carry
            lax.fori_loop(0, n_lat // tk, body, 0)

    def max_pass(kc, vc):
        m_ref[...] = functools.reduce(jnp.maximum, lane_blocks(_dot_nt(qq, kc)), m_ref[...])

    over_keys(max_pass)
    m_ref[...] = jnp.broadcast_to(jnp.max(m_ref[...], axis=1, keepdims=True), m_ref.shape)

    def sum_pass(kc, vc):
        m = m_ref[...]
        p = [jnp.exp(sb - m) for sb in lane_blocks(_dot_nt(qq, kc))]
        l_ref[...] = functools.reduce(jnp.add, p, l_ref[...])
        pb = jnp.concatenate([x.astype(BF16) for x in p], axis=1)
        acc_ref[...] = acc_ref[...] + _dot(pb, vc)

    over_keys(sum_pass)
    l_ref[...] = jnp.broadcast_to(jnp.sum(l_ref[...], axis=1, keepdims=True), l_ref.shape)

    lv = lv_ref[...]
    s01 = jnp.sum(lv[0:1, :] * lv[1:2, :], axis=1, keepdims=True)
    s23 = jnp.sum(lv[2:3, :] * lv[3:4, :], axis=1, keepdims=True)
    lam = jnp.exp(s01) - jnp.exp(s23) + lam_init
    acc = acc_ref[...]
    l = l_ref[...]
    o = acc[:tq, :] / l[:tq, :] - lam * (acc[tq:, :] / l[tq:, :])
    o = o * lax.rsqrt(jnp.mean(o * o, axis=-1, keepdims=True) + LN_EPS) * nw_ref[...]
    o_ref[...] = (o * (1.0 - lam_init)).astype(o_ref.dtype)


def _attention(qr, kr, P, lam_vecs, norm_w, out_prev, *, B, S, CTX, lam_init, ctx_queries):
    M = qr.shape[0]
    H = ATT_HEADS
    tk = 512
    assert S % tk == 0 and (CTX % tk == 0 or tk % CTX == 0)
    vcol = COL_V * (1024 // LANES)
    n_lat_blk = (B * S) // CTX
    if ctx_queries:
        tq = min(CTX, 256)
        nq = CTX // tq
        q_spec = pl.BlockSpec((tq, LANES), lambda b, h, i: ((B * S + b * CTX) // tq + i, h))
    else:
        tq = 256
        nq = S // tq
        q_spec = pl.BlockSpec((tq, LANES), lambda b, h, i: (b * nq + i, h))
    in_specs = [q_spec,
                pl.BlockSpec((CTX, LANES), lambda b, h, i: (n_lat_blk + b, h)),
                pl.BlockSpec((CTX, LANES), lambda b, h, i: (n_lat_blk + b, vcol + h))]
    args = [qr, kr, P]
    if not ctx_queries:
        in_specs += [pl.BlockSpec((S, LANES), lambda b, h, i: (b, h)),
                     pl.BlockSpec((S, LANES), lambda b, h, i: (b, vcol + h))]
        args += [kr, P]
    in_specs += [pl.BlockSpec((4, ATT_QK_DIM), lambda b, h, i: (0, 0)),
                 pl.BlockSpec((1, LANES), lambda b, h, i: (0, 0))]
    args += [lam_vecs, norm_w]
    kwargs = {}
    if out_prev is not None:
        in_specs.append(pl.BlockSpec(memory_space=pl.ANY))
        args.append(out_prev)
        kwargs["input_output_aliases"] = {len(args) - 1: 0}

    kernel = functools.partial(_attn_kernel, tq=tq, tk=tk, n_ctx=CTX, n_lat=0 if ctx_queries else S,
                               lam_init=lam_init)
    if out_prev is not None:
        base = kernel
        kernel = lambda *refs: base(*refs[:len(args) - 1], *refs[len(args):])
    return pl.pallas_call(
        kernel,
        grid=(B, H, nq),
        in_specs=in_specs,
        out_specs=q_spec,
        out_shape=jax.ShapeDtypeStruct((M, H * ATT_V_DIM), BF16),
        scratch_shapes=[pltpu.VMEM((2 * tq, LANES), F32)] * 3,
        compiler_params=_cparams("parallel", "parallel", "arbitrary"),
        name="attn_ctx" if ctx_queries else "attn_lat",
        **kwargs,
    )(*args)


def _merge_kernel(hf_ref, hb_ref, gl_ref, yf_ref, yb_ref, z_ref, att_ref, g0_ref, g1_ref, g2_ref,
                  wb_ref, nw_ref, o_ref, br_ref):
    @pl.when(pl.program_id(1) == 0)
    def _():
        y = hf_ref[...].astype(F32) + hb_ref[...].astype(F32)
        g = gl_ref[...].astype(F32)
        gelu = 0.5 * g * (1.0 + jnp.tanh(0.7978845608028654 * (g + 0.044715 * (g * g * g))))
        br_ref[0] = (y * gelu).astype(BF16)
        ys = yf_ref[...].astype(F32) + yb_ref[...].astype(F32)
        zz = z_ref[...].astype(F32)
        t = ys * (zz * _sigmoid(zz))
        t = t * lax.rsqrt(jnp.mean(t * t, axis=-1, keepdims=True) + LN_EPS) * nw_ref[...]
        br_ref[1] = t.astype(BF16)
        br_ref[2] = att_ref[...]

    acc = _sigmoid(g0_ref[...].astype(F32)) * _dot(br_ref[0], wb_ref[0])
    acc = acc + _sigmoid(g1_ref[...].astype(F32)) * _dot(br_ref[1], wb_ref[1])
    acc = acc + _sigmoid(g2_ref[...].astype(F32)) * _dot(br_ref[2], wb_ref[2])
    o_ref[...] = acc.astype(o_ref.dtype)


def _merge(hf, hb, yf, yb, att, P, wb, ssd_norm_w, *, n_rows, tm, tn):
    M = P.shape[0]
    D = wb.shape[2]
    BW = wb.shape[1]
    row = lambda col: pl.BlockSpec((tm, BW), lambda i, j: (i, col))
    gcol = COL_GATES * 1024 // tn
    gate = lambda k: pl.BlockSpec((tm, tn), lambda i, j: (i, gcol + k * (D // tn) + j))
    return pl.pallas_call(
        _merge_kernel,
        grid=(n_rows // tm, D // tn),
        in_specs=[row(0), row(0), row(COL_LRU_G), row(0), row(0), row(COL_Z), row(0),
                  gate(0), gate(1), gate(2),
                  pl.BlockSpec((3, BW, tn), lambda i, j: (0, 0, j)),
                  pl.BlockSpec((1, BW), lambda i, j: (0, 0))],
        out_specs=pl.BlockSpec((tm, tn), lambda i, j: (i, j)),
        out_shape=jax.ShapeDtypeStruct((M, D), BF16),
        scratch_shapes=[pltpu.VMEM((3, tm, BW), BF16)],
        compiler_params=_cparams("parallel", "arbitrary"),
        name="merge",
    )(hf, hb, P, yf, yb, P, att, P, P, P, wb, ssd_norm_w)


def _route(logits, rbias):
    G, E = N_EXPERT_GROUPS, EXPERTS_PER_GROUP
    mx = jnp.max(logits, axis=0, keepdims=True)
    ex = jnp.exp(logits - mx)
    sc = ex / jnp.sum(ex, axis=0, keepdims=True)
    sel = sc + rbias
    srow = [sel[r:r + 1, :] for r in range(G * E)]
    prow = [sc[r:r + 1, :] for r in range(G * E)]
    gscore = []
    for g in range(G):
        v = srow[g * E:(g + 1) * E]
        best = v[0] + v[1]
        for a in range(E):
            for b in range(a + 1, E):
                if (a, b) != (0, 1):
                    best = jnp.maximum(best, v[a] + v[b])
        gscore.append(best)
    gmax = functools.reduce(jnp.maximum, gscore)
    gidx = jnp.full(gmax.shape, G - 1, jnp.int32)
    for g in range(G - 2, -1, -1):
        gidx = jnp.where(gscore[g] == gmax, g, gidx)

    def pick(rows, k):
        out = rows[(G - 1) * E + k]
        for g in range(G - 2, -1, -1):
            out = jnp.where(gidx == g, rows[g * E + k], out)
        return out

    v = [pick(srow, k) for k in range(E)]
    p = [pick(prow, k) for k in range(E)]
    chosen = []
    for k in range(E):
        rank = jnp.zeros(gmax.shape, jnp.int32)
        for j in range(E):
            if j == k:
                continue
            ahead = (v[j] > v[k]) | ((v[j] == v[k]) & (j < k)) if j < k else (v[j] > v[k])
            rank = rank + jnp.where(ahead, 1, 0)
        chosen.append(rank < 2)
    first = jnp.full(gmax.shape, E, jnp.int32)
    second = jnp.full(gmax.shape, -1, jnp.int32)
    for k in range(E):
        first = jnp.minimum(first, jnp.where(chosen[k], k, E))
        second = jnp.maximum(second, jnp.where(chosen[k], k, -1))
    p_first = jnp.zeros(gmax.shape, F32)
    p_second = jnp.zeros(gmax.shape, F32)
    for k in range(E):
        p_first = p_first + jnp.where(first == k, p[k], 0.0)
        p_second = p_second + jnp.where(second == k, p[k], 0.0)
    tot = p_first + p_second
    return gidx * E + first, gidx * E + second, p_first / tot, p_second / tot


def _outproj_kernel(m_ref, w_ref, x_ref, mod_ref, lng_ref, lnb_ref, wr_ref, rb_ref,
                    x1_ref, h2_ref, ids_ref, wts_ref, *, tm, alpha):
    o = _dot(m_ref[...], w_ref[...])
    g1 = mod_ref[0, 2:3, :]
    sh2 = mod_ref[0, 3:4, :]
    sc2 = mod_ref[0, 4:5, :]
    x1 = _ln(alpha * x_ref[...] + g1 * o) * lng_ref[...] + lnb_ref[...]
    x1_ref[...] = x1
    h2 = _ln(x1) * (1.0 + sc2) + sh2
    for k in range(SLAB):
        h2_ref[pl.ds(k, tm, stride=SLAB), :] = h2[:, k * LANES:(k + 1) * LANES]
    h_hi, h_lo = _split2(h2)
    w_hi, w_lo = _split2(wr_ref[...])
    logits = _dot_nt(w_hi, h_hi) + _dot_nt(w_lo, h_hi) + _dot_nt(w_hi, h_lo)
    e0, e1, w0, w1 = _route(logits, rb_ref[...])
    row = lax.broadcasted_iota(jnp.int32, (8, tm), 0)
    ids_ref[...] = jnp.where(row == 0, e0, jnp.where(row == 1, e1, 0))
    wts_ref[...] = jnp.where(row == 0, w0, jnp.where(row == 1, w1, 0.0))


def _outproj(m, w_out, x, mod, ln_g, ln_b, wr_t, rbias, *, n_rows, tm, n_lat, S, alpha):
    M, D = x.shape

    def midx(i):
        return jnp.where(i * tm < n_lat, 1 + (i * tm) // S, 0)

    vec = pl.BlockSpec((1, D), lambda i: (0, 0))
    return pl.pallas_call(
        functools.partial(_outproj_kernel, tm=tm, alpha=alpha),
        grid=(n_rows // tm,),
        in_specs=[pl.BlockSpec((tm, D), lambda i: (i, 0)),
                  pl.BlockSpec((D, D), lambda i: (0, 0)),
                  pl.BlockSpec((tm, D), lambda i: (i, 0)),
                  pl.BlockSpec((1, 6, D), lambda i: (midx(i), 0, 0)),
                  vec, vec,
                  pl.BlockSpec((N_EXPERTS, D), lambda i: (0, 0)),
                  pl.BlockSpec((N_EXPERTS, 1), lambda i: (0, 0))],
        out_specs=[pl.BlockSpec((tm, D), lambda i: (i, 0)),
                   pl.BlockSpec((tm * SLAB, LANES), lambda i: (i, 0)),
                   pl.BlockSpec((8, tm), lambda i: (0, i)),
                   pl.BlockSpec((8, tm), lambda i: (0, i))],
        out_shape=[jax.ShapeDtypeStruct((M, D), F32),
                   jax.ShapeDtypeStruct((n_rows * SLAB, LANES), F32),
                   jax.ShapeDtypeStruct((8, n_rows), jnp.int32),
                   jax.ShapeDtypeStruct((8, n_rows), F32)],
        compiler_params=_cparams("parallel"),
        name="outproj",
    )(m, w_out, x, mod, ln_g, ln_b, wr_t, rbias)


def _ffn_kernel(te_ref, tv_ref, src_ref, dst_ref, h_hbm, ws_ref, wg_ref, wu_ref, wd_ref, y_hbm,
                xbuf, ybuf, xb_ref, gsem, ssem, *, tm):
    i = pl.program_id(0)
    n = pl.num_programs(0)
    slot = i % 2

    def gather_start(tile, s):
        def issue(r, carry):
            tok = src_ref[tile * tm + r]
            pltpu.make_async_copy(h_hbm.at[pl.ds(pl.multiple_of(tok * SLAB, SLAB), SLAB)],
                                  xbuf.at[s, pl.ds(pl.multiple_of(r * SLAB, SLAB), SLAB)],
                                  gsem.at[s]).start()
            return carry
        lax.fori_loop(0, tm, issue, 0)

    def gather_wait(s):
        pltpu.make_async_copy(h_hbm.at[pl.ds(0, tm * SLAB)], xbuf.at[s], gsem.at[s]).wait()

    def scatter_start(tile):
        def issue(r, carry):
            tok = dst_ref[tile * tm + r]
            pltpu.make_async_copy(ybuf.at[pl.ds(pl.multiple_of(r * SLAB, SLAB), SLAB)],
                                  y_hbm.at[pl.ds(pl.multiple_of(tok * SLAB, SLAB), SLAB)],
                                  ssem).start()
            return carry
        lax.fori_loop(0, tm, issue, 0)

    def scatter_wait():
        pltpu.make_async_copy(ybuf, y_hbm.at[pl.ds(0, tm * SLAB)], ssem).wait()

    @pl.when(i == 0)
    def _():
        gather_start(0, 0)

    @pl.when(tv_ref[i] != 0)
    def _():
        has_next = jnp.logical_and(i + 1 < n, tv_ref[jnp.minimum(i + 1, n - 1)] != 0)

        @pl.when(has_next)
        def _():
            gather_start(i + 1, 1 - slot)

        gather_wait(slot)
        for k in range(SLAB):
            xb_ref[:, k * LANES:(k + 1) * LANES] = xbuf[slot, pl.ds(k, tm, stride=SLAB), :].astype(BF16)
        x = xb_ref[...]
        g = _dot(x, wg_ref[0])
        u = _dot(x, wu_ref[0])
        act = (g * _sigmoid(g) * u).astype(BF16)
        y = _dot(act, wd_ref[0]) * ws_ref[...]

        @pl.when(i > 0)
        def _():
            scatter_wait()

        for k in range(SLAB):
            ybuf[pl.ds(k, tm, stride=SLAB), :] = y[:, k * LANES:(k + 1) * LANES]
        scatter_start(i)

        @pl.when(jnp.logical_not(has_next))
        def _():
            scatter_wait()


def _expert_ffn(h2, src_tok, dst_tok, w_sorted, tile_e, tile_v, wg, wu, wd, *, tm, n_out):
    n_tiles = tile_e.shape[0]
    E, D, F = wg.shape
    return pl.pallas_call(
        functools.partial(_ffn_kernel, tm=tm),
        grid_spec=pltpu.PrefetchScalarGridSpec(
            num_scalar_prefetch=4,
            grid=(n_tiles,),
            in_specs=[pl.BlockSpec(memory_space=pl.ANY),
                      pl.BlockSpec((tm, 1), lambda i, te, tv, sr, ds: (i, 0)),
                      pl.BlockSpec((1, D, F), lambda i, te, tv, sr, ds: (te[i], 0, 0)),
                      pl.BlockSpec((1, D, F), lambda i, te, tv, sr, ds: (te[i], 0, 0)),
                      pl.BlockSpec((1, F, D), lambda i, te, tv, sr, ds: (te[i], 0, 0))],
            out_specs=pl.BlockSpec(memory_space=pl.ANY),
            scratch_shapes=[pltpu.VMEM((2, tm * SLAB, LANES), F32),
                            pltpu.VMEM((tm * SLAB, LANES), F32),
                            pltpu.VMEM((tm, D), BF16),
                            pltpu.SemaphoreType.DMA((2,)),
                            pltpu.SemaphoreType.DMA(())]),
        out_shape=jax.ShapeDtypeStruct((n_out * SLAB, LANES), F32),
        compiler_params=_cparams("arbitrary"),
        name="expert_ffn",
    )(tile_e, tile_v, src_tok, dst_tok, h2, w_sorted, wg, wu, wd)


def _final_kernel(x_ref, y0_ref, y1_ref, mod_ref, lng_ref, lnb_ref, o_ref, f_ref, *, tm, alpha):
    for k in range(SLAB):
        rows = pl.ds(k, tm, stride=SLAB)
        f_ref[:, k * LANES:(k + 1) * LANES] = y0_ref[rows, :] + y1_ref[rows, :]
    g2 = mod_ref[0, 5:6, :]
    o_ref[...] = _ln(alpha * x_ref[...] + g2 * f_ref[...]) * lng_ref[...] + lnb_ref[...]


def _final(x1, yg, mod, ln_g, ln_b, *, n_rows, tm, n_lat, S, alpha):
    D = x1.shape[1]

    def midx(i):
        return jnp.where(i * tm < n_lat, 1 + (i * tm) // S, 0)

    nt = n_rows // tm
    vec = pl.BlockSpec((1, D), lambda i: (0, 0))
    return pl.pallas_call(
        functools.partial(_final_kernel, tm=tm, alpha=alpha),
        grid=(nt,),
        in_specs=[pl.BlockSpec((tm, D), lambda i: (i, 0)),
                  pl.BlockSpec((tm * SLAB, LANES), lambda i: (i, 0)),
                  pl.BlockSpec((tm * SLAB, LANES), lambda i: (nt + i, 0)),
                  pl.BlockSpec((1, 6, D), lambda i: (midx(i), 0, 0)),
                  vec, vec],
        out_specs=pl.BlockSpec((tm, D), lambda i: (i, 0)),
        out_shape=jax.ShapeDtypeStruct((n_rows, D), F32),
        scratch_shapes=[pltpu.VMEM((tm, D), F32)],
        compiler_params=_cparams("parallel"),
        name="final",
    )(x1, yg, yg, mod, ln_g, ln_b)


def _rope_tables(B, S, CTX):
    t = jnp.arange(S)
    row = (t // GRID_W).astype(F32)
    col = (t % GRID_W).astype(F32)
    axis_dim = ATT_QK_DIM // 2
    freqs = ROPE_THETA ** (-jnp.arange(0, axis_dim, 2, dtype=F32) / axis_dim)
    ang = jnp.stack([row[:, None] * freqs, col[:, None] * freqs], axis=1)
    cos, sin = jnp.cos(ang), jnp.sin(ang)
    zero = jnp.zeros_like(sin)
    c64 = jnp.concatenate([cos, cos], axis=-1).reshape(S, ATT_QK_DIM)
    sa64 = jnp.concatenate([-sin, zero], axis=-1).reshape(S, ATT_QK_DIM)
    sb64 = jnp.concatenate([zero, sin], axis=-1).reshape(S, ATT_QK_DIM)

    def full(tab, ctx_val):
        lat = jnp.tile(jnp.concatenate([tab, tab], axis=-1), (B, 1))
        ctx = jnp.full((B * CTX, LANES), ctx_val, F32)
        return jnp.concatenate([lat, ctx], axis=0)

    return full(c64, 1.0), full(sa64, 0.0), full(sb64, 0.0)


def _moe_plan(ids, wts, tm):
    n_tok = ids.shape[1]
    e_flat = ids.reshape(-1)
    w_flat = wts.reshape(-1)
    onehot = (e_flat[:, None] == jnp.arange(N_EXPERTS, dtype=jnp.int32)[None, :]).astype(jnp.int32)
    csum = jnp.cumsum(onehot, axis=0)
    counts = csum[-1]
    rank = jnp.take_along_axis(csum, e_flat[:, None], axis=1)[:, 0] - 1
    padded = ((counts + tm - 1) // tm) * tm
    ends = jnp.cumsum(padded)
    starts = ends - padded
    pos = (starts[e_flat] + rank).astype(jnp.int32)
    n_slots = 2 * n_tok + N_EXPERTS * tm
    tok = jnp.tile(jnp.arange(n_tok, dtype=jnp.int32), 2)
    src_tok = jnp.zeros((n_slots,), jnp.int32).at[pos].set(tok)
    spare = 2 * n_tok + jnp.arange(n_slots, dtype=jnp.int32) % tm
    dst_tok = spare.at[pos].set(jnp.arange(2 * n_tok, dtype=jnp.int32))
    w_sorted = jnp.zeros((n_slots,), F32).at[pos].set(w_flat).reshape(n_slots, 1)
    tile_start = jnp.arange(n_slots // tm, dtype=jnp.int32) * tm
    tile_e = jnp.minimum(jnp.searchsorted(ends, tile_start, side="right"), N_EXPERTS - 1).astype(jnp.int32)
    tile_v = (tile_start < ends[-1]).astype(jnp.int32)
    return src_tok, dst_tok, w_sorted, tile_e, tile_v


def _row_tile(limit, *dims):
    t = limit
    while any(d % t for d in dims):
        t //= 2
    return t


def kernel(x, c, ctx, c_ctx, w_ada, b_ada, w_in, lru_conv_w, lru_conv_b, lru_gate_w, lru_gate_b, lru_lambda, ssd_conv_w, ssd_conv_b, ssd_a_log, ssd_dt_bias, ssd_d, ssd_norm_w, att_lambda, att_norm_w, w_branch, w_out, ln_g, ln_b, w_router, router_bias, w_gate_e, w_up_e, w_down_e):
    B, S, D = x.shape
    CTX = ctx.shape[1]
    depth = w_ada.shape[0]
    alpha = (2 * depth) ** 0.25
    n_lat = B * S
    n_ctx = B * CTX
    M = n_lat + n_ctx
    assert D == SLAB * LANES and CTX % SEQ_CHUNK == 0 and S % SEQ_CHUNK == 0 and S % CTX == 0
    ncc, nlc = CTX // SEQ_CHUNK, S // SEQ_CHUNK

    xs = jnp.concatenate([x.reshape(n_lat, D), ctx.reshape(n_ctx, D)], axis=0)
    cvec = jnp.concatenate([c_ctx[None, :], c, jnp.zeros((8 - 1 - B, D), F32)], axis=0)
    cos_t, sa_t, sb_t = _rope_tables(B, S, CTX)
    wr_t = w_router.T
    rbias = router_bias.reshape(N_EXPERTS, 1)

    tm_big = _row_tile(1024, S, n_ctx)
    tm_mid = _row_tile(512, S, n_ctx)
    tm_small = _row_tile(256, S, n_ctx)

    i0, i1, i2, i3, i4, i5, i6, i7, i8 = [int(v) for v in
                                          (0, 1024, 2048, 3072, 5120, 5152, 6176, 7200, 8224)]
    for l in range(depth):
        need_ctx = l < depth - 1
        lam_init = 0.8 - 0.6 * math.exp(-0.3 * l)
        n_rows = M if need_ctx else n_lat

        wl = w_in[l]
        w_main = jnp.concatenate([wl[:, i3:i4], wl[:, i0:i3], wl[:, i5:]], axis=1).astype(BF16)
        w_dt = jnp.pad(wl[:, i4:i5], ((0, 0), (0, LANES - (i5 - i4)))).astype(BF16)
        gw = lru_gate_w[l]
        wg_lru = jnp.concatenate([gw[:, 0], gw[:, 1]], axis=-1).astype(BF16)

        mod = _ada(cvec, w_ada[l], b_ada[l]).reshape(8, 6, D)
        P, dtp = _inproj(xs, mod, w_main, w_dt, tm=tm_big, tn=1024, n_lat=n_lat, S=S)

        h_dirs, y_dirs = [], []
        for d in range(2):
            rev = d == 1
            h_dirs.append(_lru_dir(P, lru_conv_w[l], lru_conv_b[l].reshape(1, -1), wg_lru[d],
                                   lru_gate_b[l, d], lru_lambda[l, d].reshape(1, -1),
                                   rev=rev, B=B, ncc=ncc, nlc=nlc))
            pad = ((0, 0), (SSD_HEADS * d, LANES - SSD_HEADS * (d + 1)))
            sp = jnp.concatenate([jnp.pad(ssd_dt_bias[l, d][None, :], pad),
                                  jnp.pad(ssd_a_log[l, d][None, :], pad),
                                  jnp.pad(ssd_d[l][None, :], pad),
                                  jnp.zeros((5, LANES), F32)], axis=0)
            y_dirs.append(_ssd_dir(P, dtp, ssd_conv_w[l], ssd_conv_b[l].reshape(1, -1), sp,
                                   rev=rev, d=d, B=B, ncc=ncc, nlc=nlc, add_skip=(d == 0)))

        qr, kr = _rope(P, cos_t, sa_t, sb_t, tm=tm_mid)
        nw_att = att_norm_w[l].reshape(1, LANES)
        att = _attention(qr, kr, P, att_lambda[l], nw_att, None, B=B, S=S, CTX=CTX,
                         lam_init=lam_init, ctx_queries=False)
        if need_ctx:
            att = _attention(qr, kr, P, att_lambda[l], nw_att, att, B=B, S=S, CTX=CTX,
                             lam_init=lam_init, ctx_queries=True)

        m = _merge(h_dirs[0], h_dirs[1], y_dirs[0], y_dirs[1], att, P, w_branch[l].astype(BF16),
                   ssd_norm_w[l].reshape(1, -1), n_rows=n_rows, tm=tm_mid, tn=512)
        x1, h2, ids, wts = _outproj(m, w_out[l].astype(BF16), xs, mod, ln_g[l, 0].reshape(1, D),
                                    ln_b[l, 0].reshape(1, D), wr_t, rbias, n_rows=n_rows,
                                    tm=tm_small, n_lat=n_lat, S=S, alpha=alpha)

        src_tok, dst_tok, w_sorted, tile_e, tile_v = _moe_plan(ids[:2], wts[:2], MOE_TILE)
        yg = _expert_ffn(h2, src_tok, dst_tok, w_sorted, tile_e, tile_v, w_gate_e[l].astype(BF16),
                         w_up_e[l].astype(BF16), w_down_e[l].astype(BF16), tm=MOE_TILE,
                         n_out=2 * n_rows + MOE_TILE)
        xs = _final(x1, yg, mod, ln_g[l, 1].reshape(1, D), ln_b[l, 1].reshape(1, D), n_rows=n_rows,
                    tm=tm_small, n_lat=n_lat, S=S, alpha=alpha)

    return xs[:n_lat].reshape(B, S, D)
```

```python
import functools
import math

import jax
import jax.numpy as jnp
from jax import lax
from jax.experimental import pallas as pl
from jax.experimental.pallas import tpu as pltpu

F32 = jnp.float32
BF16 = jnp.bfloat16

GRID_W = 64
ATT_HEADS = 8
ATT_QK_DIM = 64
ATT_V_DIM = 128
ROPE_THETA = 10000.0
SSD_HEADS = 16
SSD_HEAD_DIM = 64
SSD_GROUPS = 4
SSD_STATE = 128
SSD_WIDTH = SSD_HEADS * SSD_HEAD_DIM
LRU_WIDTH = 1024
LRU_BLOCKS = 8
LRU_BLOCK_DIM = LRU_WIDTH // LRU_BLOCKS
LRU_C = 8.0
N_EXPERTS = 16
N_EXPERT_GROUPS = 4
EXPERTS_PER_GROUP = 4
LN_EPS = 1e-6

LANES = 128
SLAB = 8
SEQ_CHUNK = 256
HALO = 16
MOE_TILE = 256
NEG_BIG = -1e30
VMEM_LIMIT = 56 * 2 ** 20

COL_XBC, COL_LRU_X, COL_LRU_G, COL_Z, COL_Q, COL_K, COL_V, COL_GATES = 0, 2, 3, 4, 5, 6, 7, 8


def _cparams(*sem):
    return pltpu.CompilerParams(dimension_semantics=sem, vmem_limit_bytes=VMEM_LIMIT)


def _dot(a, b):
    return jnp.dot(a, b, preferred_element_type=F32)


def _dot_nt(a, b):
    return lax.dot_general(a, b, (((1,), (1,)), ((), ())), preferred_element_type=F32)


def _split3(v):
    hi = v.astype(BF16)
    r = v - hi.astype(F32)
    mid = r.astype(BF16)
    lo = (r - mid.astype(F32)).astype(BF16)
    return hi, mid, lo


def _split2(v):
    hi = v.astype(BF16)
    lo = (v - hi.astype(F32)).astype(BF16)
    return hi, lo


def _store_slabs(ref, x, tm):
    half = SLAB * LANES
    for k in range(SLAB):
        lo = lax.bitcast_convert_type(x[:, k * LANES:(k + 1) * LANES].astype(BF16).astype(F32), jnp.uint32)
        hi = lax.bitcast_convert_type(
            x[:, half + k * LANES:half + (k + 1) * LANES].astype(BF16).astype(F32), jnp.uint32)
        ref[pl.ds(k, tm, stride=SLAB), :] = hi | (lo >> 16)


def _load_slab_piece(u):
    lo = lax.bitcast_convert_type(u << 16, F32)
    hi = lax.bitcast_convert_type(u & jnp.uint32(0xFFFF0000), F32)
    return lo, hi


def _ln(x):
    mu = jnp.mean(x, axis=-1, keepdims=True)
    xc = x - mu
    var = jnp.mean(xc * xc, axis=-1, keepdims=True)
    return xc * lax.rsqrt(var + LN_EPS)


def _sigmoid(x):
    return jax.nn.sigmoid(x)


def _softplus(x):
    return jnp.maximum(x, 0.0) + jnp.log1p(jnp.exp(-jnp.abs(x)))


def _ada_kernel(c_ref, w_ref, b_ref, o_ref):
    c = c_ref[...]
    a = c * _sigmoid(c)
    a_hi, a_lo = _split2(a)
    w_hi, w_lo = _split2(w_ref[...])
    o_ref[...] = _dot(a_hi, w_hi) + _dot(a_lo, w_hi) + _dot(a_hi, w_lo) + b_ref[...]


def _ada(cvec, w, b):
    R, D = cvec.shape
    N = w.shape[1]
    tn = 1024
    return pl.pallas_call(
        _ada_kernel,
        grid=(N // tn,),
        in_specs=[pl.BlockSpec((R, D), lambda j: (0, 0)),
                  pl.BlockSpec((D, tn), lambda j: (0, j)),
                  pl.BlockSpec((1, tn), lambda j: (0, j))],
        out_specs=pl.BlockSpec((R, tn), lambda j: (0, j)),
        out_shape=jax.ShapeDtypeStruct((R, N), F32),
        compiler_params=_cparams("arbitrary"),
        name="ada",
    )(cvec, w, b.reshape(1, N))


def _inproj_kernel(x_ref, mod_ref, w_ref, wdt_ref, o_ref, dt_ref, h_ref, *, tm, rb):
    @pl.when(pl.program_id(1) == 0)
    def _():
        sh = mod_ref[0, 0:1, :]
        sc = mod_ref[0, 1:2, :]

        def body(r, carry):
            rows = pl.ds(pl.multiple_of(r * rb, rb), rb)
            h = _ln(x_ref[rows, :]) * (1.0 + sc) + sh
            hb = h.astype(BF16)
            h_ref[rows, :] = hb
            dt_ref[rows, :] = _dot(hb, wdt_ref[...])
            return carry

        lax.fori_loop(0, tm // rb, body, 0)

    o_ref[...] = _dot(h_ref[...], w_ref[...]).astype(o_ref.dtype)


def _inproj(x, mod, w_main, w_dt, *, tm, tn, n_lat, S):
    M, D = x.shape
    N = w_main.shape[1]

    def midx(i):
        return jnp.where(i * tm < n_lat, 1 + (i * tm) // S, 0)

    return pl.pallas_call(
        functools.partial(_inproj_kernel, tm=tm, rb=min(tm, 256)),
        grid=(M // tm, N // tn),
        in_specs=[pl.BlockSpec((tm, D), lambda i, j: (i, 0)),
                  pl.BlockSpec((1, 6, D), lambda i, j: (midx(i), 0, 0)),
                  pl.BlockSpec((D, tn), lambda i, j: (0, j)),
                  pl.BlockSpec((D, LANES), lambda i, j: (0, 0))],
        out_specs=[pl.BlockSpec((tm, tn), lambda i, j: (i, j)),
                   pl.BlockSpec((tm, LANES), lambda i, j: (i, 0))],
        out_shape=[jax.ShapeDtypeStruct((M, N), BF16), jax.ShapeDtypeStruct((M, LANES), F32)],
        scratch_shapes=[pltpu.VMEM((tm, D), BF16)],
        compiler_params=_cparams("parallel", "arbitrary"),
        name="inproj",
    )(x, mod, w_main, w_dt)


def _chunk_of_step(i, ncc, nlc, rev):
    if not rev:
        return i
    return jnp.where(i < ncc, ncc - 1 - i, ncc + nlc - 1 - (i - ncc))


def _block_of_chunk(b, c, B, ncc, nlc):
    return jnp.where(c < ncc, B * nlc + b * ncc + c, b * nlc + (c - ncc))


def _seq_specs(width, col, *, B, ncc, nlc, rev, n_rows):
    hb = SEQ_CHUNK // HALO
    last = n_rows // HALO - 1

    def blk(b, i):
        return _block_of_chunk(b, _chunk_of_step(i, ncc, nlc, rev), B, ncc, nlc)

    main = pl.BlockSpec((SEQ_CHUNK, width), lambda b, i: (blk(b, i), col))
    prev = pl.BlockSpec((HALO, width), lambda b, i: (jnp.maximum(blk(b, i) * hb - 1, 0), col))
    nxt = pl.BlockSpec((HALO, width), lambda b, i: (jnp.minimum((blk(b, i) + 1) * hb, last), col))
    return blk, main, prev, nxt


def _dwconv_chunk(x_ref, xp_ref, xn_ref, cw_ref, cb_ref, c, ncc, nch):
    T = SEQ_CHUNK
    x = x_ref[...].astype(F32)
    has_prev = jnp.where((c == 0) | (c == ncc), 0.0, 1.0)
    has_next = jnp.where((c == ncc - 1) | (c == nch - 1), 0.0, 1.0)
    p6 = xp_ref[HALO - 2:HALO - 1, :].astype(F32) * has_prev
    p7 = xp_ref[HALO - 1:HALO, :].astype(F32) * has_prev
    n0 = xn_ref[0:1, :].astype(F32) * has_next
    row = lax.broadcasted_iota(jnp.int32, (T, 1), 0)
    xm1 = jnp.where(row == 0, p7, pltpu.roll(x, 1, 0))
    xm2 = jnp.where(row == 0, p6, jnp.where(row == 1, p7, pltpu.roll(x, 2, 0)))
    xp1 = jnp.where(row == T - 1, n0, pltpu.roll(x, T - 1, 0))
    return (cw_ref[0:1, :] * xm2 + cw_ref[1:2, :] * xm1 + cw_ref[2:3, :] * x
            + cw_ref[3:4, :] * xp1 + cb_ref[...])


def _linear_scan(a, b, rev):
    T = a.shape[0]
    row = lax.broadcasted_iota(jnp.int32, (T, 1), 0)
    s = 1
    while s < T:
        if rev:
            valid = row < T - s
            shift = T - s
        else:
            valid = row >= s
            shift = s
        a_p = jnp.where(valid, pltpu.roll(a, shift, 0), 1.0)
        b_p = jnp.where(valid, pltpu.roll(b, shift, 0), 0.0)
        b = a * b_p + b
        a = a * a_p
        s *= 2
    return a, b


def _lru_kernel(x_ref, xp_ref, xn_ref, cw_ref, cb_ref, wg_ref, gb_ref, lam_ref, o_ref, carry_ref,
                *, rev, ncc, nlc):
    T = SEQ_CHUNK
    i = pl.program_id(1)
    c = _chunk_of_step(i, ncc, nlc, rev)

    @pl.when(i == 0)
    def _():
        carry_ref[...] = jnp.zeros_like(carry_ref)

    xc = _dwconv_chunk(x_ref, xp_ref, xn_ref, cw_ref, cb_ref, c, ncc, ncc + nlc)
    sp = _softplus(-lam_ref[...])
    for n in range(LRU_BLOCKS):
        sl = slice(n * LRU_BLOCK_DIM, (n + 1) * LRU_BLOCK_DIM)
        xb = xc[:, sl]
        gl = _dot(xb.astype(BF16), wg_ref[n])
        r_gate = _sigmoid(gl[:, :LRU_BLOCK_DIM] + gb_ref[0:1, sl])
        i_gate = _sigmoid(gl[:, LRU_BLOCK_DIM:] + gb_ref[1:2, sl])
        log_a = -LRU_C * r_gate * sp[:, sl]
        a = jnp.exp(log_a)
        bb = jnp.sqrt(1.0 - a * a) * (i_gate * xb)
        a_s, b_s = _linear_scan(a, bb, rev)
        h = b_s + a_s * carry_ref[0:1, sl]
        o_ref[:, sl] = h.astype(o_ref.dtype)
        carry_ref[0:1, sl] = h[0:1, :] if rev else h[T - 1:T, :]


def _lru_dir(P, conv_w, conv_b, wg, gb, lam, *, rev, B, ncc, nlc):
    M = P.shape[0]
    W = LRU_WIDTH
    blk, main, prev, nxt = _seq_specs(W, COL_LRU_X, B=B, ncc=ncc, nlc=nlc, rev=rev, n_rows=M)
    full2 = lambda shape: pl.BlockSpec(shape, lambda b, i: (0, 0))
    return pl.pallas_call(
        functools.partial(_lru_kernel, rev=rev, ncc=ncc, nlc=nlc),
        grid=(B, ncc + nlc),
        in_specs=[main, prev, nxt, full2((4, W)), full2((1, W)),
                  pl.BlockSpec((LRU_BLOCKS, LRU_BLOCK_DIM, 2 * LRU_BLOCK_DIM), lambda b, i: (0, 0, 0)),
                  full2((2, W)), full2((1, W))],
        out_specs=pl.BlockSpec((SEQ_CHUNK, W), lambda b, i: (blk(b, i), 0)),
        out_shape=jax.ShapeDtypeStruct((M, W), BF16),
        scratch_shapes=[pltpu.VMEM((8, W), F32)],
        compiler_params=_cparams("parallel", "arbitrary"),
        name="lru_bwd" if rev else "lru_fwd",
    )(P, P, P, conv_w, conv_b, wg, gb, lam)


def _ssd_kernel(x_ref, xp_ref, xn_ref, dt_ref, cw_ref, cb_ref, sp_ref, o_ref, st_ref,
                *, rev, d, ncc, nlc, add_skip):
    T = SEQ_CHUNK
    N = SSD_STATE
    HPG = SSD_HEADS // SSD_GROUPS
    GW = HPG * SSD_HEAD_DIM
    i = pl.program_id(1)
    c = _chunk_of_step(i, ncc, nlc, rev)

    @pl.when(i == 0)
    def _():
        st_ref[...] = jnp.zeros_like(st_ref)

    xc = _dwconv_chunk(x_ref, xp_ref, xn_ref, cw_ref, cb_ref, c, ncc, ncc + nlc)
    xbc = xc * _sigmoid(xc)
    xs = xbc[:, :SSD_WIDTH]
    Bm = xbc[:, SSD_WIDTH:SSD_WIDTH + SSD_GROUPS * N]
    Cm = xbc[:, SSD_WIDTH + SSD_GROUPS * N:]

    lane = lax.broadcasted_iota(jnp.int32, (1, LANES), 1)
    in_dir = (lane >= SSD_HEADS * d) & (lane < SSD_HEADS * (d + 1))
    dt = jnp.where(in_dir, _softplus(dt_ref[...] + sp_ref[0:1, :]), 0.0)
    a_neg = jnp.where(in_dir, -jnp.exp(sp_ref[1:2, :]), 0.0)
    a = dt * a_neg

    r_i = lax.broadcasted_iota(jnp.int32, (T, T), 0)
    c_i = lax.broadcasted_iota(jnp.int32, (T, T), 1)
    tri = (c_i >= r_i) if rev else (c_i <= r_i)
    tri_b = jnp.where(tri, 1.0, 0.0).astype(BF16)
    cs = sum(_dot(tri_b, p) for p in _split3(a))
    e_r = lax.broadcasted_iota(jnp.int32, (LANES, LANES), 0)
    e_c = lax.broadcasted_iota(jnp.int32, (LANES, LANES), 1)
    eye = jnp.where(e_r == e_c, 1.0, 0.0).astype(BF16)
    cs_row = sum(_dot_nt(eye, p) for p in _split3(cs))
    cs_end = cs[0:1, :] if rev else cs[T - 1:T, :]

    x_r = lax.broadcasted_iota(jnp.int32, (LANES, SSD_WIDTH), 0)
    x_c = lax.broadcasted_iota(jnp.int32, (LANES, SSD_WIDTH), 1)
    expand_m = jnp.where(x_r == SSD_HEADS * d + x_c // SSD_HEAD_DIM, 1.0, 0.0).astype(BF16)

    def expand(v):
        return sum(_dot(p, expand_m) for p in _split3(v))

    dt_e = expand(dt)
    dec_out_e = expand(jnp.exp(cs))
    dec_st_e = expand(jnp.exp(cs_end - cs))
    row8 = lax.broadcasted_iota(jnp.int32, (8, LANES), 0)
    small = jnp.where(row8 == 0, jnp.exp(cs_end), jnp.where(row8 == 1, sp_ref[2:3, :], 0.0))
    small_e = expand(small)
    dec_ch_e = small_e[0:1, :]
    dskip_e = small_e[1:2, :]

    xdt = xs * dt_e
    xdt_b = xdt.astype(BF16)
    z_st = (xdt * dec_st_e).astype(BF16)
    lo_half = lax.broadcasted_iota(jnp.int32, (1, LANES), 1) < SSD_HEAD_DIM

    for g in range(SSD_GROUPS):
        Bg = Bm[:, g * N:(g + 1) * N].astype(BF16)
        Cg = Cm[:, g * N:(g + 1) * N].astype(BF16)
        CB = _dot_nt(Cg, Bg)
        gs = slice(g * GW, (g + 1) * GW)
        Sg = st_ref[:, gs]
        y_off = _dot(Cg, Sg.astype(BF16)) * dec_out_e[:, gs]
        for p in range(HPG // 2):
            col = g * GW + p * LANES
            x_pair = xdt_b[:, col:col + LANES]
            y_pair = y_off[:, p * LANES:(p + 1) * LANES]
            for q in range(2):
                hh = SSD_HEADS * d + g * HPG + 2 * p + q
                L = jnp.exp(jnp.where(tri, cs[:, hh:hh + 1] - cs_row[hh:hh + 1, :], NEG_BIG))
                Wm = (CB * L).astype(BF16)
                xm = jnp.where(lo_half if q == 0 else jnp.logical_not(lo_half), x_pair,
                               jnp.zeros_like(x_pair))
                y_pair = y_pair + _dot(Wm, xm)
            if add_skip:
                y_pair = y_pair + dskip_e[:, col:col + LANES] * xs[:, col:col + LANES]
            o_ref[:, col:col + LANES] = y_pair.astype(o_ref.dtype)
        BgT = _dot_nt(eye, Bg).astype(BF16)
        st_ref[:, gs] = dec_ch_e[:, gs] * Sg + _dot(BgT, z_st[:, gs])


def _ssd_dir(P, dtp, conv_w, conv_b, sp, *, rev, d, B, ncc, nlc, add_skip):
    M = P.shape[0]
    XW = SSD_WIDTH + 2 * SSD_GROUPS * SSD_STATE
    blk, main, prev, nxt = _seq_specs(XW, COL_XBC, B=B, ncc=ncc, nlc=nlc, rev=rev, n_rows=M)
    full2 = lambda shape: pl.BlockSpec(shape, lambda b, i: (0, 0))
    return pl.pallas_call(
        functools.partial(_ssd_kernel, rev=rev, d=d, ncc=ncc, nlc=nlc, add_skip=add_skip),
        grid=(B, ncc + nlc),
        in_specs=[main, prev, nxt,
                  pl.BlockSpec((SEQ_CHUNK, LANES), lambda b, i: (blk(b, i), 0)),
                  full2((4, XW)), full2((1, XW)), full2((8, LANES))],
        out_specs=pl.BlockSpec((SEQ_CHUNK, SSD_WIDTH), lambda b, i: (blk(b, i), 0)),
        out_shape=jax.ShapeDtypeStruct((M, SSD_WIDTH), BF16),
        scratch_shapes=[pltpu.VMEM((SSD_STATE, SSD_WIDTH), F32)],
        compiler_params=_cparams("parallel", "arbitrary"),
        name="ssd_bwd" if rev else "ssd_fwd",
    )(P, P, P, dtp, conv_w, conv_b, sp)


def _rope_kernel(q_ref, k_ref, c_ref, sa_ref, sb_ref, qo_ref, ko_ref, *, scale):
    W = q_ref.shape[1]
    reps = W // LANES
    cos = jnp.concatenate([c_ref[...]] * reps, axis=1)
    sa = jnp.concatenate([sa_ref[...]] * reps, axis=1)
    sb = jnp.concatenate([sb_ref[...]] * reps, axis=1)
    quarter = ATT_QK_DIM // 4

    def rot(x):
        return x * cos + pltpu.roll(x, W - quarter, 1) * sa + pltpu.roll(x, quarter, 1) * sb

    qo_ref[...] = (rot(q_ref[...].astype(F32)) * scale).astype(qo_ref.dtype)
    ko_ref[...] = rot(k_ref[...].astype(F32)).astype(ko_ref.dtype)


def _rope(P, cos, sa, sb, *, tm):
    M = P.shape[0]
    W = ATT_HEADS * 2 * ATT_QK_DIM
    tab = pl.BlockSpec((tm, LANES), lambda i: (i, 0))
    return pl.pallas_call(
        functools.partial(_rope_kernel, scale=ATT_QK_DIM ** -0.5),
        grid=(M // tm,),
        in_specs=[pl.BlockSpec((tm, W), lambda i: (i, COL_Q)),
                  pl.BlockSpec((tm, W), lambda i: (i, COL_K)), tab, tab, tab],
        out_specs=[pl.BlockSpec((tm, W), lambda i: (i, 0))] * 2,
        out_shape=[jax.ShapeDtypeStruct((M, W), BF16)] * 2,
        compiler_params=_cparams("parallel"),
        name="rope",
    )(P, P, cos, sa, sb)


def _attn_kernel(*refs, tq, tk, n_ctx, n_lat, lam_init):
    if n_lat:
        (q_ref, kc_ref, vc_ref, kl_ref, vl_ref, lv_ref, nw_ref, o_ref, m_ref, l_ref, acc_ref) = refs
    else:
        (q_ref, kc_ref, vc_ref, lv_ref, nw_ref, o_ref, m_ref, l_ref, acc_ref) = refs
        kl_ref = vl_ref = None

    q = q_ref[...]
    lane = lax.broadcasted_iota(jnp.int32, (1, LANES), 1)
    zero = jnp.zeros_like(q)
    qq = jnp.concatenate([jnp.where(lane < ATT_QK_DIM, q, zero),
                          jnp.where(lane >= ATT_QK_DIM, q, zero)], axis=0)

    m_ref[...] = jnp.full(m_ref.shape, NEG_BIG, F32)
    l_ref[...] = jnp.zeros_like(l_ref)
    acc_ref[...] = jnp.zeros_like(acc_ref)

    def lane_blocks(s):
        return [s[:, j * LANES:(j + 1) * LANES] for j in range(s.shape[1] // LANES)]

    def over_keys(fn):
        tc = min(n_ctx, tk)
        for j in range(n_ctx // tc):
            fn(kc_ref[j * tc:(j + 1) * tc, :], vc_ref[j * tc:(j + 1) * tc, :])
        if n_lat:
            def body(j, carry):
                rows = pl.ds(pl.multiple_of(j * tk, tk), tk)
                fn(kl_ref[rows, :], vl_ref[rows, :])
                return carry
            lax.fori_loop(0, n_lat // tk, body, 0, unroll=2)

    def max_pass(kc, vc):
        m_ref[...] = functools.reduce(jnp.maximum, lane_blocks(_dot_nt(qq, kc)), m_ref[...])

    over_keys(max_pass)
    m_ref[...] = jnp.broadcast_to(jnp.max(m_ref[...], axis=1, keepdims=True), m_ref.shape)

    def sum_pass(kc, vc):
        m = m_ref[...]
        p = [jnp.exp(sb - m) for sb in lane_blocks(_dot_nt(qq, kc))]
        l_ref[...] = functools.reduce(jnp.add, p, l_ref[...])
        pb = jnp.concatenate([x.astype(BF16) for x in p], axis=1)
        acc_ref[...] = acc_ref[...] + _dot(pb, vc)

    over_keys(sum_pass)
    l_ref[...] = jnp.broadcast_to(jnp.sum(l_ref[...], axis=1, keepdims=True), l_ref.shape)

    lv = lv_ref[...]
    s01 = jnp.sum(lv[0:1, :] * lv[1:2, :], axis=1, keepdims=True)
    s23 = jnp.sum(lv[2:3, :] * lv[3:4, :], axis=1, keepdims=True)
    lam = jnp.exp(s01) - jnp.exp(s23) + lam_init
    acc = acc_ref[...]
    l = l_ref[...]
    o = acc[:tq, :] / l[:tq, :] - lam * (acc[tq:, :] / l[tq:, :])
    o = o * lax.rsqrt(jnp.mean(o * o, axis=-1, keepdims=True) + LN_EPS) * nw_ref[...]
    o_ref[...] = (o * (1.0 - lam_init)).astype(o_ref.dtype)


def _attention(qr, kr, P, lam_vecs, norm_w, out_prev, *, B, S, CTX, lam_init, ctx_queries):
    M = qr.shape[0]
    H = ATT_HEADS
    tk = 512
    assert S % tk == 0 and (CTX % tk == 0 or tk % CTX == 0)
    vcol = COL_V * (1024 // LANES)
    n_lat_blk = (B * S) // CTX
    if ctx_queries:
        tq = min(CTX, 256)
        nq = CTX // tq
        q_spec = pl.BlockSpec((tq, LANES), lambda b, h, i: ((B * S + b * CTX) // tq + i, h))
    else:
        tq = 512
        nq = S // tq
        q_spec = pl.BlockSpec((tq, LANES), lambda b, h, i: (b * nq + i, h))
    in_specs = [q_spec,
                pl.BlockSpec((CTX, LANES), lambda b, h, i: (n_lat_blk + b, h)),
                pl.BlockSpec((CTX, LANES), lambda b, h, i: (n_lat_blk + b, vcol + h))]
    args = [qr, kr, P]
    if not ctx_queries:
        in_specs += [pl.BlockSpec((S, LANES), lambda b, h, i: (b, h)),
                     pl.BlockSpec((S, LANES), lambda b, h, i: (b, vcol + h))]
        args += [kr, P]
    in_specs += [pl.BlockSpec((4, ATT_QK_DIM), lambda b, h, i: (0, 0)),
                 pl.BlockSpec((1, LANES), lambda b, h, i: (0, 0))]
    args += [lam_vecs, norm_w]
    kwargs = {}
    if out_prev is not None:
        in_specs.append(pl.BlockSpec(memory_space=pl.ANY))
        args.append(out_prev)
        kwargs["input_output_aliases"] = {len(args) - 1: 0}

    kernel = functools.partial(_attn_kernel, tq=tq, tk=tk, n_ctx=CTX, n_lat=0 if ctx_queries else S,
                               lam_init=lam_init)
    if out_prev is not None:
        base = kernel
        kernel = lambda *refs: base(*refs[:len(args) - 1], *refs[len(args):])
    return pl.pallas_call(
        kernel,
        grid=(B, H, nq),
        in_specs=in_specs,
        out_specs=q_spec,
        out_shape=jax.ShapeDtypeStruct((M, H * ATT_V_DIM), BF16),
        scratch_shapes=[pltpu.VMEM((2 * tq, LANES), F32)] * 3,
        compiler_params=_cparams("parallel", "parallel", "arbitrary"),
        name="attn_ctx" if ctx_queries else "attn_lat",
        **kwargs,
    )(*args)


def _merge_kernel(hf_ref, hb_ref, gl_ref, yf_ref, yb_ref, z_ref, att_ref, g0_ref, g1_ref, g2_ref,
                  wb_ref, nw_ref, o_ref, br_ref):
    @pl.when(pl.program_id(1) == 0)
    def _():
        y = hf_ref[...].astype(F32) + hb_ref[...].astype(F32)
        g = gl_ref[...].astype(F32)
        gelu = 0.5 * g * (1.0 + jnp.tanh(0.7978845608028654 * (g + 0.044715 * (g * g * g))))
        br_ref[0] = (y * gelu).astype(BF16)
        ys = yf_ref[...].astype(F32) + yb_ref[...].astype(F32)
        zz = z_ref[...].astype(F32)
        t = ys * (zz * _sigmoid(zz))
        t = t * lax.rsqrt(jnp.mean(t * t, axis=-1, keepdims=True) + LN_EPS) * nw_ref[...]
        br_ref[1] = t.astype(BF16)
        br_ref[2] = att_ref[...]

    acc = _sigmoid(g0_ref[...].astype(F32)) * _dot(br_ref[0], wb_ref[0])
    acc = acc + _sigmoid(g1_ref[...].astype(F32)) * _dot(br_ref[1], wb_ref[1])
    acc = acc + _sigmoid(g2_ref[...].astype(F32)) * _dot(br_ref[2], wb_ref[2])
    o_ref[...] = acc.astype(o_ref.dtype)


def _merge(hf, hb, yf, yb, att, P, wb, ssd_norm_w, *, n_rows, tm, tn):
    M = P.shape[0]
    D = wb.shape[2]
    BW = wb.shape[1]
    row = lambda col: pl.BlockSpec((tm, BW), lambda i, j: (i, col))
    gcol = COL_GATES * 1024 // tn
    gate = lambda k: pl.BlockSpec((tm, tn), lambda i, j: (i, gcol + k * (D // tn) + j))
    return pl.pallas_call(
        _merge_kernel,
        grid=(n_rows // tm, D // tn),
        in_specs=[row(0), row(0), row(COL_LRU_G), row(0), row(0), row(COL_Z), row(0),
                  gate(0), gate(1), gate(2),
                  pl.BlockSpec((3, BW, tn), lambda i, j: (0, 0, j)),
                  pl.BlockSpec((1, BW), lambda i, j: (0, 0))],
        out_specs=pl.BlockSpec((tm, tn), lambda i, j: (i, j)),
        out_shape=jax.ShapeDtypeStruct((M, D), BF16),
        scratch_shapes=[pltpu.VMEM((3, tm, BW), BF16)],
        compiler_params=_cparams("parallel", "arbitrary"),
        name="merge",
    )(hf, hb, P, yf, yb, P, att, P, P, P, wb, ssd_norm_w)


def _route(logits, rbias):
    G, E = N_EXPERT_GROUPS, EXPERTS_PER_GROUP
    mx = jnp.max(logits, axis=0, keepdims=True)
    ex = jnp.exp(logits - mx)
    sc = ex / jnp.sum(ex, axis=0, keepdims=True)
    sel = sc + rbias
    srow = [sel[r:r + 1, :] for r in range(G * E)]
    prow = [sc[r:r + 1, :] for r in range(G * E)]
    gscore = []
    for g in range(G):
        v = srow[g * E:(g + 1) * E]
        best = v[0] + v[1]
        for a in range(E):
            for b in range(a + 1, E):
                if (a, b) != (0, 1):
                    best = jnp.maximum(best, v[a] + v[b])
        gscore.append(best)
    gmax = functools.reduce(jnp.maximum, gscore)
    gidx = jnp.full(gmax.shape, G - 1, jnp.int32)
    for g in range(G - 2, -1, -1):
        gidx = jnp.where(gscore[g] == gmax, g, gidx)

    def pick(rows, k):
        out = rows[(G - 1) * E + k]
        for g in range(G - 2, -1, -1):
            out = jnp.where(gidx == g, rows[g * E + k], out)
        return out

    v = [pick(srow, k) for k in range(E)]
    p = [pick(prow, k) for k in range(E)]
    chosen = []
    for k in range(E):
        rank = jnp.zeros(gmax.shape, jnp.int32)
        for j in range(E):
            if j == k:
                continue
            ahead = (v[j] > v[k]) | ((v[j] == v[k]) & (j < k)) if j < k else (v[j] > v[k])
            rank = rank + jnp.where(ahead, 1, 0)
        chosen.append(rank < 2)
    first = jnp.full(gmax.shape, E, jnp.int32)
    second = jnp.full(gmax.shape, -1, jnp.int32)
    for k in range(E):
        first = jnp.minimum(first, jnp.where(chosen[k], k, E))
        second = jnp.maximum(second, jnp.where(chosen[k], k, -1))
    p_first = jnp.zeros(gmax.shape, F32)
    p_second = jnp.zeros(gmax.shape, F32)
    for k in range(E):
        p_first = p_first + jnp.where(first == k, p[k], 0.0)
        p_second = p_second + jnp.where(second == k, p[k], 0.0)
    tot = p_first + p_second
    return gidx * E + first, gidx * E + second, p_first / tot, p_second / tot


def _outproj_kernel(m_ref, w_ref, x_ref, mod_ref, lng_ref, lnb_ref, wr_ref, rb_ref,
                    x1_ref, h2_ref, ids_ref, wts_ref, *, tm, alpha):
    o = _dot(m_ref[...], w_ref[...])
    g1 = mod_ref[0, 2:3, :]
    sh2 = mod_ref[0, 3:4, :]
    sc2 = mod_ref[0, 4:5, :]
    x1 = _ln(alpha * x_ref[...] + g1 * o) * lng_ref[...] + lnb_ref[...]
    x1_ref[...] = x1
    h2 = _ln(x1) * (1.0 + sc2) + sh2
    _store_slabs(h2_ref, h2, tm)
    h_hi, h_lo = _split2(h2)
    w_hi, w_lo = _split2(wr_ref[...])
    logits = _dot_nt(w_hi, h_hi) + _dot_nt(w_lo, h_hi) + _dot_nt(w_hi, h_lo)
    e0, e1, w0, w1 = _route(logits, rb_ref[...])
    row = lax.broadcasted_iota(jnp.int32, (8, tm), 0)
    ids_ref[...] = jnp.where(row == 0, e0, jnp.where(row == 1, e1, 0))
    wts_ref[...] = jnp.where(row == 0, w0, jnp.where(row == 1, w1, 0.0))


def _outproj(m, w_out, x, mod, ln_g, ln_b, wr_t, rbias, *, n_rows, tm, n_lat, S, alpha):
    M, D = x.shape

    def midx(i):
        return jnp.where(i * tm < n_lat, 1 + (i * tm) // S, 0)

    vec = pl.BlockSpec((1, D), lambda i: (0, 0))
    return pl.pallas_call(
        functools.partial(_outproj_kernel, tm=tm, alpha=alpha),
        grid=(n_rows // tm,),
        in_specs=[pl.BlockSpec((tm, D), lambda i: (i, 0)),
                  pl.BlockSpec((D, D), lambda i: (0, 0)),
                  pl.BlockSpec((tm, D), lambda i: (i, 0)),
                  pl.BlockSpec((1, 6, D), lambda i: (midx(i), 0, 0)),
                  vec, vec,
                  pl.BlockSpec((N_EXPERTS, D), lambda i: (0, 0)),
                  pl.BlockSpec((N_EXPERTS, 1), lambda i: (0, 0))],
        out_specs=[pl.BlockSpec((tm, D), lambda i: (i, 0)),
                   pl.BlockSpec((tm * SLAB, LANES), lambda i: (i, 0)),
                   pl.BlockSpec((8, tm), lambda i: (0, i)),
                   pl.BlockSpec((8, tm), lambda i: (0, i))],
        out_shape=[jax.ShapeDtypeStruct((M, D), F32),
                   jax.ShapeDtypeStruct((n_rows * SLAB, LANES), jnp.uint32),
                   jax.ShapeDtypeStruct((8, n_rows), jnp.int32),
                   jax.ShapeDtypeStruct((8, n_rows), F32)],
        compiler_params=_cparams("parallel"),
        name="outproj",
    )(m, w_out, x, mod, ln_g, ln_b, wr_t, rbias)


def _ffn_kernel(te_ref, tv_ref, src_ref, dst_ref, h_hbm, wg_ref, wu_ref, wd_ref, y_hbm,
                xbuf, ybuf, xb_ref, gsem, ssem, *, tm):
    i = pl.program_id(0)
    n = pl.num_programs(0)
    slot = i % 2

    def gather_start(tile, s):
        def issue(r, carry):
            tok = src_ref[tile * tm + r]
            pltpu.make_async_copy(h_hbm.at[pl.ds(pl.multiple_of(tok * SLAB, SLAB), SLAB)],
                                  xbuf.at[s, pl.ds(pl.multiple_of(r * SLAB, SLAB), SLAB)],
                                  gsem.at[s]).start()
            return carry
        lax.fori_loop(0, tm, issue, 0)

    def gather_wait(s):
        pltpu.make_async_copy(h_hbm.at[pl.ds(0, tm * SLAB)], xbuf.at[s], gsem.at[s]).wait()

    def scatter_start(tile):
        def issue(r, carry):
            tok = dst_ref[tile * tm + r]
            pltpu.make_async_copy(ybuf.at[pl.ds(pl.multiple_of(r * SLAB, SLAB), SLAB)],
                                  y_hbm.at[pl.ds(pl.multiple_of(tok * SLAB, SLAB), SLAB)],
                                  ssem).start()
            return carry
        lax.fori_loop(0, tm, issue, 0)

    def scatter_wait():
        pltpu.make_async_copy(ybuf, y_hbm.at[pl.ds(0, tm * SLAB)], ssem).wait()

    @pl.when(i == 0)
    def _():
        gather_start(0, 0)

    @pl.when(tv_ref[i] != 0)
    def _():
        has_next = jnp.logical_and(i + 1 < n, tv_ref[jnp.minimum(i + 1, n - 1)] != 0)

        @pl.when(has_next)
        def _():
            gather_start(i + 1, 1 - slot)

        gather_wait(slot)
        half = SLAB * LANES
        for k in range(SLAB):
            lo, hi = _load_slab_piece(xbuf[slot, pl.ds(k, tm, stride=SLAB), :])
            xb_ref[:, k * LANES:(k + 1) * LANES] = lo.astype(BF16)
            xb_ref[:, half + k * LANES:half + (k + 1) * LANES] = hi.astype(BF16)
        x = xb_ref[...]
        g = _dot(x, wg_ref[0])
        u = _dot(x, wu_ref[0])
        act = (g * _sigmoid(g) * u).astype(BF16)
        y = _dot(act, wd_ref[0])

        @pl.when(i > 0)
        def _():
            scatter_wait()

        _store_slabs(ybuf, y, tm)
        scatter_start(i)

        @pl.when(jnp.logical_not(has_next))
        def _():
            scatter_wait()


def _expert_ffn(h2, src_tok, dst_tok, tile_e, tile_v, wg, wu, wd, *, tm, n_out):
    n_tiles = tile_e.shape[0]
    E, D, F = wg.shape
    return pl.pallas_call(
        functools.partial(_ffn_kernel, tm=tm),
        grid_spec=pltpu.PrefetchScalarGridSpec(
            num_scalar_prefetch=4,
            grid=(n_tiles,),
            in_specs=[pl.BlockSpec(memory_space=pl.ANY),
                      pl.BlockSpec((1, D, F), lambda i, te, tv, sr, ds: (te[i], 0, 0)),
                      pl.BlockSpec((1, D, F), lambda i, te, tv, sr, ds: (te[i], 0, 0)),
                      pl.BlockSpec((1, F, D), lambda i, te, tv, sr, ds: (te[i], 0, 0))],
            out_specs=pl.BlockSpec(memory_space=pl.ANY),
            scratch_shapes=[pltpu.VMEM((2, tm * SLAB, LANES), jnp.uint32),
                            pltpu.VMEM((tm * SLAB, LANES), jnp.uint32),
                            pltpu.VMEM((tm, D), BF16),
                            pltpu.SemaphoreType.DMA((2,)),
                            pltpu.SemaphoreType.DMA(())]),
        out_shape=jax.ShapeDtypeStruct((n_out * SLAB, LANES), jnp.uint32),
        compiler_params=_cparams("arbitrary"),
        name="expert_ffn",
    )(tile_e, tile_v, src_tok, dst_tok, h2, wg, wu, wd)


def _final_kernel(x_ref, y0_ref, y1_ref, w_ref, mod_ref, lng_ref, lnb_ref, o_ref, f_ref, *, tm, alpha):
    half = SLAB * LANES
    w0 = w_ref[:, 0:1]
    w1 = w_ref[:, 1:2]
    for k in range(SLAB):
        rows = pl.ds(k, tm, stride=SLAB)
        lo0, hi0 = _load_slab_piece(y0_ref[rows, :])
        lo1, hi1 = _load_slab_piece(y1_ref[rows, :])
        f_ref[:, k * LANES:(k + 1) * LANES] = w0 * lo0 + w1 * lo1
        f_ref[:, half + k * LANES:half + (k + 1) * LANES] = w0 * hi0 + w1 * hi1
    g2 = mod_ref[0, 5:6, :]
    o_ref[...] = _ln(alpha * x_ref[...] + g2 * f_ref[...]) * lng_ref[...] + lnb_ref[...]


def _final(x1, yg, wcol, mod, ln_g, ln_b, *, n_rows, tm, n_lat, S, alpha):
    D = x1.shape[1]

    def midx(i):
        return jnp.where(i * tm < n_lat, 1 + (i * tm) // S, 0)

    nt = n_rows // tm
    vec = pl.BlockSpec((1, D), lambda i: (0, 0))
    return pl.pallas_call(
        functools.partial(_final_kernel, tm=tm, alpha=alpha),
        grid=(nt,),
        in_specs=[pl.BlockSpec((tm, D), lambda i: (i, 0)),
                  pl.BlockSpec((tm * SLAB, LANES), lambda i: (i, 0)),
                  pl.BlockSpec((tm * SLAB, LANES), lambda i: (nt + i, 0)),
                  pl.BlockSpec((tm, 2), lambda i: (i, 0)),
                  pl.BlockSpec((1, 6, D), lambda i: (midx(i), 0, 0)),
                  vec, vec],
        out_specs=pl.BlockSpec((tm, D), lambda i: (i, 0)),
        out_shape=jax.ShapeDtypeStruct((n_rows, D), F32),
        scratch_shapes=[pltpu.VMEM((tm, D), F32)],
        compiler_params=_cparams("parallel"),
        name="final",
    )(x1, yg, yg, wcol, mod, ln_g, ln_b)


def _rope_tables(B, S, CTX):
    t = jnp.arange(S)
    row = (t // GRID_W).astype(F32)
    col = (t % GRID_W).astype(F32)
    axis_dim = ATT_QK_DIM // 2
    freqs = ROPE_THETA ** (-jnp.arange(0, axis_dim, 2, dtype=F32) / axis_dim)
    ang = jnp.stack([row[:, None] * freqs, col[:, None] * freqs], axis=1)
    cos, sin = jnp.cos(ang), jnp.sin(ang)
    zero = jnp.zeros_like(sin)
    c64 = jnp.concatenate([cos, cos], axis=-1).reshape(S, ATT_QK_DIM)
    sa64 = jnp.concatenate([-sin, zero], axis=-1).reshape(S, ATT_QK_DIM)
    sb64 = jnp.concatenate([zero, sin], axis=-1).reshape(S, ATT_QK_DIM)

    def full(tab, ctx_val):
        lat = jnp.tile(jnp.concatenate([tab, tab], axis=-1), (B, 1))
        ctx = jnp.full((B * CTX, LANES), ctx_val, F32)
        return jnp.concatenate([lat, ctx], axis=0)

    return full(c64, 1.0), full(sa64, 0.0), full(sb64, 0.0)


def _moe_plan(ids, tm):
    n_tok = ids.shape[1]
    e_flat = ids.reshape(-1)
    onehot = (e_flat[:, None] == jnp.arange(N_EXPERTS, dtype=jnp.int32)[None, :]).astype(jnp.int32)
    csum = jnp.cumsum(onehot, axis=0)
    counts = csum[-1]
    rank = jnp.take_along_axis(csum, e_flat[:, None], axis=1)[:, 0] - 1
    padded = ((counts + tm - 1) // tm) * tm
    ends = jnp.cumsum(padded)
    starts = ends - padded
    pos = (starts[e_flat] + rank).astype(jnp.int32)
    n_slots = 2 * n_tok + N_EXPERTS * tm
    slot_of = jnp.full((n_slots,), -1, jnp.int32).at[pos].set(jnp.arange(2 * n_tok, dtype=jnp.int32))
    real = slot_of >= 0
    src_tok = jnp.where(real, slot_of % n_tok, 0)
    dst_tok = jnp.where(real, slot_of, 2 * n_tok + jnp.arange(n_slots, dtype=jnp.int32) % tm)
    tile_start = jnp.arange(n_slots // tm, dtype=jnp.int32) * tm
    tile_e = jnp.minimum(jnp.sum((ends[None, :] <= tile_start[:, None]).astype(jnp.int32), axis=1),
                         N_EXPERTS - 1).astype(jnp.int32)
    tile_v = (tile_start < ends[-1]).astype(jnp.int32)
    return src_tok, dst_tok, tile_e, tile_v


def _row_tile(limit, *dims):
    t = limit
    while any(d % t for d in dims):
        t //= 2
    return t


def kernel(x, c, ctx, c_ctx, w_ada, b_ada, w_in, lru_conv_w, lru_conv_b, lru_gate_w, lru_gate_b, lru_lambda, ssd_conv_w, ssd_conv_b, ssd_a_log, ssd_dt_bias, ssd_d, ssd_norm_w, att_lambda, att_norm_w, w_branch, w_out, ln_g, ln_b, w_router, router_bias, w_gate_e, w_up_e, w_down_e):
    B, S, D = x.shape
    CTX = ctx.shape[1]
    depth = w_ada.shape[0]
    alpha = (2 * depth) ** 0.25
    n_lat = B * S
    n_ctx = B * CTX
    M = n_lat + n_ctx
    assert D == 2 * SLAB * LANES and CTX % SEQ_CHUNK == 0 and S % SEQ_CHUNK == 0 and S % CTX == 0
    ncc, nlc = CTX // SEQ_CHUNK, S // SEQ_CHUNK

    xs = jnp.concatenate([x.reshape(n_lat, D), ctx.reshape(n_ctx, D)], axis=0)
    cvec = jnp.concatenate([c_ctx[None, :], c, jnp.zeros((8 - 1 - B, D), F32)], axis=0)
    cos_t, sa_t, sb_t = _rope_tables(B, S, CTX)
    wr_t = w_router.T
    rbias = router_bias.reshape(N_EXPERTS, 1)

    tm_big = _row_tile(1024, S, n_ctx)
    tm_mid = _row_tile(512, S, n_ctx)
    tm_small = _row_tile(256, S, n_ctx)

    i0, i1, i2, i3, i4, i5, i6, i7, i8 = [int(v) for v in
                                          (0, 1024, 2048, 3072, 5120, 5152, 6176, 7200, 8224)]
    for l in range(depth):
        need_ctx = l < depth - 1
        lam_init = 0.8 - 0.6 * math.exp(-0.3 * l)
        n_rows = M if need_ctx else n_lat

        wl = w_in[l]
        w_main = jnp.concatenate([wl[:, i3:i4], wl[:, i0:i3], wl[:, i5:]], axis=1).astype(BF16)
        w_dt = jnp.pad(wl[:, i4:i5], ((0, 0), (0, LANES - (i5 - i4)))).astype(BF16)
        gw = lru_gate_w[l]
        wg_lru = jnp.concatenate([gw[:, 0], gw[:, 1]], axis=-1).astype(BF16)

        mod = _ada(cvec, w_ada[l], b_ada[l]).reshape(8, 6, D)
        P, dtp = _inproj(xs, mod, w_main, w_dt, tm=tm_big, tn=1024, n_lat=n_lat, S=S)

        h_dirs, y_dirs = [], []
        for d in range(2):
            rev = d == 1
            h_dirs.append(_lru_dir(P, lru_conv_w[l], lru_conv_b[l].reshape(1, -1), wg_lru[d],
                                   lru_gate_b[l, d], lru_lambda[l, d].reshape(1, -1),
                                   rev=rev, B=B, ncc=ncc, nlc=nlc))
            pad = ((0, 0), (SSD_HEADS * d, LANES - SSD_HEADS * (d + 1)))
            sp = jnp.concatenate([jnp.pad(ssd_dt_bias[l, d][None, :], pad),
                                  jnp.pad(ssd_a_log[l, d][None, :], pad),
                                  jnp.pad(ssd_d[l][None, :], pad),
                                  jnp.zeros((5, LANES), F32)], axis=0)
            y_dirs.append(_ssd_dir(P, dtp, ssd_conv_w[l], ssd_conv_b[l].reshape(1, -1), sp,
                                   rev=rev, d=d, B=B, ncc=ncc, nlc=nlc, add_skip=(d == 0)))

        qr, kr = _rope(P, cos_t, sa_t, sb_t, tm=tm_mid)
        nw_att = att_norm_w[l].reshape(1, LANES)
        att = _attention(qr, kr, P, att_lambda[l], nw_att, None, B=B, S=S, CTX=CTX,
                         lam_init=lam_init, ctx_queries=False)
        if need_ctx:
            att = _attention(qr, kr, P, att_lambda[l], nw_att, att, B=B, S=S, CTX=CTX,
                             lam_init=lam_init, ctx_queries=True)

        m = _merge(h_dirs[0], h_dirs[1], y_dirs[0], y_dirs[1], att, P, w_branch[l].astype(BF16),
                   ssd_norm_w[l].reshape(1, -1), n_rows=n_rows, tm=tm_mid, tn=512)
        x1, h2, ids, wts = _outproj(m, w_out[l].astype(BF16), xs, mod, ln_g[l, 0].reshape(1, D),
                                    ln_b[l, 0].reshape(1, D), wr_t, rbias, n_rows=n_rows,
                                    tm=tm_small, n_lat=n_lat, S=S, alpha=alpha)

        src_tok, dst_tok, tile_e, tile_v = _moe_plan(ids[:2], MOE_TILE)
        yg = _expert_ffn(h2, src_tok, dst_tok, tile_e, tile_v, w_gate_e[l].astype(BF16),
                         w_up_e[l].astype(BF16), w_down_e[l].astype(BF16), tm=MOE_TILE,
                         n_out=2 * n_rows + MOE_TILE)
        xs = _final(x1, yg, wts[:2].T, mod, ln_g[l, 1].reshape(1, D), ln_b[l, 1].reshape(1, D),
                    n_rows=n_rows, tm=tm_small, n_lat=n_lat, S=S, alpha=alpha)

    return xs[:n_lat].reshape(B, S, D)
```

```python
import functools
import math

import jax
import jax.numpy as jnp
from jax import lax
from jax.experimental import pallas as pl
from jax.experimental.pallas import tpu as pltpu

F32 = jnp.float32
BF16 = jnp.bfloat16

GRID_W = 64
ATT_HEADS = 8
ATT_QK_DIM = 64
ATT_V_DIM = 128
ROPE_THETA = 10000.0
SSD_HEADS = 16
SSD_HEAD_DIM = 64
SSD_GROUPS = 4
SSD_STATE = 128
SSD_WIDTH = SSD_HEADS * SSD_HEAD_DIM
LRU_WIDTH = 1024
LRU_BLOCKS = 8
LRU_BLOCK_DIM = LRU_WIDTH // LRU_BLOCKS
LRU_C = 8.0
N_EXPERTS = 16
N_EXPERT_GROUPS = 4
EXPERTS_PER_GROUP = 4
LN_EPS = 1e-6

LANES = 128
SLAB = 8
SEQ_CHUNK = 256
HALO = 16
MOE_TILE = 256
NEG_BIG = -1e30
VMEM_LIMIT = 56 * 2 ** 20

COL_XBC, COL_LRU_X, COL_LRU_G, COL_Z, COL_Q, COL_K, COL_V, COL_GATES = 0, 2, 3, 4, 5, 6, 7, 8


def _cparams(*sem):
    return pltpu.CompilerParams(dimension_semantics=sem, vmem_limit_bytes=VMEM_LIMIT)


def _dot(a, b):
    return jnp.dot(a, b, preferred_element_type=F32)


def _dot_nt(a, b):
    return lax.dot_general(a, b, (((1,), (1,)), ((), ())), preferred_element_type=F32)


def _split3(v):
    hi = v.astype(BF16)
    r = v - hi.astype(F32)
    mid = r.astype(BF16)
    lo = (r - mid.astype(F32)).astype(BF16)
    return hi, mid, lo


def _split2(v):
    hi = v.astype(BF16)
    lo = (v - hi.astype(F32)).astype(BF16)
    return hi, lo


def _store_slabs(ref, x, tm):
    half = SLAB * LANES
    for k in range(SLAB):
        lo = lax.bitcast_convert_type(x[:, k * LANES:(k + 1) * LANES].astype(BF16).astype(F32), jnp.uint32)
        hi = lax.bitcast_convert_type(
            x[:, half + k * LANES:half + (k + 1) * LANES].astype(BF16).astype(F32), jnp.uint32)
        ref[pl.ds(k, tm, stride=SLAB), :] = hi | (lo >> 16)


def _load_slab_piece(u):
    lo = lax.bitcast_convert_type(u << 16, F32)
    hi = lax.bitcast_convert_type(u & jnp.uint32(0xFFFF0000), F32)
    return lo, hi


def _ln(x):
    mu = jnp.mean(x, axis=-1, keepdims=True)
    xc = x - mu
    var = jnp.mean(xc * xc, axis=-1, keepdims=True)
    return xc * lax.rsqrt(var + LN_EPS)


def _sigmoid(x):
    return jax.nn.sigmoid(x)


def _softplus(x):
    return jnp.maximum(x, 0.0) + jnp.log1p(jnp.exp(-jnp.abs(x)))


def _ada_kernel(c_ref, w_ref, b_ref, o_ref):
    c = c_ref[...]
    a = c * _sigmoid(c)
    a_hi, a_lo = _split2(a)
    w_hi, w_lo = _split2(w_ref[...])
    o_ref[...] = _dot(a_hi, w_hi) + _dot(a_lo, w_hi) + _dot(a_hi, w_lo) + b_ref[...]


def _ada(cvec, w, b):
    R, D = cvec.shape
    N = w.shape[1]
    tn = 1024
    return pl.pallas_call(
        _ada_kernel,
        grid=(N // tn,),
        in_specs=[pl.BlockSpec((R, D), lambda j: (0, 0)),
                  pl.BlockSpec((D, tn), lambda j: (0, j)),
                  pl.BlockSpec((1, tn), lambda j: (0, j))],
        out_specs=pl.BlockSpec((R, tn), lambda j: (0, j)),
        out_shape=jax.ShapeDtypeStruct((R, N), F32),
        compiler_params=_cparams("arbitrary"),
        name="ada",
    )(cvec, w, b.reshape(1, N))


def _inproj_kernel(x_ref, mod_ref, w_ref, wdt_ref, o_ref, dt_ref, h_ref, *, tm, rb):
    @pl.when(pl.program_id(1) == 0)
    def _():
        sh = mod_ref[0, 0:1, :]
        sc = mod_ref[0, 1:2, :]

        def body(r, carry):
            rows = pl.ds(pl.multiple_of(r * rb, rb), rb)
            h = _ln(x_ref[rows, :]) * (1.0 + sc) + sh
            hb = h.astype(BF16)
            h_ref[rows, :] = hb
            dt_ref[rows, :] = _dot(hb, wdt_ref[...])
            return carry

        lax.fori_loop(0, tm // rb, body, 0)

    o_ref[...] = _dot(h_ref[...], w_ref[...]).astype(o_ref.dtype)


def _inproj(x, mod, w_main, w_dt, *, tm, tn, n_lat, S):
    M, D = x.shape
    N = w_main.shape[1]

    def midx(i):
        return jnp.where(i * tm < n_lat, 1 + (i * tm) // S, 0)

    return pl.pallas_call(
        functools.partial(_inproj_kernel, tm=tm, rb=min(tm, 256)),
        grid=(M // tm, N // tn),
        in_specs=[pl.BlockSpec((tm, D), lambda i, j: (i, 0)),
                  pl.BlockSpec((1, 6, D), lambda i, j: (midx(i), 0, 0)),
                  pl.BlockSpec((D, tn), lambda i, j: (0, j)),
                  pl.BlockSpec((D, LANES), lambda i, j: (0, 0))],
        out_specs=[pl.BlockSpec((tm, tn), lambda i, j: (i, j)),
                   pl.BlockSpec((tm, LANES), lambda i, j: (i, 0))],
        out_shape=[jax.ShapeDtypeStruct((M, N), BF16), jax.ShapeDtypeStruct((M, LANES), F32)],
        scratch_shapes=[pltpu.VMEM((tm, D), BF16)],
        compiler_params=_cparams("parallel", "arbitrary"),
        name="inproj",
    )(x, mod, w_main, w_dt)


def _chunk_of_step(i, ncc, nlc, rev):
    if not rev:
        return i
    return jnp.where(i < ncc, ncc - 1 - i, ncc + nlc - 1 - (i - ncc))


def _block_of_chunk(b, c, B, ncc, nlc):
    return jnp.where(c < ncc, B * nlc + b * ncc + c, b * nlc + (c - ncc))


def _seq_specs(width, col, *, B, ncc, nlc, rev, n_rows):
    hb = SEQ_CHUNK // HALO
    last = n_rows // HALO - 1

    def blk(b, i):
        return _block_of_chunk(b, _chunk_of_step(i, ncc, nlc, rev), B, ncc, nlc)

    main = pl.BlockSpec((SEQ_CHUNK, width), lambda b, i: (blk(b, i), col))
    prev = pl.BlockSpec((HALO, width), lambda b, i: (jnp.maximum(blk(b, i) * hb - 1, 0), col))
    nxt = pl.BlockSpec((HALO, width), lambda b, i: (jnp.minimum((blk(b, i) + 1) * hb, last), col))
    return blk, main, prev, nxt


def _dwconv_chunk(x_ref, xp_ref, xn_ref, cw_ref, cb_ref, c, ncc, nch):
    T = SEQ_CHUNK
    x = x_ref[...].astype(F32)
    has_prev = jnp.where((c == 0) | (c == ncc), 0.0, 1.0)
    has_next = jnp.where((c == ncc - 1) | (c == nch - 1), 0.0, 1.0)
    p6 = xp_ref[HALO - 2:HALO - 1, :].astype(F32) * has_prev
    p7 = xp_ref[HALO - 1:HALO, :].astype(F32) * has_prev
    n0 = xn_ref[0:1, :].astype(F32) * has_next
    row = lax.broadcasted_iota(jnp.int32, (T, 1), 0)
    xm1 = jnp.where(row == 0, p7, pltpu.roll(x, 1, 0))
    xm2 = jnp.where(row == 0, p6, jnp.where(row == 1, p7, pltpu.roll(x, 2, 0)))
    xp1 = jnp.where(row == T - 1, n0, pltpu.roll(x, T - 1, 0))
    return (cw_ref[0:1, :] * xm2 + cw_ref[1:2, :] * xm1 + cw_ref[2:3, :] * x
            + cw_ref[3:4, :] * xp1 + cb_ref[...])


def _linear_scan(a, b, rev):
    T = a.shape[0]
    row = lax.broadcasted_iota(jnp.int32, (T, 1), 0)
    s = 1
    while s < T:
        if rev:
            valid = row < T - s
            shift = T - s
        else:
            valid = row >= s
            shift = s
        a_p = jnp.where(valid, pltpu.roll(a, shift, 0), 1.0)
        b_p = jnp.where(valid, pltpu.roll(b, shift, 0), 0.0)
        b = a * b_p + b
        a = a * a_p
        s *= 2
    return a, b


def _lru_kernel(x_ref, xp_ref, xn_ref, cw_ref, cb_ref, wg_ref, gb_ref, lam_ref, o_ref, carry_ref,
                *, rev, ncc, nlc):
    T = SEQ_CHUNK
    i = pl.program_id(1)
    c = _chunk_of_step(i, ncc, nlc, rev)

    @pl.when(i == 0)
    def _():
        carry_ref[...] = jnp.zeros_like(carry_ref)

    xc = _dwconv_chunk(x_ref, xp_ref, xn_ref, cw_ref, cb_ref, c, ncc, ncc + nlc)
    sp = _softplus(-lam_ref[...])
    for n in range(LRU_BLOCKS):
        sl = slice(n * LRU_BLOCK_DIM, (n + 1) * LRU_BLOCK_DIM)
        xb = xc[:, sl]
        gl = _dot(xb.astype(BF16), wg_ref[n])
        r_gate = _sigmoid(gl[:, :LRU_BLOCK_DIM] + gb_ref[0:1, sl])
        i_gate = _sigmoid(gl[:, LRU_BLOCK_DIM:] + gb_ref[1:2, sl])
        log_a = -LRU_C * r_gate * sp[:, sl]
        a = jnp.exp(log_a)
        bb = jnp.sqrt(1.0 - a * a) * (i_gate * xb)
        a_s, b_s = _linear_scan(a, bb, rev)
        h = b_s + a_s * carry_ref[0:1, sl]
        o_ref[:, sl] = h.astype(o_ref.dtype)
        carry_ref[0:1, sl] = h[0:1, :] if rev else h[T - 1:T, :]


def _lru_dir(P, conv_w, conv_b, wg, gb, lam, *, rev, B, ncc, nlc):
    M = P.shape[0]
    W = LRU_WIDTH
    blk, main, prev, nxt = _seq_specs(W, COL_LRU_X, B=B, ncc=ncc, nlc=nlc, rev=rev, n_rows=M)
    full2 = lambda shape: pl.BlockSpec(shape, lambda b, i: (0, 0))
    return pl.pallas_call(
        functools.partial(_lru_kernel, rev=rev, ncc=ncc, nlc=nlc),
        grid=(B, ncc + nlc),
        in_specs=[main, prev, nxt, full2((4, W)), full2((1, W)),
                  pl.BlockSpec((LRU_BLOCKS, LRU_BLOCK_DIM, 2 * LRU_BLOCK_DIM), lambda b, i: (0, 0, 0)),
                  full2((2, W)), full2((1, W))],
        out_specs=pl.BlockSpec((SEQ_CHUNK, W), lambda b, i: (blk(b, i), 0)),
        out_shape=jax.ShapeDtypeStruct((M, W), BF16),
        scratch_shapes=[pltpu.VMEM((8, W), F32)],
        compiler_params=_cparams("parallel", "arbitrary"),
        name="lru_bwd" if rev else "lru_fwd",
    )(P, P, P, conv_w, conv_b, wg, gb, lam)


def _ssd_kernel(x_ref, xp_ref, xn_ref, dt_ref, cw_ref, cb_ref, sp_ref, o_ref, st_ref,
                *, rev, d, ncc, nlc, add_skip):
    T = SEQ_CHUNK
    N = SSD_STATE
    HPG = SSD_HEADS // SSD_GROUPS
    GW = HPG * SSD_HEAD_DIM
    i = pl.program_id(1)
    c = _chunk_of_step(i, ncc, nlc, rev)

    @pl.when(i == 0)
    def _():
        st_ref[...] = jnp.zeros_like(st_ref)

    xc = _dwconv_chunk(x_ref, xp_ref, xn_ref, cw_ref, cb_ref, c, ncc, ncc + nlc)
    xbc = xc * _sigmoid(xc)
    xs = xbc[:, :SSD_WIDTH]
    Bm = xbc[:, SSD_WIDTH:SSD_WIDTH + SSD_GROUPS * N]
    Cm = xbc[:, SSD_WIDTH + SSD_GROUPS * N:]

    lane = lax.broadcasted_iota(jnp.int32, (1, LANES), 1)
    in_dir = (lane >= SSD_HEADS * d) & (lane < SSD_HEADS * (d + 1))
    dt = jnp.where(in_dir, _softplus(dt_ref[...] + sp_ref[0:1, :]), 0.0)
    a_neg = jnp.where(in_dir, -jnp.exp(sp_ref[1:2, :]), 0.0)
    a = dt * a_neg

    r_i = lax.broadcasted_iota(jnp.int32, (T, T), 0)
    c_i = lax.broadcasted_iota(jnp.int32, (T, T), 1)
    tri = (c_i >= r_i) if rev else (c_i <= r_i)
    tri_b = jnp.where(tri, 1.0, 0.0).astype(BF16)
    cs = sum(_dot(tri_b, p) for p in _split3(a))
    e_r = lax.broadcasted_iota(jnp.int32, (LANES, LANES), 0)
    e_c = lax.broadcasted_iota(jnp.int32, (LANES, LANES), 1)
    eye = jnp.where(e_r == e_c, 1.0, 0.0).astype(BF16)
    cs_row = sum(_dot_nt(eye, p) for p in _split3(cs))
    cs_end = cs[0:1, :] if rev else cs[T - 1:T, :]

    x_r = lax.broadcasted_iota(jnp.int32, (LANES, SSD_WIDTH), 0)
    x_c = lax.broadcasted_iota(jnp.int32, (LANES, SSD_WIDTH), 1)
    expand_m = jnp.where(x_r == SSD_HEADS * d + x_c // SSD_HEAD_DIM, 1.0, 0.0).astype(BF16)

    def expand(v):
        return sum(_dot(p, expand_m) for p in _split3(v))

    dt_e = expand(dt)
    dec_out_e = expand(jnp.exp(cs))
    dec_st_e = expand(jnp.exp(cs_end - cs))
    row8 = lax.broadcasted_iota(jnp.int32, (8, LANES), 0)
    small = jnp.where(row8 == 0, jnp.exp(cs_end), jnp.where(row8 == 1, sp_ref[2:3, :], 0.0))
    small_e = expand(small)
    dec_ch_e = small_e[0:1, :]
    dskip_e = small_e[1:2, :]

    xdt = xs * dt_e
    xdt_b = xdt.astype(BF16)
    z_st = (xdt * dec_st_e).astype(BF16)
    lo_half = lax.broadcasted_iota(jnp.int32, (1, LANES), 1) < SSD_HEAD_DIM

    for g in range(SSD_GROUPS):
        Bg = Bm[:, g * N:(g + 1) * N].astype(BF16)
        Cg = Cm[:, g * N:(g + 1) * N].astype(BF16)
        CB = _dot_nt(Cg, Bg)
        gs = slice(g * GW, (g + 1) * GW)
        Sg = st_ref[:, gs]
        y_off = _dot(Cg, Sg.astype(BF16)) * dec_out_e[:, gs]
        for p in range(HPG // 2):
            col = g * GW + p * LANES
            x_pair = xdt_b[:, col:col + LANES]
            y_pair = y_off[:, p * LANES:(p + 1) * LANES]
            for q in range(2):
                hh = SSD_HEADS * d + g * HPG + 2 * p + q
                L = jnp.exp(jnp.where(tri, cs[:, hh:hh + 1] - cs_row[hh:hh + 1, :], NEG_BIG))
                Wm = (CB * L).astype(BF16)
                xm = jnp.where(lo_half if q == 0 else jnp.logical_not(lo_half), x_pair,
                               jnp.zeros_like(x_pair))
                y_pair = y_pair + _dot(Wm, xm)
            if add_skip:
                y_pair = y_pair + dskip_e[:, col:col + LANES] * xs[:, col:col + LANES]
            o_ref[:, col:col + LANES] = y_pair.astype(o_ref.dtype)
        BgT = _dot_nt(eye, Bg).astype(BF16)
        st_ref[:, gs] = dec_ch_e[:, gs] * Sg + _dot(BgT, z_st[:, gs])


def _ssd_dir(P, dtp, conv_w, conv_b, sp, *, rev, d, B, ncc, nlc, add_skip):
    M = P.shape[0]
    XW = SSD_WIDTH + 2 * SSD_GROUPS * SSD_STATE
    blk, main, prev, nxt = _seq_specs(XW, COL_XBC, B=B, ncc=ncc, nlc=nlc, rev=rev, n_rows=M)
    full2 = lambda shape: pl.BlockSpec(shape, lambda b, i: (0, 0))
    return pl.pallas_call(
        functools.partial(_ssd_kernel, rev=rev, d=d, ncc=ncc, nlc=nlc, add_skip=add_skip),
        grid=(B, ncc + nlc),
        in_specs=[main, prev, nxt,
                  pl.BlockSpec((SEQ_CHUNK, LANES), lambda b, i: (blk(b, i), 0)),
                  full2((4, XW)), full2((1, XW)), full2((8, LANES))],
        out_specs=pl.BlockSpec((SEQ_CHUNK, SSD_WIDTH), lambda b, i: (blk(b, i), 0)),
        out_shape=jax.ShapeDtypeStruct((M, SSD_WIDTH), BF16),
        scratch_shapes=[pltpu.VMEM((SSD_STATE, SSD_WIDTH), F32)],
        compiler_params=_cparams("parallel", "arbitrary"),
        name="ssd_bwd" if rev else "ssd_fwd",
    )(P, P, P, dtp, conv_w, conv_b, sp)


def _rope_kernel(q_ref, k_ref, c_ref, sa_ref, sb_ref, qo_ref, ko_ref, *, scale):
    W = q_ref.shape[1]
    reps = W // LANES
    cos = jnp.concatenate([c_ref[...]] * reps, axis=1)
    sa = jnp.concatenate([sa_ref[...]] * reps, axis=1)
    sb = jnp.concatenate([sb_ref[...]] * reps, axis=1)
    quarter = ATT_QK_DIM // 4

    def rot(x):
        return x * cos + pltpu.roll(x, W - quarter, 1) * sa + pltpu.roll(x, quarter, 1) * sb

    qo_ref[...] = (rot(q_ref[...].astype(F32)) * scale).astype(qo_ref.dtype)
    ko_ref[...] = rot(k_ref[...].astype(F32)).astype(ko_ref.dtype)


def _rope(P, cos, sa, sb, *, tm):
    M = P.shape[0]
    W = ATT_HEADS * 2 * ATT_QK_DIM
    tab = pl.BlockSpec((tm, LANES), lambda i: (i, 0))
    return pl.pallas_call(
        functools.partial(_rope_kernel, scale=ATT_QK_DIM ** -0.5),
        grid=(M // tm,),
        in_specs=[pl.BlockSpec((tm, W), lambda i: (i, COL_Q)),
                  pl.BlockSpec((tm, W), lambda i: (i, COL_K)), tab, tab, tab],
        out_specs=[pl.BlockSpec((tm, W), lambda i: (i, 0))] * 2,
        out_shape=[jax.ShapeDtypeStruct((M, W), BF16)] * 2,
        compiler_params=_cparams("parallel"),
        name="rope",
    )(P, P, cos, sa, sb)


def _attn_kernel(*refs, tq, tk, n_ctx, n_lat, lam_init):
    if n_lat:
        (q_ref, kc_ref, vc_ref, kl_ref, vl_ref, lv_ref, nw_ref, o_ref, m_ref, l_ref, acc_ref) = refs
    else:
        (q_ref, kc_ref, vc_ref, lv_ref, nw_ref, o_ref, m_ref, l_ref, acc_ref) = refs
        kl_ref = vl_ref = None

    q = q_ref[...]
    lane = lax.broadcasted_iota(jnp.int32, (1, LANES), 1)
    zero = jnp.zeros_like(q)
    qq = jnp.concatenate([jnp.where(lane < ATT_QK_DIM, q, zero),
                          jnp.where(lane >= ATT_QK_DIM, q, zero)], axis=0)

    m_ref[...] = jnp.full(m_ref.shape, NEG_BIG, F32)
    l_ref[...] = jnp.zeros_like(l_ref)
    acc_ref[...] = jnp.zeros_like(acc_ref)

    def lane_blocks(s):
        return [s[:, j * LANES:(j + 1) * LANES] for j in range(s.shape[1] // LANES)]

    def over_keys(fn):
        tc = min(n_ctx, tk)
        for j in range(n_ctx // tc):
            fn(kc_ref[j * tc:(j + 1) * tc, :], vc_ref[j * tc:(j + 1) * tc, :])
        if n_lat:
            def body(j, carry):
                rows = pl.ds(pl.multiple_of(j * tk, tk), tk)
                fn(kl_ref[rows, :], vl_ref[rows, :])
                return carry
            lax.fori_loop(0, n_lat // tk, body, 0, unroll=4)

    def step(kc, vc):
        blocks = lane_blocks(_dot_nt(qq, kc))
        m_prev = m_ref[...]
        smax = functools.reduce(jnp.maximum, blocks)
        m_new = jnp.maximum(m_prev, jnp.max(smax, axis=1, keepdims=True))
        alpha = jnp.exp(m_prev - m_new)
        p = [jnp.exp(sb - m_new) for sb in blocks]
        l_ref[...] = alpha * l_ref[...] + functools.reduce(jnp.add, p)
        pb = jnp.concatenate([x.astype(BF16) for x in p], axis=1)
        acc_ref[...] = alpha * acc_ref[...] + _dot(pb, vc)
        m_ref[...] = m_new

    over_keys(step)
    l_ref[...] = jnp.broadcast_to(jnp.sum(l_ref[...], axis=1, keepdims=True), l_ref.shape)

    lv = lv_ref[...]
    s01 = jnp.sum(lv[0:1, :] * lv[1:2, :], axis=1, keepdims=True)
    s23 = jnp.sum(lv[2:3, :] * lv[3:4, :], axis=1, keepdims=True)
    lam = jnp.exp(s01) - jnp.exp(s23) + lam_init
    acc = acc_ref[...]
    l = l_ref[...]
    o = acc[:tq, :] / l[:tq, :] - lam * (acc[tq:, :] / l[tq:, :])
    o = o * lax.rsqrt(jnp.mean(o * o, axis=-1, keepdims=True) + LN_EPS) * nw_ref[...]
    o_ref[...] = (o * (1.0 - lam_init)).astype(o_ref.dtype)


def _attention(qr, kr, P, lam_vecs, norm_w, out_prev, *, B, S, CTX, lam_init, ctx_queries):
    M = qr.shape[0]
    H = ATT_HEADS
    tk = 512
    assert S % tk == 0 and (CTX % tk == 0 or tk % CTX == 0)
    vcol = COL_V * (1024 // LANES)
    n_lat_blk = (B * S) // CTX
    if ctx_queries:
        tq = min(CTX, 256)
        nq = CTX // tq
        q_spec = pl.BlockSpec((tq, LANES), lambda b, h, i: ((B * S + b * CTX) // tq + i, h))
    else:
        tq = 512
        nq = S // tq
        q_spec = pl.BlockSpec((tq, LANES), lambda b, h, i: (b * nq + i, h))
    in_specs = [q_spec,
                pl.BlockSpec((CTX, LANES), lambda b, h, i: (n_lat_blk + b, h)),
                pl.BlockSpec((CTX, LANES), lambda b, h, i: (n_lat_blk + b, vcol + h))]
    args = [qr, kr, P]
    if not ctx_queries:
        in_specs += [pl.BlockSpec((S, LANES), lambda b, h, i: (b, h)),
                     pl.BlockSpec((S, LANES), lambda b, h, i: (b, vcol + h))]
        args += [kr, P]
    in_specs += [pl.BlockSpec((4, ATT_QK_DIM), lambda b, h, i: (0, 0)),
                 pl.BlockSpec((1, LANES), lambda b, h, i: (0, 0))]
    args += [lam_vecs, norm_w]
    kwargs = {}
    if out_prev is not None:
        in_specs.append(pl.BlockSpec(memory_space=pl.ANY))
        args.append(out_prev)
        kwargs["input_output_aliases"] = {len(args) - 1: 0}

    kernel = functools.partial(_attn_kernel, tq=tq, tk=tk, n_ctx=CTX, n_lat=0 if ctx_queries else S,
                               lam_init=lam_init)
    if out_prev is not None:
        base = kernel
        kernel = lambda *refs: base(*refs[:len(args) - 1], *refs[len(args):])
    return pl.pallas_call(
        kernel,
        grid=(B, H, nq),
        in_specs=in_specs,
        out_specs=q_spec,
        out_shape=jax.ShapeDtypeStruct((M, H * ATT_V_DIM), BF16),
        scratch_shapes=[pltpu.VMEM((2 * tq, LANES), F32)] * 3,
        compiler_params=_cparams("parallel", "parallel", "arbitrary"),
        name="attn_ctx" if ctx_queries else "attn_lat",
        **kwargs,
    )(*args)


def _merge_kernel(hf_ref, hb_ref, gl_ref, yf_ref, yb_ref, z_ref, att_ref, g0_ref, g1_ref, g2_ref,
                  wb_ref, nw_ref, o_ref, br_ref):
    @pl.when(pl.program_id(1) == 0)
    def _():
        y = hf_ref[...].astype(F32) + hb_ref[...].astype(F32)
        g = gl_ref[...].astype(F32)
        gelu = 0.5 * g * (1.0 + jnp.tanh(0.7978845608028654 * (g + 0.044715 * (g * g * g))))
        br_ref[0] = (y * gelu).astype(BF16)
        ys = yf_ref[...].astype(F32) + yb_ref[...].astype(F32)
        zz = z_ref[...].astype(F32)
        t = ys * (zz * _sigmoid(zz))
        t = t * lax.rsqrt(jnp.mean(t * t, axis=-1, keepdims=True) + LN_EPS) * nw_ref[...]
        br_ref[1] = t.astype(BF16)
        br_ref[2] = att_ref[...]

    acc = _sigmoid(g0_ref[...].astype(F32)) * _dot(br_ref[0], wb_ref[0])
    acc = acc + _sigmoid(g1_ref[...].astype(F32)) * _dot(br_ref[1], wb_ref[1])
    acc = acc + _sigmoid(g2_ref[...].astype(F32)) * _dot(br_ref[2], wb_ref[2])
    o_ref[...] = acc.astype(o_ref.dtype)


def _merge(hf, hb, yf, yb, att, P, wb, ssd_norm_w, *, n_rows, tm, tn):
    M = P.shape[0]
    D = wb.shape[2]
    BW = wb.shape[1]
    row = lambda col: pl.BlockSpec((tm, BW), lambda i, j: (i, col))
    gcol = COL_GATES * 1024 // tn
    gate = lambda k: pl.BlockSpec((tm, tn), lambda i, j: (i, gcol + k * (D // tn) + j))
    return pl.pallas_call(
        _merge_kernel,
        grid=(n_rows // tm, D // tn),
        in_specs=[row(0), row(0), row(COL_LRU_G), row(0), row(0), row(COL_Z), row(0),
                  gate(0), gate(1), gate(2),
                  pl.BlockSpec((3, BW, tn), lambda i, j: (0, 0, j)),
                  pl.BlockSpec((1, BW), lambda i, j: (0, 0))],
        out_specs=pl.BlockSpec((tm, tn), lambda i, j: (i, j)),
        out_shape=jax.ShapeDtypeStruct((M, D), BF16),
        scratch_shapes=[pltpu.VMEM((3, tm, BW), BF16)],
        compiler_params=_cparams("parallel", "arbitrary"),
        name="merge",
    )(hf, hb, P, yf, yb, P, att, P, P, P, wb, ssd_norm_w)


def _route(logits, rbias):
    G, E = N_EXPERT_GROUPS, EXPERTS_PER_GROUP
    mx = jnp.max(logits, axis=0, keepdims=True)
    ex = jnp.exp(logits - mx)
    sc = ex / jnp.sum(ex, axis=0, keepdims=True)
    sel = sc + rbias
    srow = [sel[r:r + 1, :] for r in range(G * E)]
    prow = [sc[r:r + 1, :] for r in range(G * E)]
    gscore = []
    for g in range(G):
        v = srow[g * E:(g + 1) * E]
        best = v[0] + v[1]
        for a in range(E):
            for b in range(a + 1, E):
                if (a, b) != (0, 1):
                    best = jnp.maximum(best, v[a] + v[b])
        gscore.append(best)
    gmax = functools.reduce(jnp.maximum, gscore)
    gidx = jnp.full(gmax.shape, G - 1, jnp.int32)
    for g in range(G - 2, -1, -1):
        gidx = jnp.where(gscore[g] == gmax, g, gidx)

    def pick(rows, k):
        out = rows[(G - 1) * E + k]
        for g in range(G - 2, -1, -1):
            out = jnp.where(gidx == g, rows[g * E + k], out)
        return out

    v = [pick(srow, k) for k in range(E)]
    p = [pick(prow, k) for k in range(E)]
    chosen = []
    for k in range(E):
        rank = jnp.zeros(gmax.shape, jnp.int32)
        for j in range(E):
            if j == k:
                continue
            ahead = (v[j] > v[k]) | ((v[j] == v[k]) & (j < k)) if j < k else (v[j] > v[k])
            rank = rank + jnp.where(ahead, 1, 0)
        chosen.append(rank < 2)
    first = jnp.full(gmax.shape, E, jnp.int32)
    second = jnp.full(gmax.shape, -1, jnp.int32)
    for k in range(E):
        first = jnp.minimum(first, jnp.where(chosen[k], k, E))
        second = jnp.maximum(second, jnp.where(chosen[k], k, -1))
    p_first = jnp.zeros(gmax.shape, F32)
    p_second = jnp.zeros(gmax.shape, F32)
    for k in range(E):
        p_first = p_first + jnp.where(first == k, p[k], 0.0)
        p_second = p_second + jnp.where(second == k, p[k], 0.0)
    tot = p_first + p_second
    return gidx * E + first, gidx * E + second, p_first / tot, p_second / tot


def _outproj_kernel(m_ref, w_ref, x_ref, mod_ref, lng_ref, lnb_ref, wr_ref, rb_ref,
                    x1_ref, h2_ref, ids_ref, wts_ref, *, tm, alpha):
    o = _dot(m_ref[...], w_ref[...])
    g1 = mod_ref[0, 2:3, :]
    sh2 = mod_ref[0, 3:4, :]
    sc2 = mod_ref[0, 4:5, :]
    x1 = _ln(alpha * x_ref[...] + g1 * o) * lng_ref[...] + lnb_ref[...]
    x1_ref[...] = x1
    h2 = _ln(x1) * (1.0 + sc2) + sh2
    _store_slabs(h2_ref, h2, tm)
    h_hi, h_lo = _split2(h2)
    w_hi, w_lo = _split2(wr_ref[...])
    logits = _dot_nt(w_hi, h_hi) + _dot_nt(w_lo, h_hi) + _dot_nt(w_hi, h_lo)
    e0, e1, w0, w1 = _route(logits, rb_ref[...])
    row = lax.broadcasted_iota(jnp.int32, (8, tm), 0)
    ids_ref[...] = jnp.where(row == 0, e0, jnp.where(row == 1, e1, 0))
    wts_ref[...] = jnp.where(row == 0, w0, jnp.where(row == 1, w1, 0.0))


def _outproj(m, w_out, x, mod, ln_g, ln_b, wr_t, rbias, *, n_rows, tm, n_lat, S, alpha):
    M, D = x.shape

    def midx(i):
        return jnp.where(i * tm < n_lat, 1 + (i * tm) // S, 0)

    vec = pl.BlockSpec((1, D), lambda i: (0, 0))
    return pl.pallas_call(
        functools.partial(_outproj_kernel, tm=tm, alpha=alpha),
        grid=(n_rows // tm,),
        in_specs=[pl.BlockSpec((tm, D), lambda i: (i, 0)),
                  pl.BlockSpec((D, D), lambda i: (0, 0)),
                  pl.BlockSpec((tm, D), lambda i: (i, 0)),
                  pl.BlockSpec((1, 6, D), lambda i: (midx(i), 0, 0)),
                  vec, vec,
                  pl.BlockSpec((N_EXPERTS, D), lambda i: (0, 0)),
                  pl.BlockSpec((N_EXPERTS, 1), lambda i: (0, 0))],
        out_specs=[pl.BlockSpec((tm, D), lambda i: (i, 0)),
                   pl.BlockSpec((tm * SLAB, LANES), lambda i: (i, 0)),
                   pl.BlockSpec((8, tm), lambda i: (0, i)),
                   pl.BlockSpec((8, tm), lambda i: (0, i))],
        out_shape=[jax.ShapeDtypeStruct((M, D), F32),
                   jax.ShapeDtypeStruct((n_rows * SLAB, LANES), jnp.uint32),
                   jax.ShapeDtypeStruct((8, n_rows), jnp.int32),
                   jax.ShapeDtypeStruct((8, n_rows), F32)],
        compiler_params=_cparams("parallel"),
        name="outproj",
    )(m, w_out, x, mod, ln_g, ln_b, wr_t, rbias)


def _ffn_kernel(te_ref, tv_ref, src_ref, dst_ref, h_hbm, wg_ref, wu_ref, wd_ref, y_hbm,
                xbuf, ybuf, xb_ref, gsem, ssem, *, tm):
    i = pl.program_id(0)
    n = pl.num_programs(0)
    slot = i % 2

    def gather_start(tile, s):
        def issue(r2, carry):
            for prio in range(2):
                r = 2 * r2 + prio
                tok = src_ref[tile * tm + r]
                pltpu.make_async_copy(h_hbm.at[pl.ds(pl.multiple_of(tok * SLAB, SLAB), SLAB)],
                                      xbuf.at[s, pl.ds(pl.multiple_of(r * SLAB, SLAB), SLAB)],
                                      gsem.at[s]).start(priority=prio)
            return carry
        lax.fori_loop(0, tm // 2, issue, 0)

    def gather_wait(s):
        pltpu.make_async_copy(h_hbm.at[pl.ds(0, tm * SLAB)], xbuf.at[s], gsem.at[s]).wait()

    def scatter_start(tile):
        def issue(r2, carry):
            for prio in range(2):
                r = 2 * r2 + prio
                tok = dst_ref[tile * tm + r]
                pltpu.make_async_copy(ybuf.at[pl.ds(pl.multiple_of(r * SLAB, SLAB), SLAB)],
                                      y_hbm.at[pl.ds(pl.multiple_of(tok * SLAB, SLAB), SLAB)],
                                      ssem).start(priority=prio)
            return carry
        lax.fori_loop(0, tm // 2, issue, 0)

    def scatter_wait():
        pltpu.make_async_copy(ybuf, y_hbm.at[pl.ds(0, tm * SLAB)], ssem).wait()

    @pl.when(i == 0)
    def _():
        gather_start(0, 0)

    @pl.when(tv_ref[i] != 0)
    def _():
        has_next = jnp.logical_and(i + 1 < n, tv_ref[jnp.minimum(i + 1, n - 1)] != 0)

        @pl.when(has_next)
        def _():
            gather_start(i + 1, 1 - slot)

        gather_wait(slot)
        half = SLAB * LANES
        for k in range(SLAB):
            lo, hi = _load_slab_piece(xbuf[slot, pl.ds(k, tm, stride=SLAB), :])
            xb_ref[:, k * LANES:(k + 1) * LANES] = lo.astype(BF16)
            xb_ref[:, half + k * LANES:half + (k + 1) * LANES] = hi.astype(BF16)
        x = xb_ref[...]
        g = _dot(x, wg_ref[0])
        u = _dot(x, wu_ref[0])
        act = (g * _sigmoid(g) * u).astype(BF16)
        y = _dot(act, wd_ref[0])

        @pl.when(i > 0)
        def _():
            scatter_wait()

        _store_slabs(ybuf, y, tm)
        scatter_start(i)

        @pl.when(jnp.logical_not(has_next))
        def _():
            scatter_wait()


def _expert_ffn(h2, src_tok, dst_tok, tile_e, tile_v, wg, wu, wd, *, tm, n_out):
    n_tiles = tile_e.shape[0]
    E, D, F = wg.shape
    return pl.pallas_call(
        functools.partial(_ffn_kernel, tm=tm),
        grid_spec=pltpu.PrefetchScalarGridSpec(
            num_scalar_prefetch=4,
            grid=(n_tiles,),
            in_specs=[pl.BlockSpec(memory_space=pl.ANY),
                      pl.BlockSpec((1, D, F), lambda i, te, tv, sr, ds: (te[i], 0, 0)),
                      pl.BlockSpec((1, D, F), lambda i, te, tv, sr, ds: (te[i], 0, 0)),
                      pl.BlockSpec((1, F, D), lambda i, te, tv, sr, ds: (te[i], 0, 0))],
            out_specs=pl.BlockSpec(memory_space=pl.ANY),
            scratch_shapes=[pltpu.VMEM((2, tm * SLAB, LANES), jnp.uint32),
                            pltpu.VMEM((tm * SLAB, LANES), jnp.uint32),
                            pltpu.VMEM((tm, D), BF16),
                            pltpu.SemaphoreType.DMA((2,)),
                            pltpu.SemaphoreType.DMA(())]),
        out_shape=jax.ShapeDtypeStruct((n_out * SLAB, LANES), jnp.uint32),
        compiler_params=_cparams("arbitrary"),
        name="expert_ffn",
    )(tile_e, tile_v, src_tok, dst_tok, h2, wg, wu, wd)


def _final_kernel(x_ref, y0_ref, y1_ref, w_ref, mod_ref, lng_ref, lnb_ref, o_ref, f_ref, *, tm, alpha):
    half = SLAB * LANES
    w0 = w_ref[:, 0:1]
    w1 = w_ref[:, 1:2]
    for k in range(SLAB):
        rows = pl.ds(k, tm, stride=SLAB)
        lo0, hi0 = _load_slab_piece(y0_ref[rows, :])
        lo1, hi1 = _load_slab_piece(y1_ref[rows, :])
        f_ref[:, k * LANES:(k + 1) * LANES] = w0 * lo0 + w1 * lo1
        f_ref[:, half + k * LANES:half + (k + 1) * LANES] = w0 * hi0 + w1 * hi1
    g2 = mod_ref[0, 5:6, :]
    o_ref[...] = _ln(alpha * x_ref[...] + g2 * f_ref[...]) * lng_ref[...] + lnb_ref[...]


def _final(x1, yg, wcol, mod, ln_g, ln_b, *, n_rows, tm, n_lat, S, alpha):
    D = x1.shape[1]

    def midx(i):
        return jnp.where(i * tm < n_lat, 1 + (i * tm) // S, 0)

    nt = n_rows // tm
    vec = pl.BlockSpec((1, D), lambda i: (0, 0))
    return pl.pallas_call(
        functools.partial(_final_kernel, tm=tm, alpha=alpha),
        grid=(nt,),
        in_specs=[pl.BlockSpec((tm, D), lambda i: (i, 0)),
                  pl.BlockSpec((tm * SLAB, LANES), lambda i: (i, 0)),
                  pl.BlockSpec((tm * SLAB, LANES), lambda i: (nt + i, 0)),
                  pl.BlockSpec((tm, 2), lambda i: (i, 0)),
                  pl.BlockSpec((1, 6, D), lambda i: (midx(i), 0, 0)),
                  vec, vec],
        out_specs=pl.BlockSpec((tm, D), lambda i: (i, 0)),
        out_shape=jax.ShapeDtypeStruct((n_rows, D), F32),
        scratch_shapes=[pltpu.VMEM((tm, D), F32)],
        compiler_params=_cparams("parallel"),
        name="final",
    )(x1, yg, yg, wcol, mod, ln_g, ln_b)


def _rope_tables(B, S, CTX):
    t = jnp.arange(S)
    row = (t // GRID_W).astype(F32)
    col = (t % GRID_W).astype(F32)
    axis_dim = ATT_QK_DIM // 2
    freqs = ROPE_THETA ** (-jnp.arange(0, axis_dim, 2, dtype=F32) / axis_dim)
    ang = jnp.stack([row[:, None] * freqs, col[:, None] * freqs], axis=1)
    cos, sin = jnp.cos(ang), jnp.sin(ang)
    zero = jnp.zeros_like(sin)
    c64 = jnp.concatenate([cos, cos], axis=-1).reshape(S, ATT_QK_DIM)
    sa64 = jnp.concatenate([-sin, zero], axis=-1).reshape(S, ATT_QK_DIM)
    sb64 = jnp.concatenate([zero, sin], axis=-1).reshape(S, ATT_QK_DIM)

    def full(tab, ctx_val):
        lat = jnp.tile(jnp.concatenate([tab, tab], axis=-1), (B, 1))
        ctx = jnp.full((B * CTX, LANES), ctx_val, F32)
        return jnp.concatenate([lat, ctx], axis=0)

    return full(c64, 1.0), full(sa64, 0.0), full(sb64, 0.0)


def _moe_plan(ids, tm):
    n_tok = ids.shape[1]
    e_flat = ids.reshape(-1)
    onehot = (e_flat[:, None] == jnp.arange(N_EXPERTS, dtype=jnp.int32)[None, :]).astype(jnp.int32)
    csum = jnp.cumsum(onehot, axis=0)
    counts = csum[-1]
    rank = jnp.take_along_axis(csum, e_flat[:, None], axis=1)[:, 0] - 1
    padded = ((counts + tm - 1) // tm) * tm
    ends = jnp.cumsum(padded)
    starts = ends - padded
    pos = (starts[e_flat] + rank).astype(jnp.int32)
    n_slots = 2 * n_tok + N_EXPERTS * tm
    slot_of = jnp.full((n_slots,), -1, jnp.int32).at[pos].set(jnp.arange(2 * n_tok, dtype=jnp.int32))
    real = slot_of >= 0
    src_tok = jnp.where(real, slot_of % n_tok, 0)
    dst_tok = jnp.where(real, slot_of, 2 * n_tok + jnp.arange(n_slots, dtype=jnp.int32) % tm)
    tile_start = jnp.arange(n_slots // tm, dtype=jnp.int32) * tm
    tile_e = jnp.minimum(jnp.sum((ends[None, :] <= tile_start[:, None]).astype(jnp.int32), axis=1),
                         N_EXPERTS - 1).astype(jnp.int32)
    tile_v = (tile_start < ends[-1]).astype(jnp.int32)
    return src_tok, dst_tok, tile_e, tile_v


def _row_tile(limit, *dims):
    t = limit
    while any(d % t for d in dims):
        t //= 2
    return t


def kernel(x, c, ctx, c_ctx, w_ada, b_ada, w_in, lru_conv_w, lru_conv_b, lru_gate_w, lru_gate_b, lru_lambda, ssd_conv_w, ssd_conv_b, ssd_a_log, ssd_dt_bias, ssd_d, ssd_norm_w, att_lambda, att_norm_w, w_branch, w_out, ln_g, ln_b, w_router, router_bias, w_gate_e, w_up_e, w_down_e):
    B, S, D = x.shape
    CTX = ctx.shape[1]
    depth = w_ada.shape[0]
    alpha = (2 * depth) ** 0.25
    n_lat = B * S
    n_ctx = B * CTX
    M = n_lat + n_ctx
    assert D == 2 * SLAB * LANES and CTX % SEQ_CHUNK == 0 and S % SEQ_CHUNK == 0 and S % CTX == 0
    ncc, nlc = CTX // SEQ_CHUNK, S // SEQ_CHUNK

    xs = jnp.concatenate([x.reshape(n_lat, D), ctx.reshape(n_ctx, D)], axis=0)
    cvec = jnp.concatenate([c_ctx[None, :], c, jnp.zeros((8 - 1 - B, D), F32)], axis=0)
    cos_t, sa_t, sb_t = _rope_tables(B, S, CTX)
    wr_t = w_router.T
    rbias = router_bias.reshape(N_EXPERTS, 1)

    tm_big = _row_tile(1024, S, n_ctx)
    tm_mid = _row_tile(512, S, n_ctx)
    tm_small = _row_tile(256, S, n_ctx)

    i0, i1, i2, i3, i4, i5, i6, i7, i8 = [int(v) for v in
                                          (0, 1024, 2048, 3072, 5120, 5152, 6176, 7200, 8224)]
    for l in range(depth):
        need_ctx = l < depth - 1
        lam_init = 0.8 - 0.6 * math.exp(-0.3 * l)
        n_rows = M if need_ctx else n_lat

        wl = w_in[l]
        w_main = jnp.concatenate([wl[:, i3:i4], wl[:, i0:i3], wl[:, i5:]], axis=1).astype(BF16)
        w_dt = jnp.pad(wl[:, i4:i5], ((0, 0), (0, LANES - (i5 - i4)))).astype(BF16)
        gw = lru_gate_w[l]
        wg_lru = jnp.concatenate([gw[:, 0], gw[:, 1]], axis=-1).astype(BF16)

        mod = _ada(cvec, w_ada[l], b_ada[l]).reshape(8, 6, D)
        P, dtp = _inproj(xs, mod, w_main, w_dt, tm=tm_big, tn=1024, n_lat=n_lat, S=S)

        h_dirs, y_dirs = [], []
        for d in range(2):
            rev = d == 1
            h_dirs.append(_lru_dir(P, lru_conv_w[l], lru_conv_b[l].reshape(1, -1), wg_lru[d],
                                   lru_gate_b[l, d], lru_lambda[l, d].reshape(1, -1),
                                   rev=rev, B=B, ncc=ncc, nlc=nlc))
            pad = ((0, 0), (SSD_HEADS * d, LANES - SSD_HEADS * (d + 1)))
            sp = jnp.concatenate([jnp.pad(ssd_dt_bias[l, d][None, :], pad),
                                  jnp.pad(ssd_a_log[l, d][None, :], pad),
                                  jnp.pad(ssd_d[l][None, :], pad),
                                  jnp.zeros((5, LANES), F32)], axis=0)
            y_dirs.append(_ssd_dir(P, dtp, ssd_conv_w[l], ssd_conv_b[l].reshape(1, -1), sp,
                                   rev=rev, d=d, B=B, ncc=ncc, nlc=nlc, add_skip=(d == 0)))

        qr, kr = _rope(P, cos_t, sa_t, sb_t, tm=tm_mid)
        nw_att = att_norm_w[l].reshape(1, LANES)
        att = _attention(qr, kr, P, att_lambda[l], nw_att, None, B=B, S=S, CTX=CTX,
                         lam_init=lam_init, ctx_queries=False)
        if need_ctx:
            att = _attention(qr, kr, P, att_lambda[l], nw_att, att, B=B, S=S, CTX=CTX,
                             lam_init=lam_init, ctx_queries=True)

        m = _merge(h_dirs[0], h_dirs[1], y_dirs[0], y_dirs[1], att, P, w_branch[l].astype(BF16),
                   ssd_norm_w[l].reshape(1, -1), n_rows=n_rows, tm=tm_mid, tn=512)
        x1, h2, ids, wts = _outproj(m, w_out[l].astype(BF16), xs, mod, ln_g[l, 0].reshape(1, D),
                                    ln_b[l, 0].reshape(1, D), wr_t, rbias, n_rows=n_rows,
                                    tm=tm_small, n_lat=n_lat, S=S, alpha=alpha)

        src_tok, dst_tok, tile_e, tile_v = _moe_plan(ids[:2], MOE_TILE)
        yg = _expert_ffn(h2, src_tok, dst_tok, tile_e, tile_v, w_gate_e[l].astype(BF16),
                         w_up_e[l].astype(BF16), w_down_e[l].astype(BF16), tm=MOE_TILE,
                         n_out=2 * n_rows + MOE_TILE)
        xs = _final(x1, yg, wts[:2].T, mod, ln_g[l, 1].reshape(1, D), ln_b[l, 1].reshape(1, D),
                    n_rows=n_rows, tm=tm_small, n_lat=n_lat, S=S, alpha=alpha)

    return xs[:n_lat].reshape(B, S, D)
```

```python
import functools
import math

import jax
import jax.numpy as jnp
from jax import lax
from jax.experimental import pallas as pl
from jax.experimental.pallas import tpu as pltpu

F32 = jnp.float32
BF16 = jnp.bfloat16

GRID_W = 64
ATT_HEADS = 8
ATT_QK_DIM = 64
ATT_V_DIM = 128
ROPE_THETA = 10000.0
SSD_HEADS = 16
SSD_HEAD_DIM = 64
SSD_GROUPS = 4
SSD_STATE = 128
SSD_WIDTH = SSD_HEADS * SSD_HEAD_DIM
LRU_WIDTH = 1024
LRU_BLOCKS = 8
LRU_BLOCK_DIM = LRU_WIDTH // LRU_BLOCKS
LRU_C = 8.0
N_EXPERTS = 16
N_EXPERT_GROUPS = 4
EXPERTS_PER_GROUP = 4
LN_EPS = 1e-6

LANES = 128
SLAB = 8
SEQ_CHUNK = 256
HALO = 16
MOE_TILE = 256
NEG_BIG = -1e30
VMEM_LIMIT = 56 * 2 ** 20

COL_XBC, COL_LRU_X, COL_LRU_G, COL_Z, COL_Q, COL_K, COL_V, COL_GATES = 0, 2, 3, 4, 5, 6, 7, 8


def _cparams(*sem):
    return pltpu.CompilerParams(dimension_semantics=sem, vmem_limit_bytes=VMEM_LIMIT)


def _dot(a, b):
    return jnp.dot(a, b, preferred_element_type=F32)


def _dot_nt(a, b):
    return lax.dot_general(a, b, (((1,), (1,)), ((), ())), preferred_element_type=F32)


def _split3(v):
    hi = v.astype(BF16)
    r = v - hi.astype(F32)
    mid = r.astype(BF16)
    lo = (r - mid.astype(F32)).astype(BF16)
    return hi, mid, lo


def _split2(v):
    hi = v.astype(BF16)
    lo = (v - hi.astype(F32)).astype(BF16)
    return hi, lo


def _store_slabs(ref, x, tm):
    half = SLAB * LANES
    for k in range(SLAB):
        lo = lax.bitcast_convert_type(x[:, k * LANES:(k + 1) * LANES].astype(BF16).astype(F32), jnp.uint32)
        hi = lax.bitcast_convert_type(
            x[:, half + k * LANES:half + (k + 1) * LANES].astype(BF16).astype(F32), jnp.uint32)
        ref[pl.ds(k, tm, stride=SLAB), :] = hi | (lo >> 16)


def _load_slab_piece(u):
    lo = lax.bitcast_convert_type(u << 16, F32)
    hi = lax.bitcast_convert_type(u & jnp.uint32(0xFFFF0000), F32)
    return lo, hi


def _ln(x):
    mu = jnp.mean(x, axis=-1, keepdims=True)
    xc = x - mu
    var = jnp.mean(xc * xc, axis=-1, keepdims=True)
    return xc * lax.rsqrt(var + LN_EPS)


def _sigmoid(x):
    return jax.nn.sigmoid(x)


def _softplus(x):
    return jnp.maximum(x, 0.0) + jnp.log1p(jnp.exp(-jnp.abs(x)))


def _cast_kernel(w_ref, o_ref):
    o_ref[...] = w_ref[0].astype(o_ref.dtype)


def _cast_bf16(w, l):
    _, E, R, C = w.shape
    return pl.pallas_call(
        _cast_kernel,
        grid=(E,),
        in_specs=[pl.BlockSpec((1, 1, R, C), lambda e: (l, e, 0, 0))],
        out_specs=pl.BlockSpec((1, R, C), lambda e: (e, 0, 0)),
        out_shape=jax.ShapeDtypeStruct((E, R, C), BF16),
        compiler_params=_cparams("parallel"),
        name="cast_bf16",
    )(w)


def _ada_kernel(c_ref, w_ref, b_ref, o_ref):
    c = c_ref[...]
    a = c * _sigmoid(c)
    a_hi, a_lo = _split2(a)
    w_hi, w_lo = _split2(w_ref[0])
    o_ref[...] = _dot(a_hi, w_hi) + _dot(a_lo, w_hi) + _dot(a_hi, w_lo) + b_ref[...]


def _ada(cvec, w, b, l):
    R, D = cvec.shape
    N = w.shape[2]
    tn = 1024
    return pl.pallas_call(
        _ada_kernel,
        grid=(N // tn,),
        in_specs=[pl.BlockSpec((R, D), lambda j: (0, 0)),
                  pl.BlockSpec((1, D, tn), lambda j: (l, 0, j)),
                  pl.BlockSpec((1, tn), lambda j: (0, j))],
        out_specs=pl.BlockSpec((R, tn), lambda j: (0, j)),
        out_shape=jax.ShapeDtypeStruct((R, N), F32),
        compiler_params=_cparams("arbitrary"),
        name="ada",
    )(cvec, w, b.reshape(1, N))


def _inproj_kernel(x_ref, mod_ref, w_ref, wdt_ref, o_ref, dt_ref, h_ref, *, tm, rb):
    @pl.when(pl.program_id(1) == 0)
    def _():
        sh = mod_ref[0, 0:1, :]
        sc = mod_ref[0, 1:2, :]

        def body(r, carry):
            rows = pl.ds(pl.multiple_of(r * rb, rb), rb)
            h = _ln(x_ref[rows, :]) * (1.0 + sc) + sh
            hb = h.astype(BF16)
            h_ref[rows, :] = hb
            dt_ref[rows, :] = _dot(hb, wdt_ref[...])
            return carry

        lax.fori_loop(0, tm // rb, body, 0)

    o_ref[...] = _dot(h_ref[...], w_ref[...]).astype(o_ref.dtype)


def _inproj(x, mod, w_main, w_dt, *, tm, tn, n_lat, S):
    M, D = x.shape
    N = w_main.shape[1]

    def midx(i):
        return jnp.where(i * tm < n_lat, 1 + (i * tm) // S, 0)

    return pl.pallas_call(
        functools.partial(_inproj_kernel, tm=tm, rb=min(tm, 256)),
        grid=(M // tm, N // tn),
        in_specs=[pl.BlockSpec((tm, D), lambda i, j: (i, 0)),
                  pl.BlockSpec((1, 6, D), lambda i, j: (midx(i), 0, 0)),
                  pl.BlockSpec((D, tn), lambda i, j: (0, j)),
                  pl.BlockSpec((D, LANES), lambda i, j: (0, 0))],
        out_specs=[pl.BlockSpec((tm, tn), lambda i, j: (i, j)),
                   pl.BlockSpec((tm, LANES), lambda i, j: (i, 0))],
        out_shape=[jax.ShapeDtypeStruct((M, N), BF16), jax.ShapeDtypeStruct((M, LANES), F32)],
        scratch_shapes=[pltpu.VMEM((tm, D), BF16)],
        compiler_params=_cparams("parallel", "arbitrary"),
        name="inproj",
    )(x, mod, w_main, w_dt)


def _chunk_of_step(i, ncc, nlc, rev):
    if not rev:
        return i
    return jnp.where(i < ncc, ncc - 1 - i, ncc + nlc - 1 - (i - ncc))


def _block_of_chunk(b, c, B, ncc, nlc):
    return jnp.where(c < ncc, B * nlc + b * ncc + c, b * nlc + (c - ncc))


def _seq_specs(width, col, *, B, ncc, nlc, rev, n_rows):
    hb = SEQ_CHUNK // HALO
    last = n_rows // HALO - 1

    def blk(b, i):
        return _block_of_chunk(b, _chunk_of_step(i, ncc, nlc, rev), B, ncc, nlc)

    main = pl.BlockSpec((SEQ_CHUNK, width), lambda b, i: (blk(b, i), col))
    prev = pl.BlockSpec((HALO, width), lambda b, i: (jnp.maximum(blk(b, i) * hb - 1, 0), col))
    nxt = pl.BlockSpec((HALO, width), lambda b, i: (jnp.minimum((blk(b, i) + 1) * hb, last), col))
    return blk, main, prev, nxt


def _dwconv_chunk(x_ref, xp_ref, xn_ref, cw_ref, cb_ref, c, ncc, nch):
    T = SEQ_CHUNK
    x = x_ref[...].astype(F32)
    has_prev = jnp.where((c == 0) | (c == ncc), 0.0, 1.0)
    has_next = jnp.where((c == ncc - 1) | (c == nch - 1), 0.0, 1.0)
    p6 = xp_ref[HALO - 2:HALO - 1, :].astype(F32) * has_prev
    p7 = xp_ref[HALO - 1:HALO, :].astype(F32) * has_prev
    n0 = xn_ref[0:1, :].astype(F32) * has_next
    row = lax.broadcasted_iota(jnp.int32, (T, 1), 0)
    xm1 = jnp.where(row == 0, p7, pltpu.roll(x, 1, 0))
    xm2 = jnp.where(row == 0, p6, jnp.where(row == 1, p7, pltpu.roll(x, 2, 0)))
    xp1 = jnp.where(row == T - 1, n0, pltpu.roll(x, T - 1, 0))
    return (cw_ref[0:1, :] * xm2 + cw_ref[1:2, :] * xm1 + cw_ref[2:3, :] * x
            + cw_ref[3:4, :] * xp1 + cb_ref[...])


def _linear_scan(a, b, rev):
    T = a.shape[0]
    row = lax.broadcasted_iota(jnp.int32, (T, 1), 0)
    s = 1
    while s < T:
        if s % 8:
            valid = (row < T - s) if rev else (row >= s)
            shift = T - s if rev else s
            a_p = jnp.where(valid, pltpu.roll(a, shift, 0), 1.0)
            b_p = jnp.where(valid, pltpu.roll(b, shift, 0), 0.0)
            b = a * b_p + b
            a = a * a_p
        elif rev:
            b = jnp.concatenate([a[:T - s] * b[s:] + b[:T - s], b[T - s:]], axis=0)
            a = jnp.concatenate([a[:T - s] * a[s:], a[T - s:]], axis=0)
        else:
            b = jnp.concatenate([b[:s], a[s:] * b[:T - s] + b[s:]], axis=0)
            a = jnp.concatenate([a[:s], a[s:] * a[:T - s]], axis=0)
        s *= 2
    return a, b


def _lru_kernel(x_ref, xp_ref, xn_ref, cw_ref, cb_ref, wg_ref, gb_ref, lam_ref, o_ref, carry_ref,
                *, rev, ncc, nlc):
    T = SEQ_CHUNK
    i = pl.program_id(1)
    c = _chunk_of_step(i, ncc, nlc, rev)

    @pl.when(i == 0)
    def _():
        carry_ref[...] = jnp.zeros_like(carry_ref)

    xc = _dwconv_chunk(x_ref, xp_ref, xn_ref, cw_ref, cb_ref, c, ncc, ncc + nlc)
    sp = _softplus(-lam_ref[...])
    for n in range(LRU_BLOCKS):
        sl = slice(n * LRU_BLOCK_DIM, (n + 1) * LRU_BLOCK_DIM)
        xb = xc[:, sl]
        gl = _dot(xb.astype(BF16), wg_ref[n])
        r_gate = _sigmoid(gl[:, :LRU_BLOCK_DIM] + gb_ref[0:1, sl])
        i_gate = _sigmoid(gl[:, LRU_BLOCK_DIM:] + gb_ref[1:2, sl])
        log_a = -LRU_C * r_gate * sp[:, sl]
        a = jnp.exp(log_a)
        bb = jnp.sqrt(1.0 - a * a) * (i_gate * xb)
        a_s, b_s = _linear_scan(a, bb, rev)
        h = b_s + a_s * carry_ref[0:1, sl]
        o_ref[:, sl] = h.astype(o_ref.dtype)
        carry_ref[0:1, sl] = h[0:1, :] if rev else h[T - 1:T, :]


def _lru_dir(P, conv_w, conv_b, wg, gb, lam, *, rev, B, ncc, nlc):
    M = P.shape[0]
    W = LRU_WIDTH
    blk, main, prev, nxt = _seq_specs(W, COL_LRU_X, B=B, ncc=ncc, nlc=nlc, rev=rev, n_rows=M)
    full2 = lambda shape: pl.BlockSpec(shape, lambda b, i: (0, 0))
    return pl.pallas_call(
        functools.partial(_lru_kernel, rev=rev, ncc=ncc, nlc=nlc),
        grid=(B, ncc + nlc),
        in_specs=[main, prev, nxt, full2((4, W)), full2((1, W)),
                  pl.BlockSpec((LRU_BLOCKS, LRU_BLOCK_DIM, 2 * LRU_BLOCK_DIM), lambda b, i: (0, 0, 0)),
                  full2((2, W)), full2((1, W))],
        out_specs=pl.BlockSpec((SEQ_CHUNK, W), lambda b, i: (blk(b, i), 0)),
        out_shape=jax.ShapeDtypeStruct((M, W), BF16),
        scratch_shapes=[pltpu.VMEM((8, W), F32)],
        compiler_params=_cparams("parallel", "arbitrary"),
        name="lru_bwd" if rev else "lru_fwd",
    )(P, P, P, conv_w, conv_b, wg, gb, lam)


def _ssd_kernel(x_ref, xp_ref, xn_ref, dt_ref, cw_ref, cb_ref, sp_ref, o_ref, st_ref,
                *, rev, d, ncc, nlc, add_skip):
    T = SEQ_CHUNK
    N = SSD_STATE
    HPG = SSD_HEADS // SSD_GROUPS
    GW = HPG * SSD_HEAD_DIM
    i = pl.program_id(1)
    c = _chunk_of_step(i, ncc, nlc, rev)

    @pl.when(i == 0)
    def _():
        st_ref[...] = jnp.zeros_like(st_ref)

    xc = _dwconv_chunk(x_ref, xp_ref, xn_ref, cw_ref, cb_ref, c, ncc, ncc + nlc)
    xbc = xc * _sigmoid(xc)
    xs = xbc[:, :SSD_WIDTH]
    Bm = xbc[:, SSD_WIDTH:SSD_WIDTH + SSD_GROUPS * N]
    Cm = xbc[:, SSD_WIDTH + SSD_GROUPS * N:]

    lane = lax.broadcasted_iota(jnp.int32, (1, LANES), 1)
    in_dir = (lane >= SSD_HEADS * d) & (lane < SSD_HEADS * (d + 1))
    dt = jnp.where(in_dir, _softplus(dt_ref[...] + sp_ref[0:1, :]), 0.0)
    a_neg = jnp.where(in_dir, -jnp.exp(sp_ref[1:2, :]), 0.0)
    a = dt * a_neg

    r_i = lax.broadcasted_iota(jnp.int32, (T, T), 0)
    c_i = lax.broadcasted_iota(jnp.int32, (T, T), 1)
    tri = (c_i >= r_i) if rev else (c_i <= r_i)
    tri_b = jnp.where(tri, 1.0, 0.0).astype(BF16)
    cs = sum(_dot(tri_b, p) for p in _split3(a))
    e_r = lax.broadcasted_iota(jnp.int32, (LANES, LANES), 0)
    e_c = lax.broadcasted_iota(jnp.int32, (LANES, LANES), 1)
    eye = jnp.where(e_r == e_c, 1.0, 0.0).astype(BF16)
    cs_row = sum(_dot_nt(eye, p) for p in _split3(cs))
    cs_end = cs[0:1, :] if rev else cs[T - 1:T, :]

    x_r = lax.broadcasted_iota(jnp.int32, (LANES, SSD_WIDTH), 0)
    x_c = lax.broadcasted_iota(jnp.int32, (LANES, SSD_WIDTH), 1)
    expand_m = jnp.where(x_r == SSD_HEADS * d + x_c // SSD_HEAD_DIM, 1.0, 0.0).astype(BF16)

    def expand(v):
        return sum(_dot(p, expand_m) for p in _split3(v))

    dt_e = expand(dt)
    dec_out_e = expand(jnp.exp(cs))
    dec_st_e = expand(jnp.exp(cs_end - cs))
    row8 = lax.broadcasted_iota(jnp.int32, (8, LANES), 0)
    small = jnp.where(row8 == 0, jnp.exp(cs_end), jnp.where(row8 == 1, sp_ref[2:3, :], 0.0))
    small_e = expand(small)
    dec_ch_e = small_e[0:1, :]
    dskip_e = small_e[1:2, :]

    xdt = xs * dt_e
    xdt_b = xdt.astype(BF16)
    z_st = (xdt * dec_st_e).astype(BF16)
    lo_half = lax.broadcasted_iota(jnp.int32, (1, LANES), 1) < SSD_HEAD_DIM

    for g in range(SSD_GROUPS):
        Bg = Bm[:, g * N:(g + 1) * N].astype(BF16)
        Cg = Cm[:, g * N:(g + 1) * N].astype(BF16)
        CB = _dot_nt(Cg, Bg)
        gs = slice(g * GW, (g + 1) * GW)
        Sg = st_ref[:, gs]
        y_off = _dot(Cg, Sg.astype(BF16)) * dec_out_e[:, gs]
        for p in range(HPG // 2):
            col = g * GW + p * LANES
            x_pair = xdt_b[:, col:col + LANES]
            y_pair = y_off[:, p * LANES:(p + 1) * LANES]
            for q in range(2):
                hh = SSD_HEADS * d + g * HPG + 2 * p + q
                L = jnp.exp(jnp.where(tri, cs[:, hh:hh + 1] - cs_row[hh:hh + 1, :], NEG_BIG))
                Wm = (CB * L).astype(BF16)
                xm = jnp.where(lo_half if q == 0 else jnp.logical_not(lo_half), x_pair,
                               jnp.zeros_like(x_pair))
                y_pair = y_pair + _dot(Wm, xm)
            if add_skip:
                y_pair = y_pair + dskip_e[:, col:col + LANES] * xs[:, col:col + LANES]
            o_ref[:, col:col + LANES] = y_pair.astype(o_ref.dtype)
        BgT = _dot_nt(eye, Bg).astype(BF16)
        st_ref[:, gs] = dec_ch_e[:, gs] * Sg + _dot(BgT, z_st[:, gs])


def _ssd_dir(P, dtp, conv_w, conv_b, sp, *, rev, d, B, ncc, nlc, add_skip):
    M = P.shape[0]
    XW = SSD_WIDTH + 2 * SSD_GROUPS * SSD_STATE
    blk, main, prev, nxt = _seq_specs(XW, COL_XBC, B=B, ncc=ncc, nlc=nlc, rev=rev, n_rows=M)
    full2 = lambda shape: pl.BlockSpec(shape, lambda b, i: (0, 0))
    return pl.pallas_call(
        functools.partial(_ssd_kernel, rev=rev, d=d, ncc=ncc, nlc=nlc, add_skip=add_skip),
        grid=(B, ncc + nlc),
        in_specs=[main, prev, nxt,
                  pl.BlockSpec((SEQ_CHUNK, LANES), lambda b, i: (blk(b, i), 0)),
                  full2((4, XW)), full2((1, XW)), full2((8, LANES))],
        out_specs=pl.BlockSpec((SEQ_CHUNK, SSD_WIDTH), lambda b, i: (blk(b, i), 0)),
        out_shape=jax.ShapeDtypeStruct((M, SSD_WIDTH), BF16),
        scratch_shapes=[pltpu.VMEM((SSD_STATE, SSD_WIDTH), F32)],
        compiler_params=_cparams("parallel", "arbitrary"),
        name="ssd_bwd" if rev else "ssd_fwd",
    )(P, P, P, dtp, conv_w, conv_b, sp)


def _rope_kernel(q_ref, k_ref, c_ref, sa_ref, sb_ref, qo_ref, ko_ref, *, scale):
    W = q_ref.shape[1]
    reps = W // LANES
    cos = jnp.concatenate([c_ref[...]] * reps, axis=1)
    sa = jnp.concatenate([sa_ref[...]] * reps, axis=1)
    sb = jnp.concatenate([sb_ref[...]] * reps, axis=1)
    quarter = ATT_QK_DIM // 4

    def rot(x):
        return x * cos + pltpu.roll(x, W - quarter, 1) * sa + pltpu.roll(x, quarter, 1) * sb

    qo_ref[...] = (rot(q_ref[...].astype(F32)) * scale).astype(qo_ref.dtype)
    ko_ref[...] = rot(k_ref[...].astype(F32)).astype(ko_ref.dtype)


def _rope(P, cos, sa, sb, *, tm):
    M = P.shape[0]
    W = ATT_HEADS * 2 * ATT_QK_DIM
    tab = pl.BlockSpec((tm, LANES), lambda i: (i, 0))
    return pl.pallas_call(
        functools.partial(_rope_kernel, scale=ATT_QK_DIM ** -0.5),
        grid=(M // tm,),
        in_specs=[pl.BlockSpec((tm, W), lambda i: (i, COL_Q)),
                  pl.BlockSpec((tm, W), lambda i: (i, COL_K)), tab, tab, tab],
        out_specs=[pl.BlockSpec((tm, W), lambda i: (i, 0))] * 2,
        out_shape=[jax.ShapeDtypeStruct((M, W), BF16)] * 2,
        compiler_params=_cparams("parallel"),
        name="rope",
    )(P, P, cos, sa, sb)


def _attn_kernel(*refs, tq, tk, n_ctx, n_lat, lam_init):
    if n_lat:
        (q_ref, kc_ref, vc_ref, kl_ref, vl_ref, lv_ref, nw_ref, o_ref, m_ref, l_ref, acc_ref) = refs
    else:
        (q_ref, kc_ref, vc_ref, lv_ref, nw_ref, o_ref, m_ref, l_ref, acc_ref) = refs
        kl_ref = vl_ref = None

    q = q_ref[...]
    lane = lax.broadcasted_iota(jnp.int32, (1, LANES), 1)
    zero = jnp.zeros_like(q)
    qq = jnp.concatenate([jnp.where(lane < ATT_QK_DIM, q, zero),
                          jnp.where(lane >= ATT_QK_DIM, q, zero)], axis=0)

    m_ref[...] = jnp.full(m_ref.shape, NEG_BIG, F32)
    l_ref[...] = jnp.zeros_like(l_ref)
    acc_ref[...] = jnp.zeros_like(acc_ref)

    def lane_blocks(s):
        return [s[:, j * LANES:(j + 1) * LANES] for j in range(s.shape[1] // LANES)]

    def over_keys(fn):
        tc = min(n_ctx, tk)
        for j in range(n_ctx // tc):
            fn(kc_ref[j * tc:(j + 1) * tc, :], vc_ref[j * tc:(j + 1) * tc, :])
        if n_lat:
            def body(j, carry):
                rows = pl.ds(pl.multiple_of(j * tk, tk), tk)
                fn(kl_ref[rows, :], vl_ref[rows, :])
                return carry
            lax.fori_loop(0, n_lat // tk, body, 0, unroll=2)

    def step(kc, vc):
        blocks = lane_blocks(_dot_nt(qq, kc))
        m_prev = m_ref[...]
        smax = functools.reduce(jnp.maximum, blocks)
        m_new = jnp.maximum(m_prev, jnp.max(smax, axis=1, keepdims=True))
        alpha = jnp.exp(m_prev - m_new)
        p = [jnp.exp(sb - m_new) for sb in blocks]
        l_ref[...] = alpha * l_ref[...] + functools.reduce(jnp.add, p)
        pb = jnp.concatenate([x.astype(BF16) for x in p], axis=1)
        acc_ref[...] = alpha * acc_ref[...] + _dot(pb, vc)
        m_ref[...] = m_new

    over_keys(step)
    l_ref[...] = jnp.broadcast_to(jnp.sum(l_ref[...], axis=1, keepdims=True), l_ref.shape)

    lv = lv_ref[...]
    s01 = jnp.sum(lv[0:1, :] * lv[1:2, :], axis=1, keepdims=True)
    s23 = jnp.sum(lv[2:3, :] * lv[3:4, :], axis=1, keepdims=True)
    lam = jnp.exp(s01) - jnp.exp(s23) + lam_init
    acc = acc_ref[...]
    l = l_ref[...]
    o = acc[:tq, :] / l[:tq, :] - lam * (acc[tq:, :] / l[tq:, :])
    o = o * lax.rsqrt(jnp.mean(o * o, axis=-1, keepdims=True) + LN_EPS) * nw_ref[...]
    o_ref[...] = (o * (1.0 - lam_init)).astype(o_ref.dtype)


def _attention(qr, kr, P, lam_vecs, norm_w, out_prev, *, B, S, CTX, lam_init, ctx_queries):
    M = qr.shape[0]
    H = ATT_HEADS
    tk = min(S, 1024)
    assert S % tk == 0 and (CTX % tk == 0 or tk % CTX == 0)
    vcol = COL_V * (1024 // LANES)
    n_lat_blk = (B * S) // CTX
    if ctx_queries:
        tq = min(CTX, 256)
        nq = CTX // tq
        q_spec = pl.BlockSpec((tq, LANES), lambda b, h, i: ((B * S + b * CTX) // tq + i, h))
    else:
        tq = 512
        nq = S // tq
        q_spec = pl.BlockSpec((tq, LANES), lambda b, h, i: (b * nq + i, h))
    in_specs = [q_spec,
                pl.BlockSpec((CTX, LANES), lambda b, h, i: (n_lat_blk + b, h)),
                pl.BlockSpec((CTX, LANES), lambda b, h, i: (n_lat_blk + b, vcol + h))]
    args = [qr, kr, P]
    if not ctx_queries:
        in_specs += [pl.BlockSpec((S, LANES), lambda b, h, i: (b, h)),
                     pl.BlockSpec((S, LANES), lambda b, h, i: (b, vcol + h))]
        args += [kr, P]
    in_specs += [pl.BlockSpec((4, ATT_QK_DIM), lambda b, h, i: (0, 0)),
                 pl.BlockSpec((1, LANES), lambda b, h, i: (0, 0))]
    args += [lam_vecs, norm_w]
    kwargs = {}
    if out_prev is not None:
        in_specs.append(pl.BlockSpec(memory_space=pl.ANY))
        args.append(out_prev)
        kwargs["input_output_aliases"] = {len(args) - 1: 0}

    kernel = functools.partial(_attn_kernel, tq=tq, tk=tk, n_ctx=CTX, n_lat=0 if ctx_queries else S,
                               lam_init=lam_init)
    if out_prev is not None:
        base = kernel
        kernel = lambda *refs: base(*refs[:len(args) - 1], *refs[len(args):])
    return pl.pallas_call(
        kernel,
        grid=(B, H, nq),
        in_specs=in_specs,
        out_specs=q_spec,
        out_shape=jax.ShapeDtypeStruct((M, H * ATT_V_DIM), BF16),
        scratch_shapes=[pltpu.VMEM((2 * tq, LANES), F32)] * 3,
        compiler_params=_cparams("parallel", "parallel", "arbitrary"),
        name="attn_ctx" if ctx_queries else "attn_lat",
        **kwargs,
    )(*args)


def _merge_kernel(hf_ref, hb_ref, gl_ref, yf_ref, yb_ref, z_ref, att_ref, g0_ref, g1_ref, g2_ref,
                  wb_ref, nw_ref, o_ref, br_ref, d_ref, *, tm, rb):
    def row_blocks(body):
        def step(r, carry):
            body(pl.ds(pl.multiple_of(r * rb, rb), rb))
            return carry
        lax.fori_loop(0, tm // rb, step, 0)

    @pl.when(pl.program_id(1) == 0)
    def _():
        def finish(rows):
            y = hf_ref[rows, :].astype(F32) + hb_ref[rows, :].astype(F32)
            g = gl_ref[rows, :].astype(F32)
            gelu = 0.5 * g * (1.0 + jnp.tanh(0.7978845608028654 * (g + 0.044715 * (g * g * g))))
            br_ref[0, rows, :] = (y * gelu).astype(BF16)
            ys = yf_ref[rows, :].astype(F32) + yb_ref[rows, :].astype(F32)
            zz = z_ref[rows, :].astype(F32)
            t = ys * (zz * _sigmoid(zz))
            t = t * lax.rsqrt(jnp.mean(t * t, axis=-1, keepdims=True) + LN_EPS) * nw_ref[...]
            br_ref[1, rows, :] = t.astype(BF16)
        row_blocks(finish)
        br_ref[2] = att_ref[...]

    for k in range(3):
        d_ref[k] = _dot(br_ref[k], wb_ref[k])

    def gate(rows):
        acc = _sigmoid(g0_ref[rows, :].astype(F32)) * d_ref[0, rows, :]
        acc = acc + _sigmoid(g1_ref[rows, :].astype(F32)) * d_ref[1, rows, :]
        acc = acc + _sigmoid(g2_ref[rows, :].astype(F32)) * d_ref[2, rows, :]
        o_ref[rows, :] = acc.astype(o_ref.dtype)
    row_blocks(gate)


def _merge(hf, hb, yf, yb, att, P, wb, ssd_norm_w, *, n_rows, tm, tn):
    M = P.shape[0]
    D = wb.shape[2]
    BW = wb.shape[1]
    row = lambda col: pl.BlockSpec((tm, BW), lambda i, j: (i, col))
    gcol = COL_GATES * 1024 // tn
    gate = lambda k: pl.BlockSpec((tm, tn), lambda i, j: (i, gcol + k * (D // tn) + j))
    return pl.pallas_call(
        functools.partial(_merge_kernel, tm=tm, rb=min(tm, 64)),
        grid=(n_rows // tm, D // tn),
        in_specs=[row(0), row(0), row(COL_LRU_G), row(0), row(0), row(COL_Z), row(0),
                  gate(0), gate(1), gate(2),
                  pl.BlockSpec((3, BW, tn), lambda i, j: (0, 0, j)),
                  pl.BlockSpec((1, BW), lambda i, j: (0, 0))],
        out_specs=pl.BlockSpec((tm, tn), lambda i, j: (i, j)),
        out_shape=jax.ShapeDtypeStruct((M, D), BF16),
        scratch_shapes=[pltpu.VMEM((3, tm, BW), BF16), pltpu.VMEM((3, tm, tn), F32)],
        compiler_params=_cparams("parallel", "arbitrary"),
        name="merge",
    )(hf, hb, P, yf, yb, P, att, P, P, P, wb, ssd_norm_w)


def _route(logits, rbias):
    G, E = N_EXPERT_GROUPS, EXPERTS_PER_GROUP
    mx = jnp.max(logits, axis=0, keepdims=True)
    ex = jnp.exp(logits - mx)
    sc = ex / jnp.sum(ex, axis=0, keepdims=True)
    sel = sc + rbias
    srow = [sel[r:r + 1, :] for r in range(G * E)]
    prow = [sc[r:r + 1, :] for r in range(G * E)]
    gscore = []
    for g in range(G):
        v = srow[g * E:(g + 1) * E]
        best = v[0] + v[1]
        for a in range(E):
            for b in range(a + 1, E):
                if (a, b) != (0, 1):
                    best = jnp.maximum(best, v[a] + v[b])
        gscore.append(best)
    gmax = functools.reduce(jnp.maximum, gscore)
    gidx = jnp.full(gmax.shape, G - 1, jnp.int32)
    for g in range(G - 2, -1, -1):
        gidx = jnp.where(gscore[g] == gmax, g, gidx)

    def pick(rows, k):
        out = rows[(G - 1) * E + k]
        for g in range(G - 2, -1, -1):
            out = jnp.where(gidx == g, rows[g * E + k], out)
        return out

    v = [pick(srow, k) for k in range(E)]
    p = [pick(prow, k) for k in range(E)]
    chosen = []
    for k in range(E):
        rank = jnp.zeros(gmax.shape, jnp.int32)
        for j in range(E):
            if j == k:
                continue
            ahead = (v[j] > v[k]) | ((v[j] == v[k]) & (j < k)) if j < k else (v[j] > v[k])
            rank = rank + jnp.where(ahead, 1, 0)
        chosen.append(rank < 2)
    first = jnp.full(gmax.shape, E, jnp.int32)
    second = jnp.full(gmax.shape, -1, jnp.int32)
    for k in range(E):
        first = jnp.minimum(first, jnp.where(chosen[k], k, E))
        second = jnp.maximum(second, jnp.where(chosen[k], k, -1))
    p_first = jnp.zeros(gmax.shape, F32)
    p_second = jnp.zeros(gmax.shape, F32)
    for k in range(E):
        p_first = p_first + jnp.where(first == k, p[k], 0.0)
        p_second = p_second + jnp.where(second == k, p[k], 0.0)
    tot = p_first + p_second
    return gidx * E + first, gidx * E + second, p_first / tot, p_second / tot


def _outproj_kernel(m_ref, w_ref, x_ref, mod_ref, lng_ref, lnb_ref, wr_ref, rb_ref,
                    x1_ref, h2_ref, ids_ref, wts_ref, *, tm, alpha):
    o = _dot(m_ref[...], w_ref[...])
    g1 = mod_ref[0, 2:3, :]
    sh2 = mod_ref[0, 3:4, :]
    sc2 = mod_ref[0, 4:5, :]
    x1 = _ln(alpha * x_ref[...] + g1 * o) * lng_ref[...] + lnb_ref[...]
    x1_ref[...] = x1
    h2 = _ln(x1) * (1.0 + sc2) + sh2
    _store_slabs(h2_ref, h2, tm)
    h_hi, h_lo = _split2(h2)
    w_hi, w_lo = _split2(wr_ref[...])
    logits = _dot_nt(w_hi, h_hi) + _dot_nt(w_lo, h_hi) + _dot_nt(w_hi, h_lo)
    e0, e1, w0, w1 = _route(logits, rb_ref[...])
    row = lax.broadcasted_iota(jnp.int32, (8, tm), 0)
    ids_ref[...] = jnp.where(row == 0, e0, jnp.where(row == 1, e1, 0))
    wts_ref[...] = jnp.where(row == 0, w0, jnp.where(row == 1, w1, 0.0))


def _outproj(m, w_out, x, mod, ln_g, ln_b, wr_t, rbias, *, n_rows, tm, n_lat, S, alpha):
    M, D = x.shape

    def midx(i):
        return jnp.where(i * tm < n_lat, 1 + (i * tm) // S, 0)

    vec = pl.BlockSpec((1, D), lambda i: (0, 0))
    return pl.pallas_call(
        functools.partial(_outproj_kernel, tm=tm, alpha=alpha),
        grid=(n_rows // tm,),
        in_specs=[pl.BlockSpec((tm, D), lambda i: (i, 0)),
                  pl.BlockSpec((D, D), lambda i: (0, 0)),
                  pl.BlockSpec((tm, D), lambda i: (i, 0)),
                  pl.BlockSpec((1, 6, D), lambda i: (midx(i), 0, 0)),
                  vec, vec,
                  pl.BlockSpec((N_EXPERTS, D), lambda i: (0, 0)),
                  pl.BlockSpec((N_EXPERTS, 1), lambda i: (0, 0))],
        out_specs=[pl.BlockSpec((tm, D), lambda i: (i, 0)),
                   pl.BlockSpec((tm * SLAB, LANES), lambda i: (i, 0)),
                   pl.BlockSpec((8, tm), lambda i: (0, i)),
                   pl.BlockSpec((8, tm), lambda i: (0, i))],
        out_shape=[jax.ShapeDtypeStruct((M, D), F32),
                   jax.ShapeDtypeStruct((n_rows * SLAB, LANES), jnp.uint32),
                   jax.ShapeDtypeStruct((8, n_rows), jnp.int32),
                   jax.ShapeDtypeStruct((8, n_rows), F32)],
        compiler_params=_cparams("parallel"),
        name="outproj",
    )(m, w_out, x, mod, ln_g, ln_b, wr_t, rbias)


def _ffn_kernel(te_ref, tv_ref, src_ref, dst_ref, h_hbm, wg_ref, wu_ref, wd_ref, y_hbm,
                xbuf, ybuf, xb_ref, gsem, ssem, *, tm):
    i = pl.program_id(0)
    n = pl.num_programs(0)
    slot = i % 2

    def gather_start(tile, s):
        def issue(r2, carry):
            for prio in range(2):
                r = 2 * r2 + prio
                tok = src_ref[tile * tm + r]
                pltpu.make_async_copy(h_hbm.at[pl.ds(pl.multiple_of(tok * SLAB, SLAB), SLAB)],
                                      xbuf.at[s, pl.ds(pl.multiple_of(r * SLAB, SLAB), SLAB)],
                                      gsem.at[s]).start(priority=prio)
            return carry
        lax.fori_loop(0, tm // 2, issue, 0)

    def gather_wait(s):
        pltpu.make_async_copy(h_hbm.at[pl.ds(0, tm * SLAB)], xbuf.at[s], gsem.at[s]).wait()

    def scatter_start(tile):
        def issue(r2, carry):
            for prio in range(2):
                r = 2 * r2 + prio
                tok = dst_ref[tile * tm + r]
                pltpu.make_async_copy(ybuf.at[pl.ds(pl.multiple_of(r * SLAB, SLAB), SLAB)],
                                      y_hbm.at[pl.ds(pl.multiple_of(tok * SLAB, SLAB), SLAB)],
                                      ssem).start(priority=prio)
            return carry
        lax.fori_loop(0, tm // 2, issue, 0)

    def scatter_wait():
        pltpu.make_async_copy(ybuf, y_hbm.at[pl.ds(0, tm * SLAB)], ssem).wait()

    @pl.when(i == 0)
    def _():
        gather_start(0, 0)

    @pl.when(tv_ref[i] != 0)
    def _():
        has_next = jnp.logical_and(i + 1 < n, tv_ref[jnp.minimum(i + 1, n - 1)] != 0)

        @pl.when(has_next)
        def _():
            gather_start(i + 1, 1 - slot)

        gather_wait(slot)
        half = SLAB * LANES
        for k in range(SLAB):
            lo, hi = _load_slab_piece(xbuf[slot, pl.ds(k, tm, stride=SLAB), :])
            xb_ref[:, k * LANES:(k + 1) * LANES] = lo.astype(BF16)
            xb_ref[:, half + k * LANES:half + (k + 1) * LANES] = hi.astype(BF16)
        x = xb_ref[...]
        g = _dot(x, wg_ref[0])
        u = _dot(x, wu_ref[0])
        act = (g * _sigmoid(g) * u).astype(BF16)
        y = _dot(act, wd_ref[0])

        @pl.when(i > 0)
        def _():
            scatter_wait()

        _store_slabs(ybuf, y, tm)
        scatter_start(i)

        @pl.when(jnp.logical_not(has_next))
        def _():
            scatter_wait()


def _expert_ffn(h2, src_tok, dst_tok, tile_e, tile_v, wg, wu, wd, *, tm, n_out):
    n_tiles = tile_e.shape[0]
    E, D, F = wg.shape
    return pl.pallas_call(
        functools.partial(_ffn_kernel, tm=tm),
        grid_spec=pltpu.PrefetchScalarGridSpec(
            num_scalar_prefetch=4,
            grid=(n_tiles,),
            in_specs=[pl.BlockSpec(memory_space=pl.ANY),
                      pl.BlockSpec((1, D, F), lambda i, te, tv, sr, ds: (te[i], 0, 0)),
                      pl.BlockSpec((1, D, F), lambda i, te, tv, sr, ds: (te[i], 0, 0)),
                      pl.BlockSpec((1, F, D), lambda i, te, tv, sr, ds: (te[i], 0, 0))],
            out_specs=pl.BlockSpec(memory_space=pl.ANY),
            scratch_shapes=[pltpu.VMEM((2, tm * SLAB, LANES), jnp.uint32),
                            pltpu.VMEM((tm * SLAB, LANES), jnp.uint32),
                            pltpu.VMEM((tm, D), BF16),
                            pltpu.SemaphoreType.DMA((2,)),
                            pltpu.SemaphoreType.DMA(())]),
        out_shape=jax.ShapeDtypeStruct((n_out * SLAB, LANES), jnp.uint32),
        compiler_params=_cparams("arbitrary"),
        name="expert_ffn",
    )(tile_e, tile_v, src_tok, dst_tok, h2, wg, wu, wd)


def _final_kernel(x_ref, y0_ref, y1_ref, w_ref, mod_ref, lng_ref, lnb_ref, o_ref, f_ref, *, tm, alpha):
    half = SLAB * LANES
    w0 = w_ref[:, 0:1]
    w1 = w_ref[:, 1:2]
    for k in range(SLAB):
        rows = pl.ds(k, tm, stride=SLAB)
        lo0, hi0 = _load_slab_piece(y0_ref[rows, :])
        lo1, hi1 = _load_slab_piece(y1_ref[rows, :])
        f_ref[:, k * LANES:(k + 1) * LANES] = w0 * lo0 + w1 * lo1
        f_ref[:, half + k * LANES:half + (k + 1) * LANES] = w0 * hi0 + w1 * hi1
    g2 = mod_ref[0, 5:6, :]
    o_ref[...] = _ln(alpha * x_ref[...] + g2 * f_ref[...]) * lng_ref[...] + lnb_ref[...]


def _final(x1, yg, wcol, mod, ln_g, ln_b, *, n_rows, tm, n_lat, S, alpha):
    D = x1.shape[1]

    def midx(i):
        return jnp.where(i * tm < n_lat, 1 + (i * tm) // S, 0)

    nt = n_rows // tm
    vec = pl.BlockSpec((1, D), lambda i: (0, 0))
    return pl.pallas_call(
        functools.partial(_final_kernel, tm=tm, alpha=alpha),
        grid=(nt,),
        in_specs=[pl.BlockSpec((tm, D), lambda i: (i, 0)),
                  pl.BlockSpec((tm * SLAB, LANES), lambda i: (i, 0)),
                  pl.BlockSpec((tm * SLAB, LANES), lambda i: (nt + i, 0)),
                  pl.BlockSpec((tm, 2), lambda i: (i, 0)),
                  pl.BlockSpec((1, 6, D), lambda i: (midx(i), 0, 0)),
                  vec, vec],
        out_specs=pl.BlockSpec((tm, D), lambda i: (i, 0)),
        out_shape=jax.ShapeDtypeStruct((n_rows, D), F32),
        scratch_shapes=[pltpu.VMEM((tm, D), F32)],
        compiler_params=_cparams("parallel"),
        name="final",
    )(x1, yg, yg, wcol, mod, ln_g, ln_b)


def _rope_tables(B, S, CTX):
    t = jnp.arange(S)
    row = (t // GRID_W).astype(F32)
    col = (t % GRID_W).astype(F32)
    axis_dim = ATT_QK_DIM // 2
    freqs = ROPE_THETA ** (-jnp.arange(0, axis_dim, 2, dtype=F32) / axis_dim)
    ang = jnp.stack([row[:, None] * freqs, col[:, None] * freqs], axis=1)
    cos, sin = jnp.cos(ang), jnp.sin(ang)
    zero = jnp.zeros_like(sin)
    c64 = jnp.concatenate([cos, cos], axis=-1).reshape(S, ATT_QK_DIM)
    sa64 = jnp.concatenate([-sin, zero], axis=-1).reshape(S, ATT_QK_DIM)
    sb64 = jnp.concatenate([zero, sin], axis=-1).reshape(S, ATT_QK_DIM)

    def full(tab, ctx_val):
        lat = jnp.tile(jnp.concatenate([tab, tab], axis=-1), (B, 1))
        ctx = jnp.full((B * CTX, LANES), ctx_val, F32)
        return jnp.concatenate([lat, ctx], axis=0)

    return full(c64, 1.0), full(sa64, 0.0), full(sb64, 0.0)


def _moe_plan(ids, tm):
    n_tok = ids.shape[1]
    e_flat = ids.reshape(-1)
    onehot = (e_flat[:, None] == jnp.arange(N_EXPERTS, dtype=jnp.int32)[None, :]).astype(jnp.int32)
    csum = jnp.cumsum(onehot, axis=0)
    counts = csum[-1]
    rank = jnp.take_along_axis(csum, e_flat[:, None], axis=1)[:, 0] - 1
    padded = ((counts + tm - 1) // tm) * tm
    ends = jnp.cumsum(padded)
    starts = ends - padded
    pos = (starts[e_flat] + rank).astype(jnp.int32)
    n_slots = 2 * n_tok + N_EXPERTS * tm
    slot_of = jnp.full((n_slots,), -1, jnp.int32).at[pos].set(jnp.arange(2 * n_tok, dtype=jnp.int32))
    real = slot_of >= 0
    src_tok = jnp.where(real, slot_of % n_tok, 0)
    dst_tok = jnp.where(real, slot_of, 2 * n_tok + jnp.arange(n_slots, dtype=jnp.int32) % tm)
    tile_start = jnp.arange(n_slots // tm, dtype=jnp.int32) * tm
    tile_e = jnp.minimum(jnp.sum((ends[None, :] <= tile_start[:, None]).astype(jnp.int32), axis=1),
                         N_EXPERTS - 1).astype(jnp.int32)
    tile_v = (tile_start < ends[-1]).astype(jnp.int32)
    return src_tok, dst_tok, tile_e, tile_v


def _row_tile(limit, *dims):
    t = limit
    while any(d % t for d in dims):
        t //= 2
    return t


def kernel(x, c, ctx, c_ctx, w_ada, b_ada, w_in, lru_conv_w, lru_conv_b, lru_gate_w, lru_gate_b, lru_lambda, ssd_conv_w, ssd_conv_b, ssd_a_log, ssd_dt_bias, ssd_d, ssd_norm_w, att_lambda, att_norm_w, w_branch, w_out, ln_g, ln_b, w_router, router_bias, w_gate_e, w_up_e, w_down_e):
    B, S, D = x.shape
    CTX = ctx.shape[1]
    depth = w_ada.shape[0]
    alpha = (2 * depth) ** 0.25
    n_lat = B * S
    n_ctx = B * CTX
    M = n_lat + n_ctx
    assert D == 2 * SLAB * LANES and CTX % SEQ_CHUNK == 0 and S % SEQ_CHUNK == 0 and S % CTX == 0
    ncc, nlc = CTX // SEQ_CHUNK, S // SEQ_CHUNK

    xs = jnp.concatenate([x.reshape(n_lat, D), ctx.reshape(n_ctx, D)], axis=0)
    cvec = jnp.concatenate([c_ctx[None, :], c, jnp.zeros((8 - 1 - B, D), F32)], axis=0)
    cos_t, sa_t, sb_t = _rope_tables(B, S, CTX)
    wr_t = w_router.T
    rbias = router_bias.reshape(N_EXPERTS, 1)

    tm_big = _row_tile(1024, S, n_ctx)
    tm_mid = _row_tile(512, S, n_ctx)
    tm_small = _row_tile(256, S, n_ctx)

    i0, i1, i2, i3, i4, i5, i6, i7, i8 = [int(v) for v in
                                          (0, 1024, 2048, 3072, 5120, 5152, 6176, 7200, 8224)]
    for l in range(depth):
        need_ctx = l < depth - 1
        lam_init = 0.8 - 0.6 * math.exp(-0.3 * l)
        n_rows = M if need_ctx else n_lat

        wl = w_in[l]
        w_main = jnp.concatenate([wl[:, i3:i4], wl[:, i0:i3], wl[:, i5:]], axis=1).astype(BF16)
        w_dt = jnp.pad(wl[:, i4:i5], ((0, 0), (0, LANES - (i5 - i4)))).astype(BF16)
        gw = lru_gate_w[l]
        wg_lru = jnp.concatenate([gw[:, 0], gw[:, 1]], axis=-1).astype(BF16)

        mod = _ada(cvec, w_ada, b_ada[l], l).reshape(8, 6, D)
        P, dtp = _inproj(xs, mod, w_main, w_dt, tm=tm_big, tn=1024, n_lat=n_lat, S=S)

        h_dirs, y_dirs = [], []
        for d in range(2):
            rev = d == 1
            h_dirs.append(_lru_dir(P, lru_conv_w[l], lru_conv_b[l].reshape(1, -1), wg_lru[d],
                                   lru_gate_b[l, d], lru_lambda[l, d].reshape(1, -1),
                                   rev=rev, B=B, ncc=ncc, nlc=nlc))
            pad = ((0, 0), (SSD_HEADS * d, LANES - SSD_HEADS * (d + 1)))
            sp = jnp.concatenate([jnp.pad(ssd_dt_bias[l, d][None, :], pad),
                                  jnp.pad(ssd_a_log[l, d][None, :], pad),
                                  jnp.pad(ssd_d[l][None, :], pad),
                                  jnp.zeros((5, LANES), F32)], axis=0)
            y_dirs.append(_ssd_dir(P, dtp, ssd_conv_w[l], ssd_conv_b[l].reshape(1, -1), sp,
                                   rev=rev, d=d, B=B, ncc=ncc, nlc=nlc, add_skip=(d == 0)))

        qr, kr = _rope(P, cos_t, sa_t, sb_t, tm=tm_mid)
        nw_att = att_norm_w[l].reshape(1, LANES)
        att = _attention(qr, kr, P, att_lambda[l], nw_att, None, B=B, S=S, CTX=CTX,
                         lam_init=lam_init, ctx_queries=False)
        if need_ctx:
            att = _attention(qr, kr, P, att_lambda[l], nw_att, att, B=B, S=S, CTX=CTX,
                             lam_init=lam_init, ctx_queries=True)

        m = _merge(h_dirs[0], h_dirs[1], y_dirs[0], y_dirs[1], att, P, w_branch[l].astype(BF16),
                   ssd_norm_w[l].reshape(1, -1), n_rows=n_rows, tm=tm_mid, tn=512)
        x1, h2, ids, wts = _outproj(m, w_out[l].astype(BF16), xs, mod, ln_g[l, 0].reshape(1, D),
                                    ln_b[l, 0].reshape(1, D), wr_t, rbias, n_rows=n_rows,
                                    tm=tm_small, n_lat=n_lat, S=S, alpha=alpha)

        src_tok, dst_tok, tile_e, tile_v = _moe_plan(ids[:2], MOE_TILE)
        yg = _expert_ffn(h2, src_tok, dst_tok, tile_e, tile_v, _cast_bf16(w_gate_e, l),
                         _cast_bf16(w_up_e, l), _cast_bf16(w_down_e, l), tm=MOE_TILE,
                         n_out=2 * n_rows + MOE_TILE)
        xs = _final(x1, yg, wts[:2].T, mod, ln_g[l, 1].reshape(1, D), ln_b[l, 1].reshape(1, D),
                    n_rows=n_rows, tm=tm_small, n_lat=n_lat, S=S, alpha=alpha)

    return xs[:n_lat].reshape(B, S, D)
```

```python
import functools
import math

import jax
import jax.numpy as jnp
from jax import lax
from jax.experimental import pallas as pl
from jax.experimental.pallas import tpu as pltpu

F32 = jnp.float32
BF16 = jnp.bfloat16

GRID_W = 64
ATT_HEADS = 8
ATT_QK_DIM = 64
ATT_V_DIM = 128
ROPE_THETA = 10000.0
SSD_HEADS = 16
SSD_HEAD_DIM = 64
SSD_GROUPS = 4
SSD_STATE = 128
SSD_WIDTH = SSD_HEADS * SSD_HEAD_DIM
LRU_WIDTH = 1024
LRU_BLOCKS = 8
LRU_BLOCK_DIM = LRU_WIDTH // LRU_BLOCKS
LRU_C = 8.0
N_EXPERTS = 16
N_EXPERT_GROUPS = 4
EXPERTS_PER_GROUP = 4
LN_EPS = 1e-6

LANES = 128
SLAB = 8
SEQ_CHUNK = 256
HALO = 16
MOE_TILE = 256
NEG_BIG = -1e30
VMEM_LIMIT = 56 * 2 ** 20

COL_XBC, COL_LRU_X, COL_LRU_G, COL_Z, COL_Q, COL_K, COL_V, COL_GATES = 0, 2, 3, 4, 5, 6, 7, 8


def _cparams(*sem):
    return pltpu.CompilerParams(dimension_semantics=sem, vmem_limit_bytes=VMEM_LIMIT)


def _dot(a, b):
    return jnp.dot(a, b, preferred_element_type=F32)


def _dot_nt(a, b):
    return lax.dot_general(a, b, (((1,), (1,)), ((), ())), preferred_element_type=F32)


def _split3(v):
    hi = v.astype(BF16)
    r = v - hi.astype(F32)
    mid = r.astype(BF16)
    lo = (r - mid.astype(F32)).astype(BF16)
    return hi, mid, lo


def _split2(v):
    hi = v.astype(BF16)
    lo = (v - hi.astype(F32)).astype(BF16)
    return hi, lo


def _store_slabs(ref, x, tm):
    half = SLAB * LANES
    for k in range(SLAB):
        lo = lax.bitcast_convert_type(x[:, k * LANES:(k + 1) * LANES].astype(BF16).astype(F32), jnp.uint32)
        hi = lax.bitcast_convert_type(
            x[:, half + k * LANES:half + (k + 1) * LANES].astype(BF16).astype(F32), jnp.uint32)
        ref[pl.ds(k, tm, stride=SLAB), :] = hi | (lo >> 16)


def _load_slab_piece(u):
    lo = lax.bitcast_convert_type(u << 16, F32)
    hi = lax.bitcast_convert_type(u & jnp.uint32(0xFFFF0000), F32)
    return lo, hi


def _ln(x):
    mu = jnp.mean(x, axis=-1, keepdims=True)
    xc = x - mu
    var = jnp.mean(xc * xc, axis=-1, keepdims=True)
    return xc * lax.rsqrt(var + LN_EPS)


def _sigmoid(x):
    return jax.nn.sigmoid(x)


def _softplus(x):
    return jnp.maximum(x, 0.0) + jnp.log1p(jnp.exp(-jnp.abs(x)))


def _cast_kernel(w_ref, o_ref):
    o_ref[...] = w_ref[0].astype(o_ref.dtype)


def _cast_bf16(w, l):
    _, E, R, C = w.shape
    return pl.pallas_call(
        _cast_kernel,
        grid=(E,),
        in_specs=[pl.BlockSpec((1, 1, R, C), lambda e: (l, e, 0, 0))],
        out_specs=pl.BlockSpec((1, R, C), lambda e: (e, 0, 0)),
        out_shape=jax.ShapeDtypeStruct((E, R, C), BF16),
        compiler_params=_cparams("parallel"),
        name="cast_bf16",
    )(w)


def _ada_kernel(c_ref, w_ref, b_ref, o_ref):
    c = c_ref[...]
    a = c * _sigmoid(c)
    a_hi, a_lo = _split2(a)
    w_hi, w_lo = _split2(w_ref[0])
    o_ref[...] = _dot(a_hi, w_hi) + _dot(a_lo, w_hi) + _dot(a_hi, w_lo) + b_ref[...]


def _ada(cvec, w, b, l):
    R, D = cvec.shape
    N = w.shape[2]
    tn = 1024
    return pl.pallas_call(
        _ada_kernel,
        grid=(N // tn,),
        in_specs=[pl.BlockSpec((R, D), lambda j: (0, 0)),
                  pl.BlockSpec((1, D, tn), lambda j: (l, 0, j)),
                  pl.BlockSpec((1, tn), lambda j: (0, j))],
        out_specs=pl.BlockSpec((R, tn), lambda j: (0, j)),
        out_shape=jax.ShapeDtypeStruct((R, N), F32),
        compiler_params=_cparams("arbitrary"),
        name="ada",
    )(cvec, w, b.reshape(1, N))


def _inproj_kernel(x_ref, mod_ref, w_ref, wdt_ref, o_ref, dt_ref, h_ref, *, tm, rb):
    @pl.when(pl.program_id(1) == 0)
    def _():
        sh = mod_ref[0, 0:1, :]
        sc = mod_ref[0, 1:2, :]

        def body(r, carry):
            rows = pl.ds(pl.multiple_of(r * rb, rb), rb)
            h = _ln(x_ref[rows, :]) * (1.0 + sc) + sh
            hb = h.astype(BF16)
            h_ref[rows, :] = hb
            dt_ref[rows, :] = _dot(hb, wdt_ref[...])
            return carry

        lax.fori_loop(0, tm // rb, body, 0)

    o_ref[...] = _dot(h_ref[...], w_ref[...]).astype(o_ref.dtype)


def _inproj(x, mod, w_main, w_dt, *, tm, tn, n_lat, S):
    M, D = x.shape
    N = w_main.shape[1]

    def midx(i):
        return jnp.where(i * tm < n_lat, 1 + (i * tm) // S, 0)

    return pl.pallas_call(
        functools.partial(_inproj_kernel, tm=tm, rb=min(tm, 256)),
        grid=(M // tm, N // tn),
        in_specs=[pl.BlockSpec((tm, D), lambda i, j: (i, 0)),
                  pl.BlockSpec((1, 6, D), lambda i, j: (midx(i), 0, 0)),
                  pl.BlockSpec((D, tn), lambda i, j: (0, j)),
                  pl.BlockSpec((D, LANES), lambda i, j: (0, 0))],
        out_specs=[pl.BlockSpec((tm, tn), lambda i, j: (i, j)),
                   pl.BlockSpec((tm, LANES), lambda i, j: (i, 0))],
        out_shape=[jax.ShapeDtypeStruct((M, N), BF16), jax.ShapeDtypeStruct((M, LANES), F32)],
        scratch_shapes=[pltpu.VMEM((tm, D), BF16)],
        compiler_params=_cparams("parallel", "arbitrary"),
        name="inproj",
    )(x, mod, w_main, w_dt)


def _chunk_of_step(i, ncc, nlc, rev):
    if not rev:
        return i
    return jnp.where(i < ncc, ncc - 1 - i, ncc + nlc - 1 - (i - ncc))


def _block_of_chunk(b, c, B, ncc, nlc):
    return jnp.where(c < ncc, B * nlc + b * ncc + c, b * nlc + (c - ncc))


def _seq_specs(width, col, *, B, ncc, nlc, rev, n_rows):
    hb = SEQ_CHUNK // HALO
    last = n_rows // HALO - 1

    def blk(b, i):
        return _block_of_chunk(b, _chunk_of_step(i, ncc, nlc, rev), B, ncc, nlc)

    main = pl.BlockSpec((SEQ_CHUNK, width), lambda b, i: (blk(b, i), col))
    prev = pl.BlockSpec((HALO, width), lambda b, i: (jnp.maximum(blk(b, i) * hb - 1, 0), col))
    nxt = pl.BlockSpec((HALO, width), lambda b, i: (jnp.minimum((blk(b, i) + 1) * hb, last), col))
    return blk, main, prev, nxt


def _dwconv_chunk(x_ref, xp_ref, xn_ref, cw_ref, cb_ref, c, ncc, nch):
    T = SEQ_CHUNK
    x = x_ref[...].astype(F32)
    has_prev = jnp.where((c == 0) | (c == ncc), 0.0, 1.0)
    has_next = jnp.where((c == ncc - 1) | (c == nch - 1), 0.0, 1.0)
    p6 = xp_ref[HALO - 2:HALO - 1, :].astype(F32) * has_prev
    p7 = xp_ref[HALO - 1:HALO, :].astype(F32) * has_prev
    n0 = xn_ref[0:1, :].astype(F32) * has_next
    row = lax.broadcasted_iota(jnp.int32, (T, 1), 0)
    xm1 = jnp.where(row == 0, p7, pltpu.roll(x, 1, 0))
    xm2 = jnp.where(row == 0, p6, jnp.where(row == 1, p7, pltpu.roll(x, 2, 0)))
    xp1 = jnp.where(row == T - 1, n0, pltpu.roll(x, T - 1, 0))
    return (cw_ref[0:1, :] * xm2 + cw_ref[1:2, :] * xm1 + cw_ref[2:3, :] * x
            + cw_ref[3:4, :] * xp1 + cb_ref[...])


def _linear_scan(a, b, rev):
    T = a.shape[0]
    row = lax.broadcasted_iota(jnp.int32, (T, 1), 0)
    s = 1
    while s < T:
        if s % 8:
            valid = (row < T - s) if rev else (row >= s)
            shift = T - s if rev else s
            a_p = jnp.where(valid, pltpu.roll(a, shift, 0), 1.0)
            b_p = jnp.where(valid, pltpu.roll(b, shift, 0), 0.0)
            b = a * b_p + b
            a = a * a_p
        elif rev:
            b = jnp.concatenate([a[:T - s] * b[s:] + b[:T - s], b[T - s:]], axis=0)
            a = jnp.concatenate([a[:T - s] * a[s:], a[T - s:]], axis=0)
        else:
            b = jnp.concatenate([b[:s], a[s:] * b[:T - s] + b[s:]], axis=0)
            a = jnp.concatenate([a[:s], a[s:] * a[:T - s]], axis=0)
        s *= 2
    return a, b


def _lru_kernel(x_ref, xp_ref, xn_ref, cw_ref, cb_ref, wg_ref, gb_ref, lam_ref, o_ref, carry_ref,
                *, rev, ncc, nlc):
    T = SEQ_CHUNK
    i = pl.program_id(1)
    c = _chunk_of_step(i, ncc, nlc, rev)

    @pl.when(i == 0)
    def _():
        carry_ref[...] = jnp.zeros_like(carry_ref)

    xc = _dwconv_chunk(x_ref, xp_ref, xn_ref, cw_ref, cb_ref, c, ncc, ncc + nlc)
    sp = _softplus(-lam_ref[...])
    for n in range(LRU_BLOCKS):
        sl = slice(n * LRU_BLOCK_DIM, (n + 1) * LRU_BLOCK_DIM)
        xb = xc[:, sl]
        gl = _dot(xb.astype(BF16), wg_ref[n])
        r_gate = _sigmoid(gl[:, :LRU_BLOCK_DIM] + gb_ref[0:1, sl])
        i_gate = _sigmoid(gl[:, LRU_BLOCK_DIM:] + gb_ref[1:2, sl])
        log_a = -LRU_C * r_gate * sp[:, sl]
        a = jnp.exp(log_a)
        bb = jnp.sqrt(1.0 - a * a) * (i_gate * xb)
        a_s, b_s = _linear_scan(a, bb, rev)
        h = b_s + a_s * carry_ref[0:1, sl]
        o_ref[:, sl] = h.astype(o_ref.dtype)
        carry_ref[0:1, sl] = h[0:1, :] if rev else h[T - 1:T, :]


def _lru_dir(P, conv_w, conv_b, wg, gb, lam, *, rev, B, ncc, nlc):
    M = P.shape[0]
    W = LRU_WIDTH
    blk, main, prev, nxt = _seq_specs(W, COL_LRU_X, B=B, ncc=ncc, nlc=nlc, rev=rev, n_rows=M)
    full2 = lambda shape: pl.BlockSpec(shape, lambda b, i: (0, 0))
    return pl.pallas_call(
        functools.partial(_lru_kernel, rev=rev, ncc=ncc, nlc=nlc),
        grid=(B, ncc + nlc),
        in_specs=[main, prev, nxt, full2((4, W)), full2((1, W)),
                  pl.BlockSpec((LRU_BLOCKS, LRU_BLOCK_DIM, 2 * LRU_BLOCK_DIM), lambda b, i: (0, 0, 0)),
                  full2((2, W)), full2((1, W))],
        out_specs=pl.BlockSpec((SEQ_CHUNK, W), lambda b, i: (blk(b, i), 0)),
        out_shape=jax.ShapeDtypeStruct((M, W), BF16),
        scratch_shapes=[pltpu.VMEM((8, W), F32)],
        compiler_params=_cparams("parallel", "arbitrary"),
        name="lru_bwd" if rev else "lru_fwd",
    )(P, P, P, conv_w, conv_b, wg, gb, lam)


def _ssd_kernel(x_ref, xp_ref, xn_ref, dt_ref, cw_ref, cb_ref, sp_ref, o_ref, st_ref,
                *, rev, d, ncc, nlc, add_skip):
    T = SEQ_CHUNK
    N = SSD_STATE
    HPG = SSD_HEADS // SSD_GROUPS
    GW = HPG * SSD_HEAD_DIM
    i = pl.program_id(1)
    c = _chunk_of_step(i, ncc, nlc, rev)

    @pl.when(i == 0)
    def _():
        st_ref[...] = jnp.zeros_like(st_ref)

    xc = _dwconv_chunk(x_ref, xp_ref, xn_ref, cw_ref, cb_ref, c, ncc, ncc + nlc)
    xbc = xc * _sigmoid(xc)
    xs = xbc[:, :SSD_WIDTH]
    Bm = xbc[:, SSD_WIDTH:SSD_WIDTH + SSD_GROUPS * N]
    Cm = xbc[:, SSD_WIDTH + SSD_GROUPS * N:]

    lane = lax.broadcasted_iota(jnp.int32, (1, LANES), 1)
    in_dir = (lane >= SSD_HEADS * d) & (lane < SSD_HEADS * (d + 1))
    dt = jnp.where(in_dir, _softplus(dt_ref[...] + sp_ref[0:1, :]), 0.0)
    a_neg = jnp.where(in_dir, -jnp.exp(sp_ref[1:2, :]), 0.0)
    a = dt * a_neg

    r_i = lax.broadcasted_iota(jnp.int32, (T, T), 0)
    c_i = lax.broadcasted_iota(jnp.int32, (T, T), 1)
    tri = (c_i >= r_i) if rev else (c_i <= r_i)
    tri_b = jnp.where(tri, 1.0, 0.0).astype(BF16)
    cs = sum(_dot(tri_b, p) for p in _split3(a))
    e_r = lax.broadcasted_iota(jnp.int32, (LANES, LANES), 0)
    e_c = lax.broadcasted_iota(jnp.int32, (LANES, LANES), 1)
    eye = jnp.where(e_r == e_c, 1.0, 0.0).astype(BF16)
    cs_row = sum(_dot_nt(eye, p) for p in _split3(cs))
    cs_end = cs[0:1, :] if rev else cs[T - 1:T, :]

    x_r = lax.broadcasted_iota(jnp.int32, (LANES, SSD_WIDTH), 0)
    x_c = lax.broadcasted_iota(jnp.int32, (LANES, SSD_WIDTH), 1)
    expand_m = jnp.where(x_r == SSD_HEADS * d + x_c // SSD_HEAD_DIM, 1.0, 0.0).astype(BF16)

    def expand(v):
        return sum(_dot(p, expand_m) for p in _split3(v))

    dt_e = expand(dt)
    dec_out_e = expand(jnp.exp(cs))
    dec_st_e = expand(jnp.exp(cs_end - cs))
    row8 = lax.broadcasted_iota(jnp.int32, (8, LANES), 0)
    small = jnp.where(row8 == 0, jnp.exp(cs_end), jnp.where(row8 == 1, sp_ref[2:3, :], 0.0))
    small_e = expand(small)
    dec_ch_e = small_e[0:1, :]
    dskip_e = small_e[1:2, :]

    xdt = xs * dt_e
    xdt_b = xdt.astype(BF16)
    z_st = (xdt * dec_st_e).astype(BF16)
    lo_half = lax.broadcasted_iota(jnp.int32, (1, LANES), 1) < SSD_HEAD_DIM

    for g in range(SSD_GROUPS):
        Bg = Bm[:, g * N:(g + 1) * N].astype(BF16)
        Cg = Cm[:, g * N:(g + 1) * N].astype(BF16)
        CB = _dot_nt(Cg, Bg)
        gs = slice(g * GW, (g + 1) * GW)
        Sg = st_ref[:, gs]
        y_off = _dot(Cg, Sg.astype(BF16)) * dec_out_e[:, gs]
        for p in range(HPG // 2):
            col = g * GW + p * LANES
            x_pair = xdt_b[:, col:col + LANES]
            y_pair = y_off[:, p * LANES:(p + 1) * LANES]
            for q in range(2):
                hh = SSD_HEADS * d + g * HPG + 2 * p + q
                L = jnp.exp(jnp.where(tri, cs[:, hh:hh + 1] - cs_row[hh:hh + 1, :], NEG_BIG))
                Wm = (CB * L).astype(BF16)
                xm = jnp.where(lo_half if q == 0 else jnp.logical_not(lo_half), x_pair,
                               jnp.zeros_like(x_pair))
                y_pair = y_pair + _dot(Wm, xm)
            if add_skip:
                y_pair = y_pair + dskip_e[:, col:col + LANES] * xs[:, col:col + LANES]
            o_ref[:, col:col + LANES] = y_pair.astype(o_ref.dtype)
        BgT = _dot_nt(eye, Bg).astype(BF16)
        st_ref[:, gs] = dec_ch_e[:, gs] * Sg + _dot(BgT, z_st[:, gs])


def _ssd_dir(P, dtp, conv_w, conv_b, sp, *, rev, d, B, ncc, nlc, add_skip):
    M = P.shape[0]
    XW = SSD_WIDTH + 2 * SSD_GROUPS * SSD_STATE
    blk, main, prev, nxt = _seq_specs(XW, COL_XBC, B=B, ncc=ncc, nlc=nlc, rev=rev, n_rows=M)
    full2 = lambda shape: pl.BlockSpec(shape, lambda b, i: (0, 0))
    return pl.pallas_call(
        functools.partial(_ssd_kernel, rev=rev, d=d, ncc=ncc, nlc=nlc, add_skip=add_skip),
        grid=(B, ncc + nlc),
        in_specs=[main, prev, nxt,
                  pl.BlockSpec((SEQ_CHUNK, LANES), lambda b, i: (blk(b, i), 0)),
                  full2((4, XW)), full2((1, XW)), full2((8, LANES))],
        out_specs=pl.BlockSpec((SEQ_CHUNK, SSD_WIDTH), lambda b, i: (blk(b, i), 0)),
        out_shape=jax.ShapeDtypeStruct((M, SSD_WIDTH), BF16),
        scratch_shapes=[pltpu.VMEM((SSD_STATE, SSD_WIDTH), F32)],
        compiler_params=_cparams("parallel", "arbitrary"),
        name="ssd_bwd" if rev else "ssd_fwd",
    )(P, P, P, dtp, conv_w, conv_b, sp)


def _rope_kernel(q_ref, k_ref, c_ref, sa_ref, sb_ref, qo_ref, ko_ref, *, scale):
    W = q_ref.shape[1]
    reps = W // LANES
    cos = jnp.concatenate([c_ref[...]] * reps, axis=1)
    sa = jnp.concatenate([sa_ref[...]] * reps, axis=1)
    sb = jnp.concatenate([sb_ref[...]] * reps, axis=1)
    quarter = ATT_QK_DIM // 4

    def rot(x):
        return x * cos + pltpu.roll(x, W - quarter, 1) * sa + pltpu.roll(x, quarter, 1) * sb

    qo_ref[...] = (rot(q_ref[...].astype(F32)) * scale).astype(qo_ref.dtype)
    ko_ref[...] = rot(k_ref[...].astype(F32)).astype(ko_ref.dtype)


def _rope(P, cos, sa, sb, *, tm):
    M = P.shape[0]
    W = ATT_HEADS * 2 * ATT_QK_DIM
    tab = pl.BlockSpec((tm, LANES), lambda i: (i, 0))
    return pl.pallas_call(
        functools.partial(_rope_kernel, scale=ATT_QK_DIM ** -0.5),
        grid=(M // tm,),
        in_specs=[pl.BlockSpec((tm, W), lambda i: (i, COL_Q)),
                  pl.BlockSpec((tm, W), lambda i: (i, COL_K)), tab, tab, tab],
        out_specs=[pl.BlockSpec((tm, W), lambda i: (i, 0))] * 2,
        out_shape=[jax.ShapeDtypeStruct((M, W), BF16)] * 2,
        compiler_params=_cparams("parallel"),
        name="rope",
    )(P, P, cos, sa, sb)


def _attn_kernel(*refs, tq, tk, n_ctx, n_lat, lam_init):
    if n_lat:
        (q_ref, kc_ref, vc_ref, kl_ref, vl_ref, lv_ref, nw_ref, o_ref, m_ref, l_ref, acc_ref) = refs
    else:
        (q_ref, kc_ref, vc_ref, lv_ref, nw_ref, o_ref, m_ref, l_ref, acc_ref) = refs
        kl_ref = vl_ref = None

    q = q_ref[...]
    lane = lax.broadcasted_iota(jnp.int32, (1, LANES), 1)
    zero = jnp.zeros_like(q)
    qq = jnp.concatenate([jnp.where(lane < ATT_QK_DIM, q, zero),
                          jnp.where(lane >= ATT_QK_DIM, q, zero)], axis=0)

    m_ref[...] = jnp.full(m_ref.shape, NEG_BIG, F32)
    l_ref[...] = jnp.zeros_like(l_ref)
    acc_ref[...] = jnp.zeros_like(acc_ref)

    def lane_blocks(s):
        return [s[:, j * LANES:(j + 1) * LANES] for j in range(s.shape[1] // LANES)]

    def over_keys(fn):
        tc = min(n_ctx, tk)
        for j in range(n_ctx // tc):
            fn(kc_ref[j * tc:(j + 1) * tc, :], vc_ref[j * tc:(j + 1) * tc, :])
        if n_lat:
            def body(j, carry):
                rows = pl.ds(pl.multiple_of(j * tk, tk), tk)
                fn(kl_ref[rows, :], vl_ref[rows, :])
                return carry
            lax.fori_loop(0, n_lat // tk, body, 0, unroll=2)

    def step(kc, vc):
        blocks = lane_blocks(_dot_nt(qq, kc))
        m_prev = m_ref[...]
        smax = functools.reduce(jnp.maximum, blocks)
        m_new = jnp.maximum(m_prev, jnp.max(smax, axis=1, keepdims=True))
        alpha = jnp.exp(m_prev - m_new)
        p = [jnp.exp(sb - m_new) for sb in blocks]
        l_ref[...] = alpha * l_ref[...] + functools.reduce(jnp.add, p)
        pb = jnp.concatenate([x.astype(BF16) for x in p], axis=1)
        acc_ref[...] = alpha * acc_ref[...] + _dot(pb, vc)
        m_ref[...] = m_new

    over_keys(step)
    l_ref[...] = jnp.broadcast_to(jnp.sum(l_ref[...], axis=1, keepdims=True), l_ref.shape)

    lv = lv_ref[...]
    s01 = jnp.sum(lv[0:1, :] * lv[1:2, :], axis=1, keepdims=True)
    s23 = jnp.sum(lv[2:3, :] * lv[3:4, :], axis=1, keepdims=True)
    lam = jnp.exp(s01) - jnp.exp(s23) + lam_init
    acc = acc_ref[...]
    l = l_ref[...]
    o = acc[:tq, :] / l[:tq, :] - lam * (acc[tq:, :] / l[tq:, :])
    o = o * lax.rsqrt(jnp.mean(o * o, axis=-1, keepdims=True) + LN_EPS) * nw_ref[...]
    o_ref[...] = (o * (1.0 - lam_init)).astype(o_ref.dtype)


def _attention(qr, kr, P, lam_vecs, norm_w, out_prev, *, B, S, CTX, lam_init, ctx_queries):
    M = qr.shape[0]
    H = ATT_HEADS
    tk = min(S, 1024)
    assert S % tk == 0 and (CTX % tk == 0 or tk % CTX == 0)
    vcol = COL_V * (1024 // LANES)
    n_lat_blk = (B * S) // CTX
    if ctx_queries:
        tq = min(CTX, 256)
        nq = CTX // tq
        q_spec = pl.BlockSpec((tq, LANES), lambda b, h, i: ((B * S + b * CTX) // tq + i, h))
    else:
        tq = 512
        nq = S // tq
        q_spec = pl.BlockSpec((tq, LANES), lambda b, h, i: (b * nq + i, h))
    in_specs = [q_spec,
                pl.BlockSpec((CTX, LANES), lambda b, h, i: (n_lat_blk + b, h)),
                pl.BlockSpec((CTX, LANES), lambda b, h, i: (n_lat_blk + b, vcol + h))]
    args = [qr, kr, P]
    if not ctx_queries:
        in_specs += [pl.BlockSpec((S, LANES), lambda b, h, i: (b, h)),
                     pl.BlockSpec((S, LANES), lambda b, h, i: (b, vcol + h))]
        args += [kr, P]
    in_specs += [pl.BlockSpec((4, ATT_QK_DIM), lambda b, h, i: (0, 0)),
                 pl.BlockSpec((1, LANES), lambda b, h, i: (0, 0))]
    args += [lam_vecs, norm_w]
    kwargs = {}
    if out_prev is not None:
        in_specs.append(pl.BlockSpec(memory_space=pl.ANY))
        args.append(out_prev)
        kwargs["input_output_aliases"] = {len(args) - 1: 0}

    kernel = functools.partial(_attn_kernel, tq=tq, tk=tk, n_ctx=CTX, n_lat=0 if ctx_queries else S,
                               lam_init=lam_init)
    if out_prev is not None:
        base = kernel
        kernel = lambda *refs: base(*refs[:len(args) - 1], *refs[len(args):])
    return pl.pallas_call(
        kernel,
        grid=(B, H, nq),
        in_specs=in_specs,
        out_specs=q_spec,
        out_shape=jax.ShapeDtypeStruct((M, H * ATT_V_DIM), BF16),
        scratch_shapes=[pltpu.VMEM((2 * tq, LANES), F32)] * 3,
        compiler_params=_cparams("parallel", "parallel", "arbitrary"),
        name="attn_ctx" if ctx_queries else "attn_lat",
        **kwargs,
    )(*args)


def _merge_kernel(hf_ref, hb_ref, gl_ref, yf_ref, yb_ref, z_ref, att_ref, g0_ref, g1_ref, g2_ref,
                  wb_ref, nw_ref, o_ref, br_ref):
    @pl.when(pl.program_id(1) == 0)
    def _():
        y = hf_ref[...].astype(F32) + hb_ref[...].astype(F32)
        g = gl_ref[...].astype(F32)
        gelu = 0.5 * g * (1.0 + jnp.tanh(0.7978845608028654 * (g + 0.044715 * (g * g * g))))
        br_ref[0] = (y * gelu).astype(BF16)
        ys = yf_ref[...].astype(F32) + yb_ref[...].astype(F32)
        zz = z_ref[...].astype(F32)
        t = ys * (zz * _sigmoid(zz))
        t = t * lax.rsqrt(jnp.mean(t * t, axis=-1, keepdims=True) + LN_EPS) * nw_ref[...]
        br_ref[1] = t.astype(BF16)
        br_ref[2] = att_ref[...]

    acc = _sigmoid(g0_ref[...].astype(F32)) * _dot(br_ref[0], wb_ref[0])
    acc = acc + _sigmoid(g1_ref[...].astype(F32)) * _dot(br_ref[1], wb_ref[1])
    acc = acc + _sigmoid(g2_ref[...].astype(F32)) * _dot(br_ref[2], wb_ref[2])
    o_ref[...] = acc.astype(o_ref.dtype)


def _merge(hf, hb, yf, yb, att, P, wb, ssd_norm_w, *, n_rows, tm, tn):
    M = P.shape[0]
    D = wb.shape[2]
    BW = wb.shape[1]
    row = lambda col: pl.BlockSpec((tm, BW), lambda i, j: (i, col))
    gcol = COL_GATES * 1024 // tn
    gate = lambda k: pl.BlockSpec((tm, tn), lambda i, j: (i, gcol + k * (D // tn) + j))
    return pl.pallas_call(
        _merge_kernel,
        grid=(n_rows // tm, D // tn),
        in_specs=[row(0), row(0), row(COL_LRU_G), row(0), row(0), row(COL_Z), row(0),
                  gate(0), gate(1), gate(2),
                  pl.BlockSpec((3, BW, tn), lambda i, j: (0, 0, j)),
                  pl.BlockSpec((1, BW), lambda i, j: (0, 0))],
        out_specs=pl.BlockSpec((tm, tn), lambda i, j: (i, j)),
        out_shape=jax.ShapeDtypeStruct((M, D), BF16),
        scratch_shapes=[pltpu.VMEM((3, tm, BW), BF16)],
        compiler_params=_cparams("parallel", "arbitrary"),
        name="merge",
    )(hf, hb, P, yf, yb, P, att, P, P, P, wb, ssd_norm_w)


def _route(logits, rbias):
    G, E = N_EXPERT_GROUPS, EXPERTS_PER_GROUP
    mx = jnp.max(logits, axis=0, keepdims=True)
    ex = jnp.exp(logits - mx)
    sc = ex / jnp.sum(ex, axis=0, keepdims=True)
    sel = sc + rbias
    srow = [sel[r:r + 1, :] for r in range(G * E)]
    prow = [sc[r:r + 1, :] for r in range(G * E)]
    gscore = []
    for g in range(G):
        v = srow[g * E:(g + 1) * E]
        best = v[0] + v[1]
        for a in range(E):
            for b in range(a + 1, E):
                if (a, b) != (0, 1):
                    best = jnp.maximum(best, v[a] + v[b])
        gscore.append(best)
    gmax = functools.reduce(jnp.maximum, gscore)
    gidx = jnp.full(gmax.shape, G - 1, jnp.int32)
    for g in range(G - 2, -1, -1):
        gidx = jnp.where(gscore[g] == gmax, g, gidx)

    def pick(rows, k):
        out = rows[(G - 1) * E + k]
        for g in range(G - 2, -1, -1):
            out = jnp.where(gidx == g, rows[g * E + k], out)
        return out

    v = [pick(srow, k) for k in range(E)]
    p = [pick(prow, k) for k in range(E)]
    chosen = []
    for k in range(E):
        rank = jnp.zeros(gmax.shape, jnp.int32)
        for j in range(E):
            if j == k:
                continue
            ahead = (v[j] > v[k]) | ((v[j] == v[k]) & (j < k)) if j < k else (v[j] > v[k])
            rank = rank + jnp.where(ahead, 1, 0)
        chosen.append(rank < 2)
    first = jnp.full(gmax.shape, E, jnp.int32)
    second = jnp.full(gmax.shape, -1, jnp.int32)
    for k in range(E):
        first = jnp.minimum(first, jnp.where(chosen[k], k, E))
        second = jnp.maximum(second, jnp.where(chosen[k], k, -1))
    p_first = jnp.zeros(gmax.shape, F32)
    p_second = jnp.zeros(gmax.shape, F32)
    for k in range(E):
        p_first = p_first + jnp.where(first == k, p[k], 0.0)
        p_second = p_second + jnp.where(second == k, p[k], 0.0)
    tot = p_first + p_second
    return gidx * E + first, gidx * E + second, p_first / tot, p_second / tot


def _outproj_kernel(m_ref, w_ref, x_ref, mod_ref, lng_ref, lnb_ref, wr_ref, rb_ref,
                    x1_ref, h2_ref, ids_ref, wts_ref, cnt_ref, *, tm, alpha):
    o = _dot(m_ref[...], w_ref[...])
    g1 = mod_ref[0, 2:3, :]
    sh2 = mod_ref[0, 3:4, :]
    sc2 = mod_ref[0, 4:5, :]
    x1 = _ln(alpha * x_ref[...] + g1 * o) * lng_ref[...] + lnb_ref[...]
    x1_ref[...] = x1
    h2 = _ln(x1) * (1.0 + sc2) + sh2
    _store_slabs(h2_ref, h2, tm)
    h_hi, h_lo = _split2(h2)
    w_hi, w_lo = _split2(wr_ref[...])
    logits = _dot_nt(w_hi, h_hi) + _dot_nt(w_lo, h_hi) + _dot_nt(w_hi, h_lo)
    e0, e1, w0, w1 = _route(logits, rb_ref[...])
    erow = lax.broadcasted_iota(jnp.int32, (N_EXPERTS, tm), 0)
    hit0 = erow == e0
    hit1 = erow == e1
    cnt = jnp.where(hit0, 1.0, 0.0) + jnp.where(hit1, 1.0, 0.0)
    t_r = lax.broadcasted_iota(jnp.int32, (tm, tm), 0)
    t_c = lax.broadcasted_iota(jnp.int32, (tm, tm), 1)
    before = jnp.where(t_r < t_c, 1.0, 0.0).astype(BF16)
    pre = _dot(cnt.astype(BF16), before)
    r0 = jnp.sum(jnp.where(hit0, pre, 0.0), axis=0, keepdims=True).astype(jnp.int32)
    r1 = jnp.sum(jnp.where(hit1, pre, 0.0), axis=0, keepdims=True).astype(jnp.int32)
    cnt_ref[...] = jnp.broadcast_to(jnp.sum(cnt, axis=1, keepdims=True), cnt_ref.shape)
    row = lax.broadcasted_iota(jnp.int32, (8, tm), 0)
    ids_ref[...] = jnp.where(row == 0, e0, jnp.where(row == 1, e1,
                             jnp.where(row == 2, r0, jnp.where(row == 3, r1, 0))))
    wts_ref[...] = jnp.where(row == 0, w0, jnp.where(row == 1, w1, 0.0))


def _outproj(m, w_out, x, mod, ln_g, ln_b, wr_t, rbias, *, n_rows, tm, n_lat, S, alpha):
    M, D = x.shape

    def midx(i):
        return jnp.where(i * tm < n_lat, 1 + (i * tm) // S, 0)

    vec = pl.BlockSpec((1, D), lambda i: (0, 0))
    return pl.pallas_call(
        functools.partial(_outproj_kernel, tm=tm, alpha=alpha),
        grid=(n_rows // tm,),
        in_specs=[pl.BlockSpec((tm, D), lambda i: (i, 0)),
                  pl.BlockSpec((D, D), lambda i: (0, 0)),
                  pl.BlockSpec((tm, D), lambda i: (i, 0)),
                  pl.BlockSpec((1, 6, D), lambda i: (midx(i), 0, 0)),
                  vec, vec,
                  pl.BlockSpec((N_EXPERTS, D), lambda i: (0, 0)),
                  pl.BlockSpec((N_EXPERTS, 1), lambda i: (0, 0))],
        out_specs=[pl.BlockSpec((tm, D), lambda i: (i, 0)),
                   pl.BlockSpec((tm * SLAB, LANES), lambda i: (i, 0)),
                   pl.BlockSpec((8, tm), lambda i: (0, i)),
                   pl.BlockSpec((8, tm), lambda i: (0, i)),
                   pl.BlockSpec((N_EXPERTS, LANES), lambda i: (0, i))],
        out_shape=[jax.ShapeDtypeStruct((M, D), F32),
                   jax.ShapeDtypeStruct((n_rows * SLAB, LANES), jnp.uint32),
                   jax.ShapeDtypeStruct((8, n_rows), jnp.int32),
                   jax.ShapeDtypeStruct((8, n_rows), F32),
                   jax.ShapeDtypeStruct((N_EXPERTS, (n_rows // tm) * LANES), F32)],
        compiler_params=_cparams("parallel"),
        name="outproj",
    )(m, w_out, x, mod, ln_g, ln_b, wr_t, rbias)


def _ffn_kernel(te_ref, tv_ref, src_ref, dst_ref, h_hbm, wg_ref, wu_ref, wd_ref, y_hbm,
                xbuf, ybuf, xb_ref, gsem, ssem, *, tm):
    i = pl.program_id(0)
    n = pl.num_programs(0)
    slot = i % 2

    def gather_start(tile, s):
        def issue(r2, carry):
            for prio in range(2):
                r = 2 * r2 + prio
                tok = src_ref[tile * tm + r]
                pltpu.make_async_copy(h_hbm.at[pl.ds(pl.multiple_of(tok * SLAB, SLAB), SLAB)],
                                      xbuf.at[s, pl.ds(pl.multiple_of(r * SLAB, SLAB), SLAB)],
                                      gsem.at[s]).start(priority=prio)
            return carry
        lax.fori_loop(0, tm // 2, issue, 0)

    def gather_wait(s):
        pltpu.make_async_copy(h_hbm.at[pl.ds(0, tm * SLAB)], xbuf.at[s], gsem.at[s]).wait()

    def scatter_start(tile):
        def issue(r2, carry):
            for prio in range(2):
                r = 2 * r2 + prio
                tok = dst_ref[tile * tm + r]
                pltpu.make_async_copy(ybuf.at[pl.ds(pl.multiple_of(r * SLAB, SLAB), SLAB)],
                                      y_hbm.at[pl.ds(pl.multiple_of(tok * SLAB, SLAB), SLAB)],
                                      ssem).start(priority=prio)
            return carry
        lax.fori_loop(0, tm // 2, issue, 0)

    def scatter_wait():
        pltpu.make_async_copy(ybuf, y_hbm.at[pl.ds(0, tm * SLAB)], ssem).wait()

    @pl.when(i == 0)
    def _():
        gather_start(0, 0)

    @pl.when(tv_ref[i] != 0)
    def _():
        has_next = jnp.logical_and(i + 1 < n, tv_ref[jnp.minimum(i + 1, n - 1)] != 0)

        @pl.when(has_next)
        def _():
            gather_start(i + 1, 1 - slot)

        gather_wait(slot)
        half = SLAB * LANES
        for k in range(SLAB):
            lo, hi = _load_slab_piece(xbuf[slot, pl.ds(k, tm, stride=SLAB), :])
            xb_ref[:, k * LANES:(k + 1) * LANES] = lo.astype(BF16)
            xb_ref[:, half + k * LANES:half + (k + 1) * LANES] = hi.astype(BF16)
        x = xb_ref[...]
        g = _dot(x, wg_ref[0])
        u = _dot(x, wu_ref[0])
        act = (g * _sigmoid(g) * u).astype(BF16)
        y = _dot(act, wd_ref[0])

        @pl.when(i > 0)
        def _():
            scatter_wait()

        _store_slabs(ybuf, y, tm)
        scatter_start(i)

        @pl.when(jnp.logical_not(has_next))
        def _():
            scatter_wait()


def _expert_ffn(h2, src_tok, dst_tok, tile_e, tile_v, wg, wu, wd, *, tm, n_out):
    n_tiles = tile_e.shape[0]
    E, D, F = wg.shape
    return pl.pallas_call(
        functools.partial(_ffn_kernel, tm=tm),
        grid_spec=pltpu.PrefetchScalarGridSpec(
            num_scalar_prefetch=4,
            grid=(n_tiles,),
            in_specs=[pl.BlockSpec(memory_space=pl.ANY),
                      pl.BlockSpec((1, D, F), lambda i, te, tv, sr, ds: (te[i], 0, 0)),
                      pl.BlockSpec((1, D, F), lambda i, te, tv, sr, ds: (te[i], 0, 0)),
                      pl.BlockSpec((1, F, D), lambda i, te, tv, sr, ds: (te[i], 0, 0))],
            out_specs=pl.BlockSpec(memory_space=pl.ANY),
            scratch_shapes=[pltpu.VMEM((2, tm * SLAB, LANES), jnp.uint32),
                            pltpu.VMEM((tm * SLAB, LANES), jnp.uint32),
                            pltpu.VMEM((tm, D), BF16),
                            pltpu.SemaphoreType.DMA((2,)),
                            pltpu.SemaphoreType.DMA(())]),
        out_shape=jax.ShapeDtypeStruct((n_out * SLAB, LANES), jnp.uint32),
        compiler_params=_cparams("arbitrary"),
        name="expert_ffn",
    )(tile_e, tile_v, src_tok, dst_tok, h2, wg, wu, wd)


def _final_kernel(x_ref, y0_ref, y1_ref, w_ref, mod_ref, lng_ref, lnb_ref, o_ref, f_ref, *, tm, alpha):
    half = SLAB * LANES
    w0 = w_ref[:, 0:1]
    w1 = w_ref[:, 1:2]
    for k in range(SLAB):
        rows = pl.ds(k, tm, stride=SLAB)
        lo0, hi0 = _load_slab_piece(y0_ref[rows, :])
        lo1, hi1 = _load_slab_piece(y1_ref[rows, :])
        f_ref[:, k * LANES:(k + 1) * LANES] = w0 * lo0 + w1 * lo1
        f_ref[:, half + k * LANES:half + (k + 1) * LANES] = w0 * hi0 + w1 * hi1
    g2 = mod_ref[0, 5:6, :]
    o_ref[...] = _ln(alpha * x_ref[...] + g2 * f_ref[...]) * lng_ref[...] + lnb_ref[...]


def _final(x1, yg, wcol, mod, ln_g, ln_b, *, n_rows, tm, n_lat, S, alpha):
    D = x1.shape[1]

    def midx(i):
        return jnp.where(i * tm < n_lat, 1 + (i * tm) // S, 0)

    nt = n_rows // tm
    vec = pl.BlockSpec((1, D), lambda i: (0, 0))
    return pl.pallas_call(
        functools.partial(_final_kernel, tm=tm, alpha=alpha),
        grid=(nt,),
        in_specs=[pl.BlockSpec((tm, D), lambda i: (i, 0)),
                  pl.BlockSpec((tm * SLAB, LANES), lambda i: (i, 0)),
                  pl.BlockSpec((tm * SLAB, LANES), lambda i: (nt + i, 0)),
                  pl.BlockSpec((tm, 2), lambda i: (i, 0)),
                  pl.BlockSpec((1, 6, D), lambda i: (midx(i), 0, 0)),
                  vec, vec],
        out_specs=pl.BlockSpec((tm, D), lambda i: (i, 0)),
        out_shape=jax.ShapeDtypeStruct((n_rows, D), F32),
        scratch_shapes=[pltpu.VMEM((tm, D), F32)],
        compiler_params=_cparams("parallel"),
        name="final",
    )(x1, yg, yg, wcol, mod, ln_g, ln_b)


def _rope_tables(B, S, CTX):
    t = jnp.arange(S)
    row = (t // GRID_W).astype(F32)
    col = (t % GRID_W).astype(F32)
    axis_dim = ATT_QK_DIM // 2
    freqs = ROPE_THETA ** (-jnp.arange(0, axis_dim, 2, dtype=F32) / axis_dim)
    ang = jnp.stack([row[:, None] * freqs, col[:, None] * freqs], axis=1)
    cos, sin = jnp.cos(ang), jnp.sin(ang)
    zero = jnp.zeros_like(sin)
    c64 = jnp.concatenate([cos, cos], axis=-1).reshape(S, ATT_QK_DIM)
    sa64 = jnp.concatenate([-sin, zero], axis=-1).reshape(S, ATT_QK_DIM)
    sb64 = jnp.concatenate([zero, sin], axis=-1).reshape(S, ATT_QK_DIM)

    def full(tab, ctx_val):
        lat = jnp.tile(jnp.concatenate([tab, tab], axis=-1), (B, 1))
        ctx = jnp.full((B * CTX, LANES), ctx_val, F32)
        return jnp.concatenate([lat, ctx], axis=0)

    return full(c64, 1.0), full(sa64, 0.0), full(sb64, 0.0)


def _moe_plan(ids, tile_counts, tm, tm_route):
    n_tok = ids.shape[1]
    experts = jnp.arange(N_EXPERTS, dtype=jnp.int32)
    counts = jnp.sum(tile_counts, axis=0)
    padded = ((counts + tm - 1) // tm) * tm
    ends = jnp.cumsum(padded)
    starts = ends - padded
    tile_base = starts[None, :] + jnp.cumsum(tile_counts, axis=0) - tile_counts
    base_tok = jnp.repeat(tile_base, tm_route, axis=0)
    pos = jnp.stack([jnp.sum(jnp.where(experts[None, :] == ids[k][:, None], base_tok, 0), axis=1)
                     + ids[2 + k] for k in range(2)]).reshape(-1).astype(jnp.int32)
    n_slots = 2 * n_tok + N_EXPERTS * tm
    slot_of = jnp.full((n_slots,), -1, jnp.int32).at[pos].set(jnp.arange(2 * n_tok, dtype=jnp.int32))
    real = slot_of >= 0
    src_tok = jnp.where(real, slot_of % n_tok, 0)
    dst_tok = jnp.where(real, slot_of, 2 * n_tok + jnp.arange(n_slots, dtype=jnp.int32) % tm)
    tile_start = jnp.arange(n_slots // tm, dtype=jnp.int32) * tm
    tile_e = jnp.minimum(jnp.sum((ends[None, :] <= tile_start[:, None]).astype(jnp.int32), axis=1),
                         N_EXPERTS - 1).astype(jnp.int32)
    tile_v = (tile_start < ends[-1]).astype(jnp.int32)
    return src_tok, dst_tok, tile_e, tile_v


def _row_tile(limit, *dims):
    t = limit
    while any(d % t for d in dims):
        t //= 2
    return t


def kernel(x, c, ctx, c_ctx, w_ada, b_ada, w_in, lru_conv_w, lru_conv_b, lru_gate_w, lru_gate_b, lru_lambda, ssd_conv_w, ssd_conv_b, ssd_a_log, ssd_dt_bias, ssd_d, ssd_norm_w, att_lambda, att_norm_w, w_branch, w_out, ln_g, ln_b, w_router, router_bias, w_gate_e, w_up_e, w_down_e):
    B, S, D = x.shape
    CTX = ctx.shape[1]
    depth = w_ada.shape[0]
    alpha = (2 * depth) ** 0.25
    n_lat = B * S
    n_ctx = B * CTX
    M = n_lat + n_ctx
    assert D == 2 * SLAB * LANES and CTX % SEQ_CHUNK == 0 and S % SEQ_CHUNK == 0 and S % CTX == 0
    ncc, nlc = CTX // SEQ_CHUNK, S // SEQ_CHUNK

    xs = jnp.concatenate([x.reshape(n_lat, D), ctx.reshape(n_ctx, D)], axis=0)
    cvec = jnp.concatenate([c_ctx[None, :], c, jnp.zeros((8 - 1 - B, D), F32)], axis=0)
    cos_t, sa_t, sb_t = _rope_tables(B, S, CTX)
    wr_t = w_router.T
    rbias = router_bias.reshape(N_EXPERTS, 1)

    tm_big = _row_tile(1024, S, n_ctx)
    tm_mid = _row_tile(512, S, n_ctx)
    tm_small = _row_tile(256, S, n_ctx)

    i0, i1, i2, i3, i4, i5, i6, i7, i8 = [int(v) for v in
                                          (0, 1024, 2048, 3072, 5120, 5152, 6176, 7200, 8224)]
    for l in range(depth):
        need_ctx = l < depth - 1
        lam_init = 0.8 - 0.6 * math.exp(-0.3 * l)
        n_rows = M if need_ctx else n_lat

        wl = w_in[l]
        w_main = jnp.concatenate([wl[:, i3:i4], wl[:, i0:i3], wl[:, i5:]], axis=1).astype(BF16)
        w_dt = jnp.pad(wl[:, i4:i5], ((0, 0), (0, LANES - (i5 - i4)))).astype(BF16)
        gw = lru_gate_w[l]
        wg_lru = jnp.concatenate([gw[:, 0], gw[:, 1]], axis=-1).astype(BF16)

        mod = _ada(cvec, w_ada, b_ada[l], l).reshape(8, 6, D)
        P, dtp = _inproj(xs, mod, w_main, w_dt, tm=tm_big, tn=1024, n_lat=n_lat, S=S)

        h_dirs, y_dirs = [], []
        for d in range(2):
            rev = d == 1
            h_dirs.append(_lru_dir(P, lru_conv_w[l], lru_conv_b[l].reshape(1, -1), wg_lru[d],
                                   lru_gate_b[l, d], lru_lambda[l, d].reshape(1, -1),
                                   rev=rev, B=B, ncc=ncc, nlc=nlc))
            pad = ((0, 0), (SSD_HEADS * d, LANES - SSD_HEADS * (d + 1)))
            sp = jnp.concatenate([jnp.pad(ssd_dt_bias[l, d][None, :], pad),
                                  jnp.pad(ssd_a_log[l, d][None, :], pad),
                                  jnp.pad(ssd_d[l][None, :], pad),
                                  jnp.zeros((5, LANES), F32)], axis=0)
            y_dirs.append(_ssd_dir(P, dtp, ssd_conv_w[l], ssd_conv_b[l].reshape(1, -1), sp,
                                   rev=rev, d=d, B=B, ncc=ncc, nlc=nlc, add_skip=(d == 0)))

        qr, kr = _rope(P, cos_t, sa_t, sb_t, tm=tm_mid)
        nw_att = att_norm_w[l].reshape(1, LANES)
        att = _attention(qr, kr, P, att_lambda[l], nw_att, None, B=B, S=S, CTX=CTX,
                         lam_init=lam_init, ctx_queries=False)
        if need_ctx:
            att = _attention(qr, kr, P, att_lambda[l], nw_att, att, B=B, S=S, CTX=CTX,
                             lam_init=lam_init, ctx_queries=True)

        m = _merge(h_dirs[0], h_dirs[1], y_dirs[0], y_dirs[1], att, P, w_branch[l].astype(BF16),
                   ssd_norm_w[l].reshape(1, -1), n_rows=n_rows, tm=tm_mid, tn=512)
        x1, h2, ids, wts, cnt = _outproj(m, w_out[l].astype(BF16), xs, mod, ln_g[l, 0].reshape(1, D),
                                         ln_b[l, 0].reshape(1, D), wr_t, rbias, n_rows=n_rows,
                                         tm=tm_small, n_lat=n_lat, S=S, alpha=alpha)

        tile_counts = cnt[:, ::LANES].T.astype(jnp.int32)
        src_tok, dst_tok, tile_e, tile_v = _moe_plan(ids[:4], tile_counts, MOE_TILE, tm_small)
        yg = _expert_ffn(h2, src_tok, dst_tok, tile_e, tile_v, _cast_bf16(w_gate_e, l),
                         _cast_bf16(w_up_e, l), _cast_bf16(w_down_e, l), tm=MOE_TILE,
                         n_out=2 * n_rows + MOE_TILE)
        xs = _final(x1, yg, wts[:2].T, mod, ln_g[l, 1].reshape(1, D), ln_b[l, 1].reshape(1, D),
                    n_rows=n_rows, tm=tm_small, n_lat=n_lat, S=S, alpha=alpha)

    return xs[:n_lat].reshape(B, S, D)
```

```python
import functools
import math

import jax
import jax.numpy as jnp
from jax import lax
from jax.experimental import pallas as pl
from jax.experimental.pallas import tpu as pltpu

F32 = jnp.float32
BF16 = jnp.bfloat16

GRID_W = 64
ATT_HEADS = 8
ATT_QK_DIM = 64
ATT_V_DIM = 128
ROPE_THETA = 10000.0
SSD_HEADS = 16
SSD_HEAD_DIM = 64
SSD_GROUPS = 4
SSD_STATE = 128
SSD_WIDTH = SSD_HEADS * SSD_HEAD_DIM
LRU_WIDTH = 1024
LRU_BLOCKS = 8
LRU_BLOCK_DIM = LRU_WIDTH // LRU_BLOCKS
LRU_C = 8.0
N_EXPERTS = 16
N_EXPERT_GROUPS = 4
EXPERTS_PER_GROUP = 4
LN_EPS = 1e-6

LANES = 128
SLAB = 8
SEQ_CHUNK = 256
HALO = 16
MOE_TILE = 256
NEG_BIG = -1e30
VMEM_LIMIT = 56 * 2 ** 20

COL_XBC, COL_LRU_X, COL_LRU_G, COL_Z, COL_Q, COL_K, COL_V, COL_GATES = 0, 2, 3, 4, 5, 6, 7, 8


def _cparams(*sem):
    return pltpu.CompilerParams(dimension_semantics=sem, vmem_limit_bytes=VMEM_LIMIT)


def _dot(a, b):
    return jnp.dot(a, b, preferred_element_type=F32)


def _dot_nt(a, b):
    return lax.dot_general(a, b, (((1,), (1,)), ((), ())), preferred_element_type=F32)


def _split3(v):
    hi = v.astype(BF16)
    r = v - hi.astype(F32)
    mid = r.astype(BF16)
    lo = (r - mid.astype(F32)).astype(BF16)
    return hi, mid, lo


def _split2(v):
    hi = v.astype(BF16)
    lo = (v - hi.astype(F32)).astype(BF16)
    return hi, lo


def _store_slabs(ref, x, tm):
    half = SLAB * LANES
    for k in range(SLAB):
        lo = lax.bitcast_convert_type(x[:, k * LANES:(k + 1) * LANES].astype(BF16).astype(F32), jnp.uint32)
        hi = lax.bitcast_convert_type(
            x[:, half + k * LANES:half + (k + 1) * LANES].astype(BF16).astype(F32), jnp.uint32)
        ref[pl.ds(k, tm, stride=SLAB), :] = hi | (lo >> 16)


def _load_slab_piece(u):
    lo = lax.bitcast_convert_type(u << 16, F32)
    hi = lax.bitcast_convert_type(u & jnp.uint32(0xFFFF0000), F32)
    return lo, hi


def _ln(x):
    mu = jnp.mean(x, axis=-1, keepdims=True)
    xc = x - mu
    var = jnp.mean(xc * xc, axis=-1, keepdims=True)
    return xc * lax.rsqrt(var + LN_EPS)


def _sigmoid(x):
    return jax.nn.sigmoid(x)


def _softplus(x):
    return jnp.maximum(x, 0.0) + jnp.log1p(jnp.exp(-jnp.abs(x)))


def _cast_kernel(w_ref, o_ref):
    o_ref[...] = w_ref[0].astype(o_ref.dtype)


def _cast_bf16(w, l):
    _, E, R, C = w.shape
    return pl.pallas_call(
        _cast_kernel,
        grid=(E,),
        in_specs=[pl.BlockSpec((1, 1, R, C), lambda e: (l, e, 0, 0))],
        out_specs=pl.BlockSpec((1, R, C), lambda e: (e, 0, 0)),
        out_shape=jax.ShapeDtypeStruct((E, R, C), BF16),
        compiler_params=_cparams("parallel"),
        name="cast_bf16",
    )(w)


def _ada_kernel(c_ref, w_ref, b_ref, o_ref):
    c = c_ref[...]
    a = c * _sigmoid(c)
    a_hi, a_lo = _split2(a)
    w_hi, w_lo = _split2(w_ref[0])
    o_ref[...] = _dot(a_hi, w_hi) + _dot(a_lo, w_hi) + _dot(a_hi, w_lo) + b_ref[...]


def _ada(cvec, w, b, l):
    R, D = cvec.shape
    N = w.shape[2]
    tn = 1024
    return pl.pallas_call(
        _ada_kernel,
        grid=(N // tn,),
        in_specs=[pl.BlockSpec((R, D), lambda j: (0, 0)),
                  pl.BlockSpec((1, D, tn), lambda j: (l, 0, j)),
                  pl.BlockSpec((1, tn), lambda j: (0, j))],
        out_specs=pl.BlockSpec((R, tn), lambda j: (0, j)),
        out_shape=jax.ShapeDtypeStruct((R, N), F32),
        compiler_params=_cparams("arbitrary"),
        name="ada",
    )(cvec, w, b.reshape(1, N))


def _inproj_kernel(x_ref, mod_ref, w_ref, wdt_ref, o_ref, dt_ref, h_ref, *, tm, rb):
    @pl.when(pl.program_id(1) == 0)
    def _():
        sh = mod_ref[0, 0:1, :]
        sc = mod_ref[0, 1:2, :]

        def body(r, carry):
            rows = pl.ds(pl.multiple_of(r * rb, rb), rb)
            h = _ln(x_ref[rows, :]) * (1.0 + sc) + sh
            hb = h.astype(BF16)
            h_ref[rows, :] = hb
            dt_ref[rows, :] = _dot(hb, wdt_ref[...])
            return carry

        lax.fori_loop(0, tm // rb, body, 0)

    o_ref[...] = _dot(h_ref[...], w_ref[...]).astype(o_ref.dtype)


def _inproj(x, mod, w_main, w_dt, *, tm, tn, n_lat, S):
    M, D = x.shape
    N = w_main.shape[1]

    def midx(i):
        return jnp.where(i * tm < n_lat, 1 + (i * tm) // S, 0)

    return pl.pallas_call(
        functools.partial(_inproj_kernel, tm=tm, rb=min(tm, 256)),
        grid=(M // tm, N // tn),
        in_specs=[pl.BlockSpec((tm, D), lambda i, j: (i, 0)),
                  pl.BlockSpec((1, 6, D), lambda i, j: (midx(i), 0, 0)),
                  pl.BlockSpec((D, tn), lambda i, j: (0, j)),
                  pl.BlockSpec((D, LANES), lambda i, j: (0, 0))],
        out_specs=[pl.BlockSpec((tm, tn), lambda i, j: (i, j)),
                   pl.BlockSpec((tm, LANES), lambda i, j: (i, 0))],
        out_shape=[jax.ShapeDtypeStruct((M, N), BF16), jax.ShapeDtypeStruct((M, LANES), F32)],
        scratch_shapes=[pltpu.VMEM((tm, D), BF16)],
        compiler_params=_cparams("parallel", "arbitrary"),
        name="inproj",
    )(x, mod, w_main, w_dt)


def _chunk_of_step(i, ncc, nlc, rev):
    if not rev:
        return i
    return jnp.where(i < ncc, ncc - 1 - i, ncc + nlc - 1 - (i - ncc))


def _block_of_chunk(b, c, B, ncc, nlc):
    return jnp.where(c < ncc, B * nlc + b * ncc + c, b * nlc + (c - ncc))


def _seq_specs(width, col, *, B, ncc, nlc, rev, n_rows):
    hb = SEQ_CHUNK // HALO
    last = n_rows // HALO - 1

    def blk(b, i):
        return _block_of_chunk(b, _chunk_of_step(i, ncc, nlc, rev), B, ncc, nlc)

    main = pl.BlockSpec((SEQ_CHUNK, width), lambda b, i: (blk(b, i), col))
    prev = pl.BlockSpec((HALO, width), lambda b, i: (jnp.maximum(blk(b, i) * hb - 1, 0), col))
    nxt = pl.BlockSpec((HALO, width), lambda b, i: (jnp.minimum((blk(b, i) + 1) * hb, last), col))
    return blk, main, prev, nxt


def _dwconv_chunk(x_ref, xp_ref, xn_ref, cw_ref, cb_ref, c, ncc, nch):
    T = SEQ_CHUNK
    x = x_ref[...].astype(F32)
    has_prev = jnp.where((c == 0) | (c == ncc), 0.0, 1.0)
    has_next = jnp.where((c == ncc - 1) | (c == nch - 1), 0.0, 1.0)
    p6 = xp_ref[HALO - 2:HALO - 1, :].astype(F32) * has_prev
    p7 = xp_ref[HALO - 1:HALO, :].astype(F32) * has_prev
    n0 = xn_ref[0:1, :].astype(F32) * has_next
    row = lax.broadcasted_iota(jnp.int32, (T, 1), 0)
    xm1 = jnp.where(row == 0, p7, pltpu.roll(x, 1, 0))
    xm2 = jnp.where(row == 0, p6, jnp.where(row == 1, p7, pltpu.roll(x, 2, 0)))
    xp1 = jnp.where(row == T - 1, n0, pltpu.roll(x, T - 1, 0))
    return (cw_ref[0:1, :] * xm2 + cw_ref[1:2, :] * xm1 + cw_ref[2:3, :] * x
            + cw_ref[3:4, :] * xp1 + cb_ref[...])


def _linear_scan(a, b, rev):
    T = a.shape[0]
    row = lax.broadcasted_iota(jnp.int32, (T, 1), 0)
    s = 1
    while s < T:
        if s % 8:
            valid = (row < T - s) if rev else (row >= s)
            shift = T - s if rev else s
            a_p = jnp.where(valid, pltpu.roll(a, shift, 0), 1.0)
            b_p = jnp.where(valid, pltpu.roll(b, shift, 0), 0.0)
            b = a * b_p + b
            a = a * a_p
        elif rev:
            b = jnp.concatenate([a[:T - s] * b[s:] + b[:T - s], b[T - s:]], axis=0)
            a = jnp.concatenate([a[:T - s] * a[s:], a[T - s:]], axis=0)
        else:
            b = jnp.concatenate([b[:s], a[s:] * b[:T - s] + b[s:]], axis=0)
            a = jnp.concatenate([a[:s], a[s:] * a[:T - s]], axis=0)
        s *= 2
    return a, b


def _lru_kernel(x_ref, xp_ref, xn_ref, cw_ref, cb_ref, wg_ref, gb_ref, lam_ref, o_ref, carry_ref,
                *, rev, ncc, nlc):
    T = SEQ_CHUNK
    i = pl.program_id(1)
    c = _chunk_of_step(i, ncc, nlc, rev)

    @pl.when(i == 0)
    def _():
        carry_ref[...] = jnp.zeros_like(carry_ref)

    xc = _dwconv_chunk(x_ref, xp_ref, xn_ref, cw_ref, cb_ref, c, ncc, ncc + nlc)
    sp = _softplus(-lam_ref[...])
    for n in range(LRU_BLOCKS):
        sl = slice(n * LRU_BLOCK_DIM, (n + 1) * LRU_BLOCK_DIM)
        xb = xc[:, sl]
        gl = _dot(xb.astype(BF16), wg_ref[n])
        r_gate = _sigmoid(gl[:, :LRU_BLOCK_DIM] + gb_ref[0:1, sl])
        i_gate = _sigmoid(gl[:, LRU_BLOCK_DIM:] + gb_ref[1:2, sl])
        log_a = -LRU_C * r_gate * sp[:, sl]
        a = jnp.exp(log_a)
        om = 1.0 - a * a
        root = jnp.where(om > 0.0, om * lax.rsqrt(om), 0.0)
        bb = root * (i_gate * xb)
        a_s, b_s = _linear_scan(a, bb, rev)
        h = b_s + a_s * carry_ref[0:1, sl]
        o_ref[:, sl] = h.astype(o_ref.dtype)
        carry_ref[0:1, sl] = h[0:1, :] if rev else h[T - 1:T, :]


def _lru_dir(P, conv_w, conv_b, wg, gb, lam, *, rev, B, ncc, nlc):
    M = P.shape[0]
    W = LRU_WIDTH
    blk, main, prev, nxt = _seq_specs(W, COL_LRU_X, B=B, ncc=ncc, nlc=nlc, rev=rev, n_rows=M)
    full2 = lambda shape: pl.BlockSpec(shape, lambda b, i: (0, 0))
    return pl.pallas_call(
        functools.partial(_lru_kernel, rev=rev, ncc=ncc, nlc=nlc),
        grid=(B, ncc + nlc),
        in_specs=[main, prev, nxt, full2((4, W)), full2((1, W)),
                  pl.BlockSpec((LRU_BLOCKS, LRU_BLOCK_DIM, 2 * LRU_BLOCK_DIM), lambda b, i: (0, 0, 0)),
                  full2((2, W)), full2((1, W))],
        out_specs=pl.BlockSpec((SEQ_CHUNK, W), lambda b, i: (blk(b, i), 0)),
        out_shape=jax.ShapeDtypeStruct((M, W), BF16),
        scratch_shapes=[pltpu.VMEM((8, W), F32)],
        compiler_params=_cparams("parallel", "arbitrary"),
        name="lru_bwd" if rev else "lru_fwd",
    )(P, P, P, conv_w, conv_b, wg, gb, lam)


def _ssd_kernel(x_ref, xp_ref, xn_ref, dt_ref, cw_ref, cb_ref, sp_ref, o_ref, st_ref,
                *, rev, d, ncc, nlc, add_skip):
    T = SEQ_CHUNK
    N = SSD_STATE
    HPG = SSD_HEADS // SSD_GROUPS
    GW = HPG * SSD_HEAD_DIM
    i = pl.program_id(1)
    c = _chunk_of_step(i, ncc, nlc, rev)

    @pl.when(i == 0)
    def _():
        st_ref[...] = jnp.zeros_like(st_ref)

    xc = _dwconv_chunk(x_ref, xp_ref, xn_ref, cw_ref, cb_ref, c, ncc, ncc + nlc)
    xbc = xc * _sigmoid(xc)
    xs = xbc[:, :SSD_WIDTH]
    Bm = xbc[:, SSD_WIDTH:SSD_WIDTH + SSD_GROUPS * N]
    Cm = xbc[:, SSD_WIDTH + SSD_GROUPS * N:]

    lane = lax.broadcasted_iota(jnp.int32, (1, LANES), 1)
    in_dir = (lane >= SSD_HEADS * d) & (lane < SSD_HEADS * (d + 1))
    dt = jnp.where(in_dir, _softplus(dt_ref[...] + sp_ref[0:1, :]), 0.0)
    a_neg = jnp.where(in_dir, -jnp.exp(sp_ref[1:2, :]), 0.0)
    a = dt * a_neg

    r_i = lax.broadcasted_iota(jnp.int32, (T, T), 0)
    c_i = lax.broadcasted_iota(jnp.int32, (T, T), 1)
    tri = (c_i >= r_i) if rev else (c_i <= r_i)
    tri_b = jnp.where(tri, 1.0, 0.0).astype(BF16)
    cs = sum(_dot(tri_b, p) for p in _split3(a))
    e_r = lax.broadcasted_iota(jnp.int32, (LANES, LANES), 0)
    e_c = lax.broadcasted_iota(jnp.int32, (LANES, LANES), 1)
    eye = jnp.where(e_r == e_c, 1.0, 0.0).astype(BF16)
    cs_row = sum(_dot_nt(eye, p) for p in _split3(cs))
    cs_end = cs[0:1, :] if rev else cs[T - 1:T, :]

    x_r = lax.broadcasted_iota(jnp.int32, (LANES, SSD_WIDTH), 0)
    x_c = lax.broadcasted_iota(jnp.int32, (LANES, SSD_WIDTH), 1)
    expand_m = jnp.where(x_r == SSD_HEADS * d + x_c // SSD_HEAD_DIM, 1.0, 0.0).astype(BF16)

    def expand(v):
        return sum(_dot(p, expand_m) for p in _split3(v))

    dt_e = expand(dt)
    dec_out_e = expand(jnp.exp(cs))
    dec_st_e = expand(jnp.exp(cs_end - cs))
    row8 = lax.broadcasted_iota(jnp.int32, (8, LANES), 0)
    small = jnp.where(row8 == 0, jnp.exp(cs_end), jnp.where(row8 == 1, sp_ref[2:3, :], 0.0))
    small_e = expand(small)
    dec_ch_e = small_e[0:1, :]
    dskip_e = small_e[1:2, :]

    xdt = xs * dt_e
    xdt_b = xdt.astype(BF16)
    z_st = (xdt * dec_st_e).astype(BF16)
    lo_half = lax.broadcasted_iota(jnp.int32, (1, LANES), 1) < SSD_HEAD_DIM

    for g in range(SSD_GROUPS):
        Bg = Bm[:, g * N:(g + 1) * N].astype(BF16)
        Cg = Cm[:, g * N:(g + 1) * N].astype(BF16)
        CB = _dot_nt(Cg, Bg)
        gs = slice(g * GW, (g + 1) * GW)
        Sg = st_ref[:, gs]
        y_off = _dot(Cg, Sg.astype(BF16)) * dec_out_e[:, gs]
        for p in range(HPG // 2):
            col = g * GW + p * LANES
            x_pair = xdt_b[:, col:col + LANES]
            y_pair = y_off[:, p * LANES:(p + 1) * LANES]
            for q in range(2):
                hh = SSD_HEADS * d + g * HPG + 2 * p + q
                L = jnp.exp(jnp.where(tri, cs[:, hh:hh + 1] - cs_row[hh:hh + 1, :], NEG_BIG))
                Wm = (CB * L).astype(BF16)
                xm = jnp.where(lo_half if q == 0 else jnp.logical_not(lo_half), x_pair,
                               jnp.zeros_like(x_pair))
                y_pair = y_pair + _dot(Wm, xm)
            if add_skip:
                y_pair = y_pair + dskip_e[:, col:col + LANES] * xs[:, col:col + LANES]
            o_ref[:, col:col + LANES] = y_pair.astype(o_ref.dtype)
        BgT = _dot_nt(eye, Bg).astype(BF16)
        st_ref[:, gs] = dec_ch_e[:, gs] * Sg + _dot(BgT, z_st[:, gs])


def _ssd_dir(P, dtp, conv_w, conv_b, sp, *, rev, d, B, ncc, nlc, add_skip):
    M = P.shape[0]
    XW = SSD_WIDTH + 2 * SSD_GROUPS * SSD_STATE
    blk, main, prev, nxt = _seq_specs(XW, COL_XBC, B=B, ncc=ncc, nlc=nlc, rev=rev, n_rows=M)
    full2 = lambda shape: pl.BlockSpec(shape, lambda b, i: (0, 0))
    return pl.pallas_call(
        functools.partial(_ssd_kernel, rev=rev, d=d, ncc=ncc, nlc=nlc, add_skip=add_skip),
        grid=(B, ncc + nlc),
        in_specs=[main, prev, nxt,
                  pl.BlockSpec((SEQ_CHUNK, LANES), lambda b, i: (blk(b, i), 0)),
                  full2((4, XW)), full2((1, XW)), full2((8, LANES))],
        out_specs=pl.BlockSpec((SEQ_CHUNK, SSD_WIDTH), lambda b, i: (blk(b, i), 0)),
        out_shape=jax.ShapeDtypeStruct((M, SSD_WIDTH), BF16),
        scratch_shapes=[pltpu.VMEM((SSD_STATE, SSD_WIDTH), F32)],
        compiler_params=_cparams("parallel", "arbitrary"),
        name="ssd_bwd" if rev else "ssd_fwd",
    )(P, P, P, dtp, conv_w, conv_b, sp)


def _rope_kernel(q_ref, k_ref, c_ref, sa_ref, sb_ref, qo_ref, ko_ref, *, scale):
    W = q_ref.shape[1]
    reps = W // LANES
    cos = jnp.concatenate([c_ref[...]] * reps, axis=1)
    sa = jnp.concatenate([sa_ref[...]] * reps, axis=1)
    sb = jnp.concatenate([sb_ref[...]] * reps, axis=1)
    quarter = ATT_QK_DIM // 4

    def rot(x):
        return x * cos + pltpu.roll(x, W - quarter, 1) * sa + pltpu.roll(x, quarter, 1) * sb

    qo_ref[...] = (rot(q_ref[...].astype(F32)) * scale).astype(qo_ref.dtype)
    ko_ref[...] = rot(k_ref[...].astype(F32)).astype(ko_ref.dtype)


def _rope(P, cos, sa, sb, *, tm):
    M = P.shape[0]
    W = ATT_HEADS * 2 * ATT_QK_DIM
    tab = pl.BlockSpec((tm, LANES), lambda i: (i, 0))
    return pl.pallas_call(
        functools.partial(_rope_kernel, scale=ATT_QK_DIM ** -0.5),
        grid=(M // tm,),
        in_specs=[pl.BlockSpec((tm, W), lambda i: (i, COL_Q)),
                  pl.BlockSpec((tm, W), lambda i: (i, COL_K)), tab, tab, tab],
        out_specs=[pl.BlockSpec((tm, W), lambda i: (i, 0))] * 2,
        out_shape=[jax.ShapeDtypeStruct((M, W), BF16)] * 2,
        compiler_params=_cparams("parallel"),
        name="rope",
    )(P, P, cos, sa, sb)


def _attn_kernel(*refs, tq, tk, n_ctx, n_lat, lam_init):
    if n_lat:
        (q_ref, kc_ref, vc_ref, kl_ref, vl_ref, lv_ref, nw_ref, o_ref, m_ref, l_ref, acc_ref) = refs
    else:
        (q_ref, kc_ref, vc_ref, lv_ref, nw_ref, o_ref, m_ref, l_ref, acc_ref) = refs
        kl_ref = vl_ref = None

    q = q_ref[...]
    lane = lax.broadcasted_iota(jnp.int32, (1, LANES), 1)
    zero = jnp.zeros_like(q)
    qq = jnp.concatenate([jnp.where(lane < ATT_QK_DIM, q, zero),
                          jnp.where(lane >= ATT_QK_DIM, q, zero)], axis=0)

    m_ref[...] = jnp.full(m_ref.shape, NEG_BIG, F32)
    l_ref[...] = jnp.zeros_like(l_ref)
    acc_ref[...] = jnp.zeros_like(acc_ref)

    def lane_blocks(s):
        return [s[:, j * LANES:(j + 1) * LANES] for j in range(s.shape[1] // LANES)]

    def over_keys(fn):
        tc = min(n_ctx, tk)
        for j in range(n_ctx // tc):
            fn(kc_ref[j * tc:(j + 1) * tc, :], vc_ref[j * tc:(j + 1) * tc, :])
        if n_lat:
            def body(j, carry):
                rows = pl.ds(pl.multiple_of(j * tk, tk), tk)
                fn(kl_ref[rows, :], vl_ref[rows, :])
                return carry
            lax.fori_loop(0, n_lat // tk, body, 0, unroll=2)

    def step(kc, vc):
        blocks = lane_blocks(_dot_nt(qq, kc))
        m_prev = m_ref[...]
        smax = functools.reduce(jnp.maximum, blocks)
        m_new = jnp.maximum(m_prev, jnp.max(smax, axis=1, keepdims=True))
        alpha = jnp.exp(m_prev - m_new)
        p = [jnp.exp(sb - m_new) for sb in blocks]
        l_ref[...] = alpha * l_ref[...] + functools.reduce(jnp.add, p)
        pb = jnp.concatenate([x.astype(BF16) for x in p], axis=1)
        acc_ref[...] = alpha * acc_ref[...] + _dot(pb, vc)
        m_ref[...] = m_new

    over_keys(step)
    l_ref[...] = jnp.broadcast_to(jnp.sum(l_ref[...], axis=1, keepdims=True), l_ref.shape)

    lv = lv_ref[...]
    s01 = jnp.sum(lv[0:1, :] * lv[1:2, :], axis=1, keepdims=True)
    s23 = jnp.sum(lv[2:3, :] * lv[3:4, :], axis=1, keepdims=True)
    lam = jnp.exp(s01) - jnp.exp(s23) + lam_init
    acc = acc_ref[...]
    l = l_ref[...]
    o = acc[:tq, :] / l[:tq, :] - lam * (acc[tq:, :] / l[tq:, :])
    o = o * lax.rsqrt(jnp.mean(o * o, axis=-1, keepdims=True) + LN_EPS) * nw_ref[...]
    o_ref[...] = (o * (1.0 - lam_init)).astype(o_ref.dtype)


def _attention(qr, kr, P, lam_vecs, norm_w, out_prev, *, B, S, CTX, lam_init, ctx_queries):
    M = qr.shape[0]
    H = ATT_HEADS
    tk = min(S, 1024)
    assert S % tk == 0 and (CTX % tk == 0 or tk % CTX == 0)
    vcol = COL_V * (1024 // LANES)
    n_lat_blk = (B * S) // CTX
    if ctx_queries:
        tq = min(CTX, 256)
        nq = CTX // tq
        q_spec = pl.BlockSpec((tq, LANES), lambda b, h, i: ((B * S + b * CTX) // tq + i, h))
    else:
        tq = 512
        nq = S // tq
        q_spec = pl.BlockSpec((tq, LANES), lambda b, h, i: (b * nq + i, h))
    in_specs = [q_spec,
                pl.BlockSpec((CTX, LANES), lambda b, h, i: (n_lat_blk + b, h)),
                pl.BlockSpec((CTX, LANES), lambda b, h, i: (n_lat_blk + b, vcol + h))]
    args = [qr, kr, P]
    if not ctx_queries:
        in_specs += [pl.BlockSpec((S, LANES), lambda b, h, i: (b, h)),
                     pl.BlockSpec((S, LANES), lambda b, h, i: (b, vcol + h))]
        args += [kr, P]
    in_specs += [pl.BlockSpec((4, ATT_QK_DIM), lambda b, h, i: (0, 0)),
                 pl.BlockSpec((1, LANES), lambda b, h, i: (0, 0))]
    args += [lam_vecs, norm_w]
    kwargs = {}
    if out_prev is not None:
        in_specs.append(pl.BlockSpec(memory_space=pl.ANY))
        args.append(out_prev)
        kwargs["input_output_aliases"] = {len(args) - 1: 0}

    kernel = functools.partial(_attn_kernel, tq=tq, tk=tk, n_ctx=CTX, n_lat=0 if ctx_queries else S,
                               lam_init=lam_init)
    if out_prev is not None:
        base = kernel
        kernel = lambda *refs: base(*refs[:len(args) - 1], *refs[len(args):])
    return pl.pallas_call(
        kernel,
        grid=(B, H, nq),
        in_specs=in_specs,
        out_specs=q_spec,
        out_shape=jax.ShapeDtypeStruct((M, H * ATT_V_DIM), BF16),
        scratch_shapes=[pltpu.VMEM((2 * tq, LANES), F32)] * 3,
        compiler_params=_cparams("parallel", "parallel", "arbitrary"),
        name="attn_ctx" if ctx_queries else "attn_lat",
        **kwargs,
    )(*args)


def _merge_kernel(hf_ref, hb_ref, gl_ref, yf_ref, yb_ref, z_ref, att_ref, g0_ref, g1_ref, g2_ref,
                  wb_ref, nw_ref, o_ref, br_ref):
    @pl.when(pl.program_id(1) == 0)
    def _():
        y = hf_ref[...].astype(F32) + hb_ref[...].astype(F32)
        g = gl_ref[...].astype(F32)
        gelu = 0.5 * g * (1.0 + jnp.tanh(0.7978845608028654 * (g + 0.044715 * (g * g * g))))
        br_ref[0] = (y * gelu).astype(BF16)
        ys = yf_ref[...].astype(F32) + yb_ref[...].astype(F32)
        zz = z_ref[...].astype(F32)
        t = ys * (zz * _sigmoid(zz))
        t = t * lax.rsqrt(jnp.mean(t * t, axis=-1, keepdims=True) + LN_EPS) * nw_ref[...]
        br_ref[1] = t.astype(BF16)
        br_ref[2] = att_ref[...]

    acc = _sigmoid(g0_ref[...].astype(F32)) * _dot(br_ref[0], wb_ref[0])
    acc = acc + _sigmoid(g1_ref[...].astype(F32)) * _dot(br_ref[1], wb_ref[1])
    acc = acc + _sigmoid(g2_ref[...].astype(F32)) * _dot(br_ref[2], wb_ref[2])
    o_ref[...] = acc.astype(o_ref.dtype)


def _merge(hf, hb, yf, yb, att, P, wb, ssd_norm_w, *, n_rows, tm, tn):
    M = P.shape[0]
    D = wb.shape[2]
    BW = wb.shape[1]
    row = lambda col: pl.BlockSpec((tm, BW), lambda i, j: (i, col))
    gcol = COL_GATES * 1024 // tn
    gate = lambda k: pl.BlockSpec((tm, tn), lambda i, j: (i, gcol + k * (D // tn) + j))
    return pl.pallas_call(
        _merge_kernel,
        grid=(n_rows // tm, D // tn),
        in_specs=[row(0), row(0), row(COL_LRU_G), row(0), row(0), row(COL_Z), row(0),
                  gate(0), gate(1), gate(2),
                  pl.BlockSpec((3, BW, tn), lambda i, j: (0, 0, j)),
                  pl.BlockSpec((1, BW), lambda i, j: (0, 0))],
        out_specs=pl.BlockSpec((tm, tn), lambda i, j: (i, j)),
        out_shape=jax.ShapeDtypeStruct((M, D), BF16),
        scratch_shapes=[pltpu.VMEM((3, tm, BW), BF16)],
        compiler_params=_cparams("parallel", "arbitrary"),
        name="merge",
    )(hf, hb, P, yf, yb, P, att, P, P, P, wb, ssd_norm_w)


def _route(logits, rbias):
    G, E = N_EXPERT_GROUPS, EXPERTS_PER_GROUP
    mx = jnp.max(logits, axis=0, keepdims=True)
    ex = jnp.exp(logits - mx)
    sc = ex / jnp.sum(ex, axis=0, keepdims=True)
    sel = sc + rbias
    srow = [sel[r:r + 1, :] for r in range(G * E)]
    prow = [sc[r:r + 1, :] for r in range(G * E)]
    gscore = []
    for g in range(G):
        v = srow[g * E:(g + 1) * E]
        best = v[0] + v[1]
        for a in range(E):
            for b in range(a + 1, E):
                if (a, b) != (0, 1):
                    best = jnp.maximum(best, v[a] + v[b])
        gscore.append(best)
    gmax = functools.reduce(jnp.maximum, gscore)
    gidx = jnp.full(gmax.shape, G - 1, jnp.int32)
    for g in range(G - 2, -1, -1):
        gidx = jnp.where(gscore[g] == gmax, g, gidx)

    def pick(rows, k):
        out = rows[(G - 1) * E + k]
        for g in range(G - 2, -1, -1):
            out = jnp.where(gidx == g, rows[g * E + k], out)
        return out

    v = [pick(srow, k) for k in range(E)]
    p = [pick(prow, k) for k in range(E)]
    chosen = []
    for k in range(E):
        rank = jnp.zeros(gmax.shape, jnp.int32)
        for j in range(E):
            if j == k:
                continue
            ahead = (v[j] > v[k]) | ((v[j] == v[k]) & (j < k)) if j < k else (v[j] > v[k])
            rank = rank + jnp.where(ahead, 1, 0)
        chosen.append(rank < 2)
    first = jnp.full(gmax.shape, E, jnp.int32)
    second = jnp.full(gmax.shape, -1, jnp.int32)
    for k in range(E):
        first = jnp.minimum(first, jnp.where(chosen[k], k, E))
        second = jnp.maximum(second, jnp.where(chosen[k], k, -1))
    p_first = jnp.zeros(gmax.shape, F32)
    p_second = jnp.zeros(gmax.shape, F32)
    for k in range(E):
        p_first = p_first + jnp.where(first == k, p[k], 0.0)
        p_second = p_second + jnp.where(second == k, p[k], 0.0)
    tot = p_first + p_second
    return gidx * E + first, gidx * E + second, p_first / tot, p_second / tot


def _outproj_kernel(m_ref, w_ref, x_ref, mod_ref, lng_ref, lnb_ref, wr_ref, rb_ref,
                    x1_ref, h2_ref, ids_ref, wts_ref, cnt_ref, *, tm, alpha):
    o = _dot(m_ref[...], w_ref[...])
    g1 = mod_ref[0, 2:3, :]
    sh2 = mod_ref[0, 3:4, :]
    sc2 = mod_ref[0, 4:5, :]
    x1 = _ln(alpha * x_ref[...] + g1 * o) * lng_ref[...] + lnb_ref[...]
    x1_ref[...] = x1
    h2 = _ln(x1) * (1.0 + sc2) + sh2
    _store_slabs(h2_ref, h2, tm)
    h_hi, h_lo = _split2(h2)
    w_hi, w_lo = _split2(wr_ref[...])
    logits = _dot_nt(w_hi, h_hi) + _dot_nt(w_lo, h_hi) + _dot_nt(w_hi, h_lo)
    e0, e1, w0, w1 = _route(logits, rb_ref[...])
    erow = lax.broadcasted_iota(jnp.int32, (N_EXPERTS, tm), 0)
    hit0 = erow == e0
    hit1 = erow == e1
    cnt = jnp.where(hit0, 1.0, 0.0) + jnp.where(hit1, 1.0, 0.0)
    t_r = lax.broadcasted_iota(jnp.int32, (tm, tm), 0)
    t_c = lax.broadcasted_iota(jnp.int32, (tm, tm), 1)
    before = jnp.where(t_r < t_c, 1.0, 0.0).astype(BF16)
    pre = _dot(cnt.astype(BF16), before)
    r0 = jnp.sum(jnp.where(hit0, pre, 0.0), axis=0, keepdims=True).astype(jnp.int32)
    r1 = jnp.sum(jnp.where(hit1, pre, 0.0), axis=0, keepdims=True).astype(jnp.int32)
    cnt_ref[...] = jnp.broadcast_to(jnp.sum(cnt, axis=1, keepdims=True), cnt_ref.shape)
    row = lax.broadcasted_iota(jnp.int32, (8, tm), 0)
    ids_ref[...] = jnp.where(row == 0, e0, jnp.where(row == 1, e1,
                             jnp.where(row == 2, r0, jnp.where(row == 3, r1, 0))))
    wts_ref[...] = jnp.where(row == 0, w0, jnp.where(row == 1, w1, 0.0))


def _outproj(m, w_out, x, mod, ln_g, ln_b, wr_t, rbias, *, n_rows, tm, n_lat, S, alpha):
    M, D = x.shape

    def midx(i):
        return jnp.where(i * tm < n_lat, 1 + (i * tm) // S, 0)

    vec = pl.BlockSpec((1, D), lambda i: (0, 0))
    return pl.pallas_call(
        functools.partial(_outproj_kernel, tm=tm, alpha=alpha),
        grid=(n_rows // tm,),
        in_specs=[pl.BlockSpec((tm, D), lambda i: (i, 0)),
                  pl.BlockSpec((D, D), lambda i: (0, 0)),
                  pl.BlockSpec((tm, D), lambda i: (i, 0)),
                  pl.BlockSpec((1, 6, D), lambda i: (midx(i), 0, 0)),
                  vec, vec,
                  pl.BlockSpec((N_EXPERTS, D), lambda i: (0, 0)),
                  pl.BlockSpec((N_EXPERTS, 1), lambda i: (0, 0))],
        out_specs=[pl.BlockSpec((tm, D), lambda i: (i, 0)),
                   pl.BlockSpec((tm * SLAB, LANES), lambda i: (i, 0)),
                   pl.BlockSpec((8, tm), lambda i: (0, i)),
                   pl.BlockSpec((8, tm), lambda i: (0, i)),
                   pl.BlockSpec((N_EXPERTS, LANES), lambda i: (0, i))],
        out_shape=[jax.ShapeDtypeStruct((M, D), F32),
                   jax.ShapeDtypeStruct((n_rows * SLAB, LANES), jnp.uint32),
                   jax.ShapeDtypeStruct((8, n_rows), jnp.int32),
                   jax.ShapeDtypeStruct((8, n_rows), F32),
                   jax.ShapeDtypeStruct((N_EXPERTS, (n_rows // tm) * LANES), F32)],
        compiler_params=_cparams("parallel"),
        name="outproj",
    )(m, w_out, x, mod, ln_g, ln_b, wr_t, rbias)


def _ffn_kernel(te_ref, tv_ref, src_ref, dst_ref, h_hbm, wg_ref, wu_ref, wd_ref, y_hbm,
                xbuf, ybuf, xb_ref, gsem, ssem, *, tm):
    i = pl.program_id(0)
    n = pl.num_programs(0)
    slot = i % 2

    def gather_start(tile, s):
        def issue(r2, carry):
            for prio in range(2):
                r = 2 * r2 + prio
                tok = src_ref[tile * tm + r]
                pltpu.make_async_copy(h_hbm.at[pl.ds(pl.multiple_of(tok * SLAB, SLAB), SLAB)],
                                      xbuf.at[s, pl.ds(pl.multiple_of(r * SLAB, SLAB), SLAB)],
                                      gsem.at[s]).start(priority=prio)
            return carry
        lax.fori_loop(0, tm // 2, issue, 0)

    def gather_wait(s):
        pltpu.make_async_copy(h_hbm.at[pl.ds(0, tm * SLAB)], xbuf.at[s], gsem.at[s]).wait()

    def scatter_start(tile):
        def issue(r2, carry):
            for prio in range(2):
                r = 2 * r2 + prio
                tok = dst_ref[tile * tm + r]
                pltpu.make_async_copy(ybuf.at[pl.ds(pl.multiple_of(r * SLAB, SLAB), SLAB)],
                                      y_hbm.at[pl.ds(pl.multiple_of(tok * SLAB, SLAB), SLAB)],
                                      ssem).start(priority=prio)
            return carry
        lax.fori_loop(0, tm // 2, issue, 0)

    def scatter_wait():
        pltpu.make_async_copy(ybuf, y_hbm.at[pl.ds(0, tm * SLAB)], ssem).wait()

    @pl.when(i == 0)
    def _():
        gather_start(0, 0)

    @pl.when(tv_ref[i] != 0)
    def _():
        has_next = jnp.logical_and(i + 1 < n, tv_ref[jnp.minimum(i + 1, n - 1)] != 0)

        @pl.when(has_next)
        def _():
            gather_start(i + 1, 1 - slot)

        gather_wait(slot)
        half = SLAB * LANES
        for k in range(SLAB):
            lo, hi = _load_slab_piece(xbuf[slot, pl.ds(k, tm, stride=SLAB), :])
            xb_ref[:, k * LANES:(k + 1) * LANES] = lo.astype(BF16)
            xb_ref[:, half + k * LANES:half + (k + 1) * LANES] = hi.astype(BF16)
        x = xb_ref[...]
        g = _dot(x, wg_ref[0])
        u = _dot(x, wu_ref[0])
        act = (g * _sigmoid(g) * u).astype(BF16)
        y = _dot(act, wd_ref[0])

        @pl.when(i > 0)
        def _():
            scatter_wait()

        _store_slabs(ybuf, y, tm)
        scatter_start(i)

        @pl.when(jnp.logical_not(has_next))
        def _():
            scatter_wait()


def _expert_ffn(h2, src_tok, dst_tok, tile_e, tile_v, wg, wu, wd, *, tm, n_out):
    n_tiles = tile_e.shape[0]
    E, D, F = wg.shape
    return pl.pallas_call(
        functools.partial(_ffn_kernel, tm=tm),
        grid_spec=pltpu.PrefetchScalarGridSpec(
            num_scalar_prefetch=4,
            grid=(n_tiles,),
            in_specs=[pl.BlockSpec(memory_space=pl.ANY),
                      pl.BlockSpec((1, D, F), lambda i, te, tv, sr, ds: (te[i], 0, 0)),
                      pl.BlockSpec((1, D, F), lambda i, te, tv, sr, ds: (te[i], 0, 0)),
                      pl.BlockSpec((1, F, D), lambda i, te, tv, sr, ds: (te[i], 0, 0))],
            out_specs=pl.BlockSpec(memory_space=pl.ANY),
            scratch_shapes=[pltpu.VMEM((2, tm * SLAB, LANES), jnp.uint32),
                            pltpu.VMEM((tm * SLAB, LANES), jnp.uint32),
                            pltpu.VMEM((tm, D), BF16),
                            pltpu.SemaphoreType.DMA((2,)),
                            pltpu.SemaphoreType.DMA(())]),
        out_shape=jax.ShapeDtypeStruct((n_out * SLAB, LANES), jnp.uint32),
        compiler_params=_cparams("arbitrary"),
        name="expert_ffn",
    )(tile_e, tile_v, src_tok, dst_tok, h2, wg, wu, wd)


def _final_kernel(x_ref, y0_ref, y1_ref, w_ref, mod_ref, lng_ref, lnb_ref, o_ref, f_ref, *, tm, alpha):
    half = SLAB * LANES
    w0 = w_ref[:, 0:1]
    w1 = w_ref[:, 1:2]
    for k in range(SLAB):
        rows = pl.ds(k, tm, stride=SLAB)
        lo0, hi0 = _load_slab_piece(y0_ref[rows, :])
        lo1, hi1 = _load_slab_piece(y1_ref[rows, :])
        f_ref[:, k * LANES:(k + 1) * LANES] = w0 * lo0 + w1 * lo1
        f_ref[:, half + k * LANES:half + (k + 1) * LANES] = w0 * hi0 + w1 * hi1
    g2 = mod_ref[0, 5:6, :]
    o_ref[...] = _ln(alpha * x_ref[...] + g2 * f_ref[...]) * lng_ref[...] + lnb_ref[...]


def _final(x1, yg, wcol, mod, ln_g, ln_b, *, n_rows, tm, n_lat, S, alpha):
    D = x1.shape[1]

    def midx(i):
        return jnp.where(i * tm < n_lat, 1 + (i * tm) // S, 0)

    nt = n_rows // tm
    vec = pl.BlockSpec((1, D), lambda i: (0, 0))
    return pl.pallas_call(
        functools.partial(_final_kernel, tm=tm, alpha=alpha),
        grid=(nt,),
        in_specs=[pl.BlockSpec((tm, D), lambda i: (i, 0)),
                  pl.BlockSpec((tm * SLAB, LANES), lambda i: (i, 0)),
                  pl.BlockSpec((tm * SLAB, LANES), lambda i: (nt + i, 0)),
                  pl.BlockSpec((tm, 2), lambda i: (i, 0)),
                  pl.BlockSpec((1, 6, D), lambda i: (midx(i), 0, 0)),
                  vec, vec],
        out_specs=pl.BlockSpec((tm, D), lambda i: (i, 0)),
        out_shape=jax.ShapeDtypeStruct((n_rows, D), F32),
        scratch_shapes=[pltpu.VMEM((tm, D), F32)],
        compiler_params=_cparams("parallel"),
        name="final",
    )(x1, yg, yg, wcol, mod, ln_g, ln_b)


def _rope_tables(B, S, CTX):
    t = jnp.arange(S)
    row = (t // GRID_W).astype(F32)
    col = (t % GRID_W).astype(F32)
    axis_dim = ATT_QK_DIM // 2
    freqs = ROPE_THETA ** (-jnp.arange(0, axis_dim, 2, dtype=F32) / axis_dim)
    ang = jnp.stack([row[:, None] * freqs, col[:, None] * freqs], axis=1)
    cos, sin = jnp.cos(ang), jnp.sin(ang)
    zero = jnp.zeros_like(sin)
    c64 = jnp.concatenate([cos, cos], axis=-1).reshape(S, ATT_QK_DIM)
    sa64 = jnp.concatenate([-sin, zero], axis=-1).reshape(S, ATT_QK_DIM)
    sb64 = jnp.concatenate([zero, sin], axis=-1).reshape(S, ATT_QK_DIM)

    def full(tab, ctx_val):
        lat = jnp.tile(jnp.concatenate([tab, tab], axis=-1), (B, 1))
        ctx = jnp.full((B * CTX, LANES), ctx_val, F32)
        return jnp.concatenate([lat, ctx], axis=0)

    return full(c64, 1.0), full(sa64, 0.0), full(sb64, 0.0)


def _moe_plan(ids, tile_counts, tm, tm_route):
    n_tok = ids.shape[1]
    experts = jnp.arange(N_EXPERTS, dtype=jnp.int32)
    counts = jnp.sum(tile_counts, axis=0)
    padded = ((counts + tm - 1) // tm) * tm
    ends = jnp.cumsum(padded)
    starts = ends - padded
    tile_base = starts[None, :] + jnp.cumsum(tile_counts, axis=0) - tile_counts
    base_tok = jnp.repeat(tile_base, tm_route, axis=0)
    pos = jnp.stack([jnp.sum(jnp.where(experts[None, :] == ids[k][:, None], base_tok, 0), axis=1)
                     + ids[2 + k] for k in range(2)]).reshape(-1).astype(jnp.int32)
    n_slots = 2 * n_tok + N_EXPERTS * tm
    slot_of = jnp.full((n_slots,), -1, jnp.int32).at[pos].set(
        jnp.arange(2 * n_tok, dtype=jnp.int32), unique_indices=True, mode="promise_in_bounds")
    real = slot_of >= 0
    src_tok = jnp.where(real, slot_of % n_tok, 0)
    dst_tok = jnp.where(real, slot_of, 2 * n_tok + jnp.arange(n_slots, dtype=jnp.int32) % tm)
    tile_start = jnp.arange(n_slots // tm, dtype=jnp.int32) * tm
    tile_e = jnp.minimum(jnp.sum((ends[None, :] <= tile_start[:, None]).astype(jnp.int32), axis=1),
                         N_EXPERTS - 1).astype(jnp.int32)
    tile_v = (tile_start < ends[-1]).astype(jnp.int32)
    return src_tok, dst_tok, tile_e, tile_v


def _row_tile(limit, *dims):
    t = limit
    while any(d % t for d in dims):
        t //= 2
    return t


def kernel(x, c, ctx, c_ctx, w_ada, b_ada, w_in, lru_conv_w, lru_conv_b, lru_gate_w, lru_gate_b, lru_lambda, ssd_conv_w, ssd_conv_b, ssd_a_log, ssd_dt_bias, ssd_d, ssd_norm_w, att_lambda, att_norm_w, w_branch, w_out, ln_g, ln_b, w_router, router_bias, w_gate_e, w_up_e, w_down_e):
    B, S, D = x.shape
    CTX = ctx.shape[1]
    depth = w_ada.shape[0]
    alpha = (2 * depth) ** 0.25
    n_lat = B * S
    n_ctx = B * CTX
    M = n_lat + n_ctx
    assert D == 2 * SLAB * LANES and CTX % SEQ_CHUNK == 0 and S % SEQ_CHUNK == 0 and S % CTX == 0
    ncc, nlc = CTX // SEQ_CHUNK, S // SEQ_CHUNK

    xs = jnp.concatenate([x.reshape(n_lat, D), ctx.reshape(n_ctx, D)], axis=0)
    cvec = jnp.concatenate([c_ctx[None, :], c, jnp.zeros((8 - 1 - B, D), F32)], axis=0)
    cos_t, sa_t, sb_t = _rope_tables(B, S, CTX)
    wr_t = w_router.T
    rbias = router_bias.reshape(N_EXPERTS, 1)

    tm_big = _row_tile(1024, S, n_ctx)
    tm_mid = _row_tile(512, S, n_ctx)
    tm_small = _row_tile(256, S, n_ctx)

    i0, i1, i2, i3, i4, i5, i6, i7, i8 = [int(v) for v in
                                          (0, 1024, 2048, 3072, 5120, 5152, 6176, 7200, 8224)]
    for l in range(depth):
        need_ctx = l < depth - 1
        lam_init = 0.8 - 0.6 * math.exp(-0.3 * l)
        n_rows = M if need_ctx else n_lat

        wl = w_in[l]
        w_main = jnp.concatenate([wl[:, i3:i4], wl[:, i0:i3], wl[:, i5:]], axis=1).astype(BF16)
        w_dt = jnp.pad(wl[:, i4:i5], ((0, 0), (0, LANES - (i5 - i4)))).astype(BF16)
        gw = lru_gate_w[l]
        wg_lru = jnp.concatenate([gw[:, 0], gw[:, 1]], axis=-1).astype(BF16)

        mod = _ada(cvec, w_ada, b_ada[l], l).reshape(8, 6, D)
        P, dtp = _inproj(xs, mod, w_main, w_dt, tm=tm_big, tn=1024, n_lat=n_lat, S=S)

        h_dirs, y_dirs = [], []
        for d in range(2):
            rev = d == 1
            h_dirs.append(_lru_dir(P, lru_conv_w[l], lru_conv_b[l].reshape(1, -1), wg_lru[d],
                                   lru_gate_b[l, d], lru_lambda[l, d].reshape(1, -1),
                                   rev=rev, B=B, ncc=ncc, nlc=nlc))
            pad = ((0, 0), (SSD_HEADS * d, LANES - SSD_HEADS * (d + 1)))
            sp = jnp.concatenate([jnp.pad(ssd_dt_bias[l, d][None, :], pad),
                                  jnp.pad(ssd_a_log[l, d][None, :], pad),
                                  jnp.pad(ssd_d[l][None, :], pad),
                                  jnp.zeros((5, LANES), F32)], axis=0)
            y_dirs.append(_ssd_dir(P, dtp, ssd_conv_w[l], ssd_conv_b[l].reshape(1, -1), sp,
                                   rev=rev, d=d, B=B, ncc=ncc, nlc=nlc, add_skip=(d == 0)))

        qr, kr = _rope(P, cos_t, sa_t, sb_t, tm=tm_mid)
        nw_att = att_norm_w[l].reshape(1, LANES)
        att = _attention(qr, kr, P, att_lambda[l], nw_att, None, B=B, S=S, CTX=CTX,
                         lam_init=lam_init, ctx_queries=False)
        if need_ctx:
            att = _attention(qr, kr, P, att_lambda[l], nw_att, att, B=B, S=S, CTX=CTX,
                             lam_init=lam_init, ctx_queries=True)

        m = _merge(h_dirs[0], h_dirs[1], y_dirs[0], y_dirs[1], att, P, w_branch[l].astype(BF16),
                   ssd_norm_w[l].reshape(1, -1), n_rows=n_rows, tm=tm_mid, tn=512)
        x1, h2, ids, wts, cnt = _outproj(m, w_out[l].astype(BF16), xs, mod, ln_g[l, 0].reshape(1, D),
                                         ln_b[l, 0].reshape(1, D), wr_t, rbias, n_rows=n_rows,
                                         tm=tm_small, n_lat=n_lat, S=S, alpha=alpha)

        tile_counts = cnt[:, ::LANES].T.astype(jnp.int32)
        src_tok, dst_tok, tile_e, tile_v = _moe_plan(ids[:4], tile_counts, MOE_TILE, tm_small)
        yg = _expert_ffn(h2, src_tok, dst_tok, tile_e, tile_v, _cast_bf16(w_gate_e, l),
                         _cast_bf16(w_up_e, l), _cast_bf16(w_down_e, l), tm=MOE_TILE,
                         n_out=2 * n_rows + MOE_TILE)
        xs = _final(x1, yg, wts[:2].T, mod, ln_g[l, 1].reshape(1, D), ln_b[l, 1].reshape(1, D),
                    n_rows=n_rows, tm=tm_small, n_lat=n_lat, S=S, alpha=alpha)

    return xs[:n_lat].reshape(B, S, D)
```

```python
import functools
import math

import jax
import jax.numpy as jnp
from jax import lax
from jax.experimental import pallas as pl
from jax.experimental.pallas import tpu as pltpu

F32 = jnp.float32
BF16 = jnp.bfloat16

GRID_W = 64
ATT_HEADS = 8
ATT_QK_DIM = 64
ATT_V_DIM = 128
ROPE_THETA = 10000.0
SSD_HEADS = 16
SSD_HEAD_DIM = 64
SSD_GROUPS = 4
SSD_STATE = 128
SSD_WIDTH = SSD_HEADS * SSD_HEAD_DIM
LRU_WIDTH = 1024
LRU_BLOCKS = 8
LRU_BLOCK_DIM = LRU_WIDTH // LRU_BLOCKS
LRU_C = 8.0
N_EXPERTS = 16
N_EXPERT_GROUPS = 4
EXPERTS_PER_GROUP = 4
LN_EPS = 1e-6

LANES = 128
SLAB = 8
SEQ_CHUNK = 256
HALO = 16
MOE_TILE = 256
NEG_BIG = -1e30
VMEM_LIMIT = 56 * 2 ** 20

COL_XBC, COL_LRU_X, COL_LRU_G, COL_Z, COL_Q, COL_K, COL_V, COL_GATES = 0, 2, 3, 4, 5, 6, 7, 8


def _cparams(*sem):
    return pltpu.CompilerParams(dimension_semantics=sem, vmem_limit_bytes=VMEM_LIMIT)


def _dot(a, b):
    return jnp.dot(a, b, preferred_element_type=F32)


def _dot_nt(a, b):
    return lax.dot_general(a, b, (((1,), (1,)), ((), ())), preferred_element_type=F32)


def _split3(v):
    hi = v.astype(BF16)
    r = v - hi.astype(F32)
    mid = r.astype(BF16)
    lo = (r - mid.astype(F32)).astype(BF16)
    return hi, mid, lo


def _split2(v):
    hi = v.astype(BF16)
    lo = (v - hi.astype(F32)).astype(BF16)
    return hi, lo


def _store_slabs(ref, x, tm):
    half = SLAB * LANES
    for k in range(SLAB):
        lo = lax.bitcast_convert_type(x[:, k * LANES:(k + 1) * LANES].astype(BF16).astype(F32), jnp.uint32)
        hi = lax.bitcast_convert_type(
            x[:, half + k * LANES:half + (k + 1) * LANES].astype(BF16).astype(F32), jnp.uint32)
        ref[pl.ds(k, tm, stride=SLAB), :] = hi | (lo >> 16)


def _load_slab_piece(u):
    lo = lax.bitcast_convert_type(u << 16, F32)
    hi = lax.bitcast_convert_type(u & jnp.uint32(0xFFFF0000), F32)
    return lo, hi


def _ln(x):
    mu = jnp.mean(x, axis=-1, keepdims=True)
    xc = x - mu
    var = jnp.mean(xc * xc, axis=-1, keepdims=True)
    return xc * lax.rsqrt(var + LN_EPS)


def _sigmoid(x):
    return jax.nn.sigmoid(x)


def _softplus(x):
    return jnp.maximum(x, 0.0) + jnp.log1p(jnp.exp(-jnp.abs(x)))


def _cast_kernel(w_ref, o_ref):
    o_ref[...] = w_ref[0].astype(o_ref.dtype)


def _cast_bf16(w, l):
    _, E, R, C = w.shape
    return pl.pallas_call(
        _cast_kernel,
        grid=(E,),
        in_specs=[pl.BlockSpec((1, 1, R, C), lambda e: (l, e, 0, 0))],
        out_specs=pl.BlockSpec((1, R, C), lambda e: (e, 0, 0)),
        out_shape=jax.ShapeDtypeStruct((E, R, C), BF16),
        compiler_params=_cparams("parallel"),
        name="cast_bf16",
    )(w)


def _ada_kernel(c_ref, w_ref, b_ref, o_ref):
    c = c_ref[...]
    a = c * _sigmoid(c)
    a_hi, a_lo = _split2(a)
    w_hi, w_lo = _split2(w_ref[0])
    o_ref[...] = _dot(a_hi, w_hi) + _dot(a_lo, w_hi) + _dot(a_hi, w_lo) + b_ref[...]


def _ada(cvec, w, b, l):
    R, D = cvec.shape
    N = w.shape[2]
    tn = 1024
    return pl.pallas_call(
        _ada_kernel,
        grid=(N // tn,),
        in_specs=[pl.BlockSpec((R, D), lambda j: (0, 0)),
                  pl.BlockSpec((1, D, tn), lambda j: (l, 0, j)),
                  pl.BlockSpec((1, tn), lambda j: (0, j))],
        out_specs=pl.BlockSpec((R, tn), lambda j: (0, j)),
        out_shape=jax.ShapeDtypeStruct((R, N), F32),
        compiler_params=_cparams("arbitrary"),
        name="ada",
    )(cvec, w, b.reshape(1, N))


def _inproj_kernel(x_ref, mod_ref, w_ref, wdt_ref, o_ref, dt_ref, h_ref, *, tm, rb):
    @pl.when(pl.program_id(1) == 0)
    def _():
        sh = mod_ref[0, 0:1, :]
        sc = mod_ref[0, 1:2, :]

        def body(r, carry):
            rows = pl.ds(pl.multiple_of(r * rb, rb), rb)
            h = _ln(x_ref[rows, :]) * (1.0 + sc) + sh
            hb = h.astype(BF16)
            h_ref[rows, :] = hb
            dt_ref[rows, :] = _dot(hb, wdt_ref[...])
            return carry

        lax.fori_loop(0, tm // rb, body, 0)

    o_ref[...] = _dot(h_ref[...], w_ref[...]).astype(o_ref.dtype)


def _inproj(x, mod, w_main, w_dt, *, tm, tn, n_lat, S):
    M, D = x.shape
    N = w_main.shape[1]

    def midx(i):
        return jnp.where(i * tm < n_lat, 1 + (i * tm) // S, 0)

    return pl.pallas_call(
        functools.partial(_inproj_kernel, tm=tm, rb=min(tm, 256)),
        grid=(M // tm, N // tn),
        in_specs=[pl.BlockSpec((tm, D), lambda i, j: (i, 0)),
                  pl.BlockSpec((1, 6, D), lambda i, j: (midx(i), 0, 0)),
                  pl.BlockSpec((D, tn), lambda i, j: (0, j)),
                  pl.BlockSpec((D, LANES), lambda i, j: (0, 0))],
        out_specs=[pl.BlockSpec((tm, tn), lambda i, j: (i, j)),
                   pl.BlockSpec((tm, LANES), lambda i, j: (i, 0))],
        out_shape=[jax.ShapeDtypeStruct((M, N), BF16), jax.ShapeDtypeStruct((M, LANES), F32)],
        scratch_shapes=[pltpu.VMEM((tm, D), BF16)],
        compiler_params=_cparams("parallel", "arbitrary"),
        name="inproj",
    )(x, mod, w_main, w_dt)


def _chunk_of_step(i, ncc, nlc, rev):
    if not rev:
        return i
    return jnp.where(i < ncc, ncc - 1 - i, ncc + nlc - 1 - (i - ncc))


def _block_of_chunk(b, c, B, ncc, nlc):
    return jnp.where(c < ncc, B * nlc + b * ncc + c, b * nlc + (c - ncc))


def _seq_specs(width, col, *, B, ncc, nlc, rev, n_rows):
    hb = SEQ_CHUNK // HALO
    last = n_rows // HALO - 1

    def blk(b, i):
        return _block_of_chunk(b, _chunk_of_step(i, ncc, nlc, rev), B, ncc, nlc)

    main = pl.BlockSpec((SEQ_CHUNK, width), lambda b, i: (blk(b, i), col))
    prev = pl.BlockSpec((HALO, width), lambda b, i: (jnp.maximum(blk(b, i) * hb - 1, 0), col))
    nxt = pl.BlockSpec((HALO, width), lambda b, i: (jnp.minimum((blk(b, i) + 1) * hb, last), col))
    return blk, main, prev, nxt


def _dwconv_chunk(x_ref, xp_ref, xn_ref, cw_ref, cb_ref, c, ncc, nch):
    T = SEQ_CHUNK
    x = x_ref[...].astype(F32)
    has_prev = jnp.where((c == 0) | (c == ncc), 0.0, 1.0)
    has_next = jnp.where((c == ncc - 1) | (c == nch - 1), 0.0, 1.0)
    p6 = xp_ref[HALO - 2:HALO - 1, :].astype(F32) * has_prev
    p7 = xp_ref[HALO - 1:HALO, :].astype(F32) * has_prev
    n0 = xn_ref[0:1, :].astype(F32) * has_next
    row = lax.broadcasted_iota(jnp.int32, (T, 1), 0)
    xm1 = jnp.where(row == 0, p7, pltpu.roll(x, 1, 0))
    xm2 = jnp.where(row == 0, p6, jnp.where(row == 1, p7, pltpu.roll(x, 2, 0)))
    xp1 = jnp.where(row == T - 1, n0, pltpu.roll(x, T - 1, 0))
    return (cw_ref[0:1, :] * xm2 + cw_ref[1:2, :] * xm1 + cw_ref[2:3, :] * x
            + cw_ref[3:4, :] * xp1 + cb_ref[...])


def _linear_scan(a, b, rev):
    T = a.shape[0]
    row = lax.broadcasted_iota(jnp.int32, (T, 1), 0)
    s = 1
    while s < T:
        if s % 8:
            valid = (row < T - s) if rev else (row >= s)
            shift = T - s if rev else s
            a_p = jnp.where(valid, pltpu.roll(a, shift, 0), 1.0)
            b_p = jnp.where(valid, pltpu.roll(b, shift, 0), 0.0)
            b = a * b_p + b
            a = a * a_p
        elif rev:
            b = jnp.concatenate([a[:T - s] * b[s:] + b[:T - s], b[T - s:]], axis=0)
            a = jnp.concatenate([a[:T - s] * a[s:], a[T - s:]], axis=0)
        else:
            b = jnp.concatenate([b[:s], a[s:] * b[:T - s] + b[s:]], axis=0)
            a = jnp.concatenate([a[:s], a[s:] * a[:T - s]], axis=0)
        s *= 2
    return a, b


def _lru_kernel(x_ref, xp_ref, xn_ref, cw_ref, cb_ref, wg_ref, gb_ref, lam_ref, o_ref, carry_ref,
                *, rev, ncc, nlc):
    T = SEQ_CHUNK
    i = pl.program_id(1)
    c = _chunk_of_step(i, ncc, nlc, rev)

    @pl.when(i == 0)
    def _():
        carry_ref[...] = jnp.zeros_like(carry_ref)

    xc = _dwconv_chunk(x_ref, xp_ref, xn_ref, cw_ref, cb_ref, c, ncc, ncc + nlc)
    sp = _softplus(-lam_ref[...])
    for n in range(LRU_BLOCKS):
        sl = slice(n * LRU_BLOCK_DIM, (n + 1) * LRU_BLOCK_DIM)
        xb = xc[:, sl]
        gl = _dot(xb.astype(BF16), wg_ref[n])
        r_gate = _sigmoid(gl[:, :LRU_BLOCK_DIM] + gb_ref[0:1, sl])
        i_gate = _sigmoid(gl[:, LRU_BLOCK_DIM:] + gb_ref[1:2, sl])
        log_a = -LRU_C * r_gate * sp[:, sl]
        a = jnp.exp(log_a)
        om = 1.0 - a * a
        root = jnp.where(om > 0.0, om * lax.rsqrt(om), 0.0)
        bb = root * (i_gate * xb)
        a_s, b_s = _linear_scan(a, bb, rev)
        h = b_s + a_s * carry_ref[0:1, sl]
        o_ref[:, sl] = h.astype(o_ref.dtype)
        carry_ref[0:1, sl] = h[0:1, :] if rev else h[T - 1:T, :]


def _lru_dir(P, conv_w, conv_b, wg, gb, lam, *, rev, B, ncc, nlc):
    M = P.shape[0]
    W = LRU_WIDTH
    blk, main, prev, nxt = _seq_specs(W, COL_LRU_X, B=B, ncc=ncc, nlc=nlc, rev=rev, n_rows=M)
    full2 = lambda shape: pl.BlockSpec(shape, lambda b, i: (0, 0))
    return pl.pallas_call(
        functools.partial(_lru_kernel, rev=rev, ncc=ncc, nlc=nlc),
        grid=(B, ncc + nlc),
        in_specs=[main, prev, nxt, full2((4, W)), full2((1, W)),
                  pl.BlockSpec((LRU_BLOCKS, LRU_BLOCK_DIM, 2 * LRU_BLOCK_DIM), lambda b, i: (0, 0, 0)),
                  full2((2, W)), full2((1, W))],
        out_specs=pl.BlockSpec((SEQ_CHUNK, W), lambda b, i: (blk(b, i), 0)),
        out_shape=jax.ShapeDtypeStruct((M, W), BF16),
        scratch_shapes=[pltpu.VMEM((8, W), F32)],
        compiler_params=_cparams("parallel", "arbitrary"),
        name="lru_bwd" if rev else "lru_fwd",
    )(P, P, P, conv_w, conv_b, wg, gb, lam)


def _ssd_kernel(x_ref, xp_ref, xn_ref, dt_ref, cw_ref, cb_ref, sp_ref, o_ref, st_ref,
                *, rev, d, ncc, nlc, add_skip):
    T = SEQ_CHUNK
    N = SSD_STATE
    HPG = SSD_HEADS // SSD_GROUPS
    GW = HPG * SSD_HEAD_DIM
    i = pl.program_id(1)
    c = _chunk_of_step(i, ncc, nlc, rev)

    @pl.when(i == 0)
    def _():
        st_ref[...] = jnp.zeros_like(st_ref)

    xc = _dwconv_chunk(x_ref, xp_ref, xn_ref, cw_ref, cb_ref, c, ncc, ncc + nlc)
    xbc = xc * _sigmoid(xc)
    xs = xbc[:, :SSD_WIDTH]
    Bm = xbc[:, SSD_WIDTH:SSD_WIDTH + SSD_GROUPS * N]
    Cm = xbc[:, SSD_WIDTH + SSD_GROUPS * N:]

    lane = lax.broadcasted_iota(jnp.int32, (1, LANES), 1)
    in_dir = (lane >= SSD_HEADS * d) & (lane < SSD_HEADS * (d + 1))
    dt = jnp.where(in_dir, _softplus(dt_ref[...] + sp_ref[0:1, :]), 0.0)
    a_neg = jnp.where(in_dir, -jnp.exp(sp_ref[1:2, :]), 0.0)
    a = dt * a_neg

    r_i = lax.broadcasted_iota(jnp.int32, (T, T), 0)
    c_i = lax.broadcasted_iota(jnp.int32, (T, T), 1)
    tri = (c_i >= r_i) if rev else (c_i <= r_i)
    tri_b = jnp.where(tri, 1.0, 0.0).astype(BF16)
    cs = sum(_dot(tri_b, p) for p in _split3(a))
    e_r = lax.broadcasted_iota(jnp.int32, (LANES, LANES), 0)
    e_c = lax.broadcasted_iota(jnp.int32, (LANES, LANES), 1)
    eye = jnp.where(e_r == e_c, 1.0, 0.0).astype(BF16)
    cs_row = sum(_dot_nt(eye, p) for p in _split3(cs))
    cs_end = cs[0:1, :] if rev else cs[T - 1:T, :]

    x_r = lax.broadcasted_iota(jnp.int32, (LANES, SSD_WIDTH), 0)
    x_c = lax.broadcasted_iota(jnp.int32, (LANES, SSD_WIDTH), 1)
    expand_m = jnp.where(x_r == SSD_HEADS * d + x_c // SSD_HEAD_DIM, 1.0, 0.0).astype(BF16)

    def expand(v):
        return sum(_dot(p, expand_m) for p in _split3(v))

    dt_e = expand(dt)
    dec_out_e = expand(jnp.exp(cs))
    dec_st_e = expand(jnp.exp(cs_end - cs))
    row8 = lax.broadcasted_iota(jnp.int32, (8, LANES), 0)
    small = jnp.where(row8 == 0, jnp.exp(cs_end), jnp.where(row8 == 1, sp_ref[2:3, :], 0.0))
    small_e = expand(small)
    dec_ch_e = small_e[0:1, :]
    dskip_e = small_e[1:2, :]

    xdt = xs * dt_e
    xdt_b = xdt.astype(BF16)
    z_st = (xdt * dec_st_e).astype(BF16)
    lo_half = lax.broadcasted_iota(jnp.int32, (1, LANES), 1) < SSD_HEAD_DIM

    for g in range(SSD_GROUPS):
        Bg = Bm[:, g * N:(g + 1) * N].astype(BF16)
        Cg = Cm[:, g * N:(g + 1) * N].astype(BF16)
        CB = _dot_nt(Cg, Bg)
        gs = slice(g * GW, (g + 1) * GW)
        Sg = st_ref[:, gs]
        y_off = _dot(Cg, Sg.astype(BF16)) * dec_out_e[:, gs]
        for p in range(HPG // 2):
            col = g * GW + p * LANES
            x_pair = xdt_b[:, col:col + LANES]
            y_pair = y_off[:, p * LANES:(p + 1) * LANES]
            for q in range(2):
                hh = SSD_HEADS * d + g * HPG + 2 * p + q
                L = jnp.exp(jnp.where(tri, cs[:, hh:hh + 1] - cs_row[hh:hh + 1, :], NEG_BIG))
                Wm = (CB * L).astype(BF16)
                xm = jnp.where(lo_half if q == 0 else jnp.logical_not(lo_half), x_pair,
                               jnp.zeros_like(x_pair))
                y_pair = y_pair + _dot(Wm, xm)
            if add_skip:
                y_pair = y_pair + dskip_e[:, col:col + LANES] * xs[:, col:col + LANES]
            o_ref[:, col:col + LANES] = y_pair.astype(o_ref.dtype)
        BgT = _dot_nt(eye, Bg).astype(BF16)
        st_ref[:, gs] = dec_ch_e[:, gs] * Sg + _dot(BgT, z_st[:, gs])


def _ssd_dir(P, dtp, conv_w, conv_b, sp, *, rev, d, B, ncc, nlc, add_skip):
    M = P.shape[0]
    XW = SSD_WIDTH + 2 * SSD_GROUPS * SSD_STATE
    blk, main, prev, nxt = _seq_specs(XW, COL_XBC, B=B, ncc=ncc, nlc=nlc, rev=rev, n_rows=M)
    full2 = lambda shape: pl.BlockSpec(shape, lambda b, i: (0, 0))
    return pl.pallas_call(
        functools.partial(_ssd_kernel, rev=rev, d=d, ncc=ncc, nlc=nlc, add_skip=add_skip),
        grid=(B, ncc + nlc),
        in_specs=[main, prev, nxt,
                  pl.BlockSpec((SEQ_CHUNK, LANES), lambda b, i: (blk(b, i), 0)),
                  full2((4, XW)), full2((1, XW)), full2((8, LANES))],
        out_specs=pl.BlockSpec((SEQ_CHUNK, SSD_WIDTH), lambda b, i: (blk(b, i), 0)),
        out_shape=jax.ShapeDtypeStruct((M, SSD_WIDTH), BF16),
        scratch_shapes=[pltpu.VMEM((SSD_STATE, SSD_WIDTH), F32)],
        compiler_params=_cparams("parallel", "arbitrary"),
        name="ssd_bwd" if rev else "ssd_fwd",
    )(P, P, P, dtp, conv_w, conv_b, sp)


def _rope_kernel(q_ref, k_ref, c_ref, sa_ref, sb_ref, qo_ref, ko_ref, *, scale):
    W = q_ref.shape[1]
    reps = W // LANES
    cos = jnp.concatenate([c_ref[...]] * reps, axis=1)
    sa = jnp.concatenate([sa_ref[...]] * reps, axis=1)
    sb = jnp.concatenate([sb_ref[...]] * reps, axis=1)
    quarter = ATT_QK_DIM // 4

    def rot(x):
        return x * cos + pltpu.roll(x, W - quarter, 1) * sa + pltpu.roll(x, quarter, 1) * sb

    qo_ref[...] = (rot(q_ref[...].astype(F32)) * scale).astype(qo_ref.dtype)
    ko_ref[...] = rot(k_ref[...].astype(F32)).astype(ko_ref.dtype)


def _rope(P, cos, sa, sb, *, tm):
    M = P.shape[0]
    W = ATT_HEADS * 2 * ATT_QK_DIM
    tab = pl.BlockSpec((tm, LANES), lambda i: (i, 0))
    return pl.pallas_call(
        functools.partial(_rope_kernel, scale=ATT_QK_DIM ** -0.5),
        grid=(M // tm,),
        in_specs=[pl.BlockSpec((tm, W), lambda i: (i, COL_Q)),
                  pl.BlockSpec((tm, W), lambda i: (i, COL_K)), tab, tab, tab],
        out_specs=[pl.BlockSpec((tm, W), lambda i: (i, 0))] * 2,
        out_shape=[jax.ShapeDtypeStruct((M, W), BF16)] * 2,
        compiler_params=_cparams("parallel"),
        name="rope",
    )(P, P, cos, sa, sb)


def _attn_kernel(*refs, tq, tk, n_ctx, n_lat, lam_init):
    if n_lat:
        (q_ref, kc_ref, vc_ref, kl_ref, vl_ref, lv_ref, nw_ref, o_ref, m_ref, l_ref, acc_ref) = refs
    else:
        (q_ref, kc_ref, vc_ref, lv_ref, nw_ref, o_ref, m_ref, l_ref, acc_ref) = refs
        kl_ref = vl_ref = None

    q = q_ref[...]
    lane = lax.broadcasted_iota(jnp.int32, (1, LANES), 1)
    zero = jnp.zeros_like(q)
    qq = jnp.concatenate([jnp.where(lane < ATT_QK_DIM, q, zero),
                          jnp.where(lane >= ATT_QK_DIM, q, zero)], axis=0)

    m_ref[...] = jnp.full(m_ref.shape, NEG_BIG, F32)
    l_ref[...] = jnp.zeros_like(l_ref)
    acc_ref[...] = jnp.zeros_like(acc_ref)

    def lane_blocks(s):
        return [s[:, j * LANES:(j + 1) * LANES] for j in range(s.shape[1] // LANES)]

    def over_keys(fn):
        tc = min(n_ctx, tk)
        for j in range(n_ctx // tc):
            fn(kc_ref[j * tc:(j + 1) * tc, :], vc_ref[j * tc:(j + 1) * tc, :])
        if n_lat:
            def body(j, carry):
                rows = pl.ds(pl.multiple_of(j * tk, tk), tk)
                fn(kl_ref[rows, :], vl_ref[rows, :])
                return carry
            lax.fori_loop(0, n_lat // tk, body, 0, unroll=2)

    def step(kc, vc):
        blocks = lane_blocks(_dot_nt(qq, kc))
        m_prev = m_ref[...]
        smax = functools.reduce(jnp.maximum, blocks)
        m_new = jnp.maximum(m_prev, jnp.max(smax, axis=1, keepdims=True))
        alpha = jnp.exp(m_prev - m_new)
        p = [jnp.exp(sb - m_new) for sb in blocks]
        l_ref[...] = alpha * l_ref[...] + functools.reduce(jnp.add, p)
        pb = jnp.concatenate([x.astype(BF16) for x in p], axis=1)
        acc_ref[...] = alpha * acc_ref[...] + _dot(pb, vc)
        m_ref[...] = m_new

    over_keys(step)
    l_ref[...] = jnp.broadcast_to(jnp.sum(l_ref[...], axis=1, keepdims=True), l_ref.shape)

    lv = lv_ref[...]
    s01 = jnp.sum(lv[0:1, :] * lv[1:2, :], axis=1, keepdims=True)
    s23 = jnp.sum(lv[2:3, :] * lv[3:4, :], axis=1, keepdims=True)
    lam = jnp.exp(s01) - jnp.exp(s23) + lam_init
    acc = acc_ref[...]
    l = l_ref[...]
    o = acc[:tq, :] / l[:tq, :] - lam * (acc[tq:, :] / l[tq:, :])
    o = o * lax.rsqrt(jnp.mean(o * o, axis=-1, keepdims=True) + LN_EPS) * nw_ref[...]
    o_ref[...] = (o * (1.0 - lam_init)).astype(o_ref.dtype)


def _attention(qr, kr, P, lam_vecs, norm_w, out_prev, *, B, S, CTX, lam_init, ctx_queries):
    M = qr.shape[0]
    H = ATT_HEADS
    tk = 512
    assert S % tk == 0 and (CTX % tk == 0 or tk % CTX == 0)
    vcol = COL_V * (1024 // LANES)
    n_lat_blk = (B * S) // CTX
    if ctx_queries:
        tq = min(CTX, 256)
        nq = CTX // tq
        q_spec = pl.BlockSpec((tq, LANES), lambda b, h, i: ((B * S + b * CTX) // tq + i, h))
    else:
        tq = min(S, 1024)
        nq = S // tq
        q_spec = pl.BlockSpec((tq, LANES), lambda b, h, i: (b * nq + i, h))
    in_specs = [q_spec,
                pl.BlockSpec((CTX, LANES), lambda b, h, i: (n_lat_blk + b, h)),
                pl.BlockSpec((CTX, LANES), lambda b, h, i: (n_lat_blk + b, vcol + h))]
    args = [qr, kr, P]
    if not ctx_queries:
        in_specs += [pl.BlockSpec((S, LANES), lambda b, h, i: (b, h)),
                     pl.BlockSpec((S, LANES), lambda b, h, i: (b, vcol + h))]
        args += [kr, P]
    in_specs += [pl.BlockSpec((4, ATT_QK_DIM), lambda b, h, i: (0, 0)),
                 pl.BlockSpec((1, LANES), lambda b, h, i: (0, 0))]
    args += [lam_vecs, norm_w]
    kwargs = {}
    if out_prev is not None:
        in_specs.append(pl.BlockSpec(memory_space=pl.ANY))
        args.append(out_prev)
        kwargs["input_output_aliases"] = {len(args) - 1: 0}

    kernel = functools.partial(_attn_kernel, tq=tq, tk=tk, n_ctx=CTX, n_lat=0 if ctx_queries else S,
                               lam_init=lam_init)
    if out_prev is not None:
        base = kernel
        kernel = lambda *refs: base(*refs[:len(args) - 1], *refs[len(args):])
    return pl.pallas_call(
        kernel,
        grid=(B, H, nq),
        in_specs=in_specs,
        out_specs=q_spec,
        out_shape=jax.ShapeDtypeStruct((M, H * ATT_V_DIM), BF16),
        scratch_shapes=[pltpu.VMEM((2 * tq, LANES), F32)] * 3,
        compiler_params=_cparams("parallel", "parallel", "arbitrary"),
        name="attn_ctx" if ctx_queries else "attn_lat",
        **kwargs,
    )(*args)


def _merge_kernel(hf_ref, hb_ref, gl_ref, yf_ref, yb_ref, z_ref, att_ref, g0_ref, g1_ref, g2_ref,
                  wb_ref, nw_ref, o_ref, br_ref):
    @pl.when(pl.program_id(1) == 0)
    def _():
        y = hf_ref[...].astype(F32) + hb_ref[...].astype(F32)
        g = gl_ref[...].astype(F32)
        gelu = 0.5 * g * (1.0 + jnp.tanh(0.7978845608028654 * (g + 0.044715 * (g * g * g))))
        br_ref[0] = (y * gelu).astype(BF16)
        ys = yf_ref[...].astype(F32) + yb_ref[...].astype(F32)
        zz = z_ref[...].astype(F32)
        t = ys * (zz * _sigmoid(zz))
        t = t * lax.rsqrt(jnp.mean(t * t, axis=-1, keepdims=True) + LN_EPS) * nw_ref[...]
        br_ref[1] = t.astype(BF16)
        br_ref[2] = att_ref[...]

    acc = _sigmoid(g0_ref[...].astype(F32)) * _dot(br_ref[0], wb_ref[0])
    acc = acc + _sigmoid(g1_ref[...].astype(F32)) * _dot(br_ref[1], wb_ref[1])
    acc = acc + _sigmoid(g2_ref[...].astype(F32)) * _dot(br_ref[2], wb_ref[2])
    o_ref[...] = acc.astype(o_ref.dtype)


def _merge(hf, hb, yf, yb, att, P, wb, ssd_norm_w, *, n_rows, tm, tn):
    M = P.shape[0]
    D = wb.shape[2]
    BW = wb.shape[1]
    row = lambda col: pl.BlockSpec((tm, BW), lambda i, j: (i, col))
    gcol = COL_GATES * 1024 // tn
    gate = lambda k: pl.BlockSpec((tm, tn), lambda i, j: (i, gcol + k * (D // tn) + j))
    return pl.pallas_call(
        _merge_kernel,
        grid=(n_rows // tm, D // tn),
        in_specs=[row(0), row(0), row(COL_LRU_G), row(0), row(0), row(COL_Z), row(0),
                  gate(0), gate(1), gate(2),
                  pl.BlockSpec((3, BW, tn), lambda i, j: (0, 0, j)),
                  pl.BlockSpec((1, BW), lambda i, j: (0, 0))],
        out_specs=pl.BlockSpec((tm, tn), lambda i, j: (i, j)),
        out_shape=jax.ShapeDtypeStruct((M, D), BF16),
        scratch_shapes=[pltpu.VMEM((3, tm, BW), BF16)],
        compiler_params=_cparams("parallel", "arbitrary"),
        name="merge",
    )(hf, hb, P, yf, yb, P, att, P, P, P, wb, ssd_norm_w)


def _route(logits, rbias):
    G, E = N_EXPERT_GROUPS, EXPERTS_PER_GROUP
    mx = jnp.max(logits, axis=0, keepdims=True)
    ex = jnp.exp(logits - mx)
    sc = ex / jnp.sum(ex, axis=0, keepdims=True)
    sel = sc + rbias
    srow = [sel[r:r + 1, :] for r in range(G * E)]
    prow = [sc[r:r + 1, :] for r in range(G * E)]
    gscore = []
    for g in range(G):
        v = srow[g * E:(g + 1) * E]
        best = v[0] + v[1]
        for a in range(E):
            for b in range(a + 1, E):
                if (a, b) != (0, 1):
                    best = jnp.maximum(best, v[a] + v[b])
        gscore.append(best)
    gmax = functools.reduce(jnp.maximum, gscore)
    gidx = jnp.full(gmax.shape, G - 1, jnp.int32)
    for g in range(G - 2, -1, -1):
        gidx = jnp.where(gscore[g] == gmax, g, gidx)

    def pick(rows, k):
        out = rows[(G - 1) * E + k]
        for g in range(G - 2, -1, -1):
            out = jnp.where(gidx == g, rows[g * E + k], out)
        return out

    v = [pick(srow, k) for k in range(E)]
    p = [pick(prow, k) for k in range(E)]
    chosen = []
    for k in range(E):
        rank = jnp.zeros(gmax.shape, jnp.int32)
        for j in range(E):
            if j == k:
                continue
            ahead = (v[j] > v[k]) | ((v[j] == v[k]) & (j < k)) if j < k else (v[j] > v[k])
            rank = rank + jnp.where(ahead, 1, 0)
        chosen.append(rank < 2)
    first = jnp.full(gmax.shape, E, jnp.int32)
    second = jnp.full(gmax.shape, -1, jnp.int32)
    for k in range(E):
        first = jnp.minimum(first, jnp.where(chosen[k], k, E))
        second = jnp.maximum(second, jnp.where(chosen[k], k, -1))
    p_first = jnp.zeros(gmax.shape, F32)
    p_second = jnp.zeros(gmax.shape, F32)
    for k in range(E):
        p_first = p_first + jnp.where(first == k, p[k], 0.0)
        p_second = p_second + jnp.where(second == k, p[k], 0.0)
    tot = p_first + p_second
    return gidx * E + first, gidx * E + second, p_first / tot, p_second / tot


def _outproj_kernel(m_ref, w_ref, x_ref, mod_ref, lng_ref, lnb_ref, wr_ref, rb_ref,
                    x1_ref, h2_ref, ids_ref, wts_ref, cnt_ref, *, tm, alpha):
    o = _dot(m_ref[...], w_ref[...])
    g1 = mod_ref[0, 2:3, :]
    sh2 = mod_ref[0, 3:4, :]
    sc2 = mod_ref[0, 4:5, :]
    x1 = _ln(alpha * x_ref[...] + g1 * o) * lng_ref[...] + lnb_ref[...]
    x1_ref[...] = x1
    h2 = _ln(x1) * (1.0 + sc2) + sh2
    _store_slabs(h2_ref, h2, tm)
    h_hi, h_lo = _split2(h2)
    w_hi, w_lo = _split2(wr_ref[...])
    logits = _dot_nt(w_hi, h_hi) + _dot_nt(w_lo, h_hi) + _dot_nt(w_hi, h_lo)
    e0, e1, w0, w1 = _route(logits, rb_ref[...])
    erow = lax.broadcasted_iota(jnp.int32, (N_EXPERTS, tm), 0)
    hit0 = erow == e0
    hit1 = erow == e1
    cnt = jnp.where(hit0, 1.0, 0.0) + jnp.where(hit1, 1.0, 0.0)
    t_r = lax.broadcasted_iota(jnp.int32, (tm, tm), 0)
    t_c = lax.broadcasted_iota(jnp.int32, (tm, tm), 1)
    before = jnp.where(t_r < t_c, 1.0, 0.0).astype(BF16)
    pre = _dot(cnt.astype(BF16), before)
    r0 = jnp.sum(jnp.where(hit0, pre, 0.0), axis=0, keepdims=True).astype(jnp.int32)
    r1 = jnp.sum(jnp.where(hit1, pre, 0.0), axis=0, keepdims=True).astype(jnp.int32)
    cnt_ref[...] = jnp.broadcast_to(jnp.sum(cnt, axis=1, keepdims=True), cnt_ref.shape)
    row = lax.broadcasted_iota(jnp.int32, (8, tm), 0)
    ids_ref[...] = jnp.where(row == 0, e0, jnp.where(row == 1, e1,
                             jnp.where(row == 2, r0, jnp.where(row == 3, r1, 0))))
    wts_ref[...] = jnp.where(row == 0, w0, jnp.where(row == 1, w1, 0.0))


def _outproj(m, w_out, x, mod, ln_g, ln_b, wr_t, rbias, *, n_rows, tm, n_lat, S, alpha):
    M, D = x.shape

    def midx(i):
        return jnp.where(i * tm < n_lat, 1 + (i * tm) // S, 0)

    vec = pl.BlockSpec((1, D), lambda i: (0, 0))
    return pl.pallas_call(
        functools.partial(_outproj_kernel, tm=tm, alpha=alpha),
        grid=(n_rows // tm,),
        in_specs=[pl.BlockSpec((tm, D), lambda i: (i, 0)),
                  pl.BlockSpec((D, D), lambda i: (0, 0)),
                  pl.BlockSpec((tm, D), lambda i: (i, 0)),
                  pl.BlockSpec((1, 6, D), lambda i: (midx(i), 0, 0)),
                  vec, vec,
                  pl.BlockSpec((N_EXPERTS, D), lambda i: (0, 0)),
                  pl.BlockSpec((N_EXPERTS, 1), lambda i: (0, 0))],
        out_specs=[pl.BlockSpec((tm, D), lambda i: (i, 0)),
                   pl.BlockSpec((tm * SLAB, LANES), lambda i: (i, 0)),
                   pl.BlockSpec((8, tm), lambda i: (0, i)),
                   pl.BlockSpec((8, tm), lambda i: (0, i)),
                   pl.BlockSpec((N_EXPERTS, LANES), lambda i: (0, i))],
        out_shape=[jax.ShapeDtypeStruct((M, D), F32),
                   jax.ShapeDtypeStruct((n_rows * SLAB, LANES), jnp.uint32),
                   jax.ShapeDtypeStruct((8, n_rows), jnp.int32),
                   jax.ShapeDtypeStruct((8, n_rows), F32),
                   jax.ShapeDtypeStruct((N_EXPERTS, (n_rows // tm) * LANES), F32)],
        compiler_params=_cparams("parallel"),
        name="outproj",
    )(m, w_out, x, mod, ln_g, ln_b, wr_t, rbias)


def _ffn_kernel(te_ref, tv_ref, src_ref, dst_ref, h_hbm, wg_ref, wu_ref, wd_ref, y_hbm,
                xbuf, ybuf, xb_ref, gsem, ssem, *, tm):
    i = pl.program_id(0)
    n = pl.num_programs(0)
    slot = i % 2

    def gather_start(tile, s):
        def issue(r2, carry):
            for prio in range(2):
                r = 2 * r2 + prio
                tok = src_ref[tile * tm + r]
                pltpu.make_async_copy(h_hbm.at[pl.ds(pl.multiple_of(tok * SLAB, SLAB), SLAB)],
                                      xbuf.at[s, pl.ds(pl.multiple_of(r * SLAB, SLAB), SLAB)],
                                      gsem.at[s]).start(priority=prio)
            return carry
        lax.fori_loop(0, tm // 2, issue, 0)

    def gather_wait(s):
        pltpu.make_async_copy(h_hbm.at[pl.ds(0, tm * SLAB)], xbuf.at[s], gsem.at[s]).wait()

    def scatter_start(tile):
        def issue(r2, carry):
            for prio in range(2):
                r = 2 * r2 + prio
                tok = dst_ref[tile * tm + r]
                pltpu.make_async_copy(ybuf.at[pl.ds(pl.multiple_of(r * SLAB, SLAB), SLAB)],
                                      y_hbm.at[pl.ds(pl.multiple_of(tok * SLAB, SLAB), SLAB)],
                                      ssem).start(priority=prio)
            return carry
        lax.fori_loop(0, tm // 2, issue, 0)

    def scatter_wait():
        pltpu.make_async_copy(ybuf, y_hbm.at[pl.ds(0, tm * SLAB)], ssem).wait()

    @pl.when(i == 0)
    def _():
        gather_start(0, 0)

    @pl.when(tv_ref[i] != 0)
    def _():
        has_next = jnp.logical_and(i + 1 < n, tv_ref[jnp.minimum(i + 1, n - 1)] != 0)

        @pl.when(has_next)
        def _():
            gather_start(i + 1, 1 - slot)

        gather_wait(slot)
        half = SLAB * LANES
        for k in range(SLAB):
            lo, hi = _load_slab_piece(xbuf[slot, pl.ds(k, tm, stride=SLAB), :])
            xb_ref[:, k * LANES:(k + 1) * LANES] = lo.astype(BF16)
            xb_ref[:, half + k * LANES:half + (k + 1) * LANES] = hi.astype(BF16)
        x = xb_ref[...]
        g = _dot(x, wg_ref[0])
        u = _dot(x, wu_ref[0])
        act = (g * _sigmoid(g) * u).astype(BF16)
        y = _dot(act, wd_ref[0])

        @pl.when(i > 0)
        def _():
            scatter_wait()

        _store_slabs(ybuf, y, tm)
        scatter_start(i)

        @pl.when(jnp.logical_not(has_next))
        def _():
            scatter_wait()


def _expert_ffn(h2, src_tok, dst_tok, tile_e, tile_v, wg, wu, wd, *, tm, n_out):
    n_tiles = tile_e.shape[0]
    E, D, F = wg.shape
    return pl.pallas_call(
        functools.partial(_ffn_kernel, tm=tm),
        grid_spec=pltpu.PrefetchScalarGridSpec(
            num_scalar_prefetch=4,
            grid=(n_tiles,),
            in_specs=[pl.BlockSpec(memory_space=pl.ANY),
                      pl.BlockSpec((1, D, F), lambda i, te, tv, sr, ds: (te[i], 0, 0)),
                      pl.BlockSpec((1, D, F), lambda i, te, tv, sr, ds: (te[i], 0, 0)),
                      pl.BlockSpec((1, F, D), lambda i, te, tv, sr, ds: (te[i], 0, 0))],
            out_specs=pl.BlockSpec(memory_space=pl.ANY),
            scratch_shapes=[pltpu.VMEM((2, tm * SLAB, LANES), jnp.uint32),
                            pltpu.VMEM((tm * SLAB, LANES), jnp.uint32),
                            pltpu.VMEM((tm, D), BF16),
                            pltpu.SemaphoreType.DMA((2,)),
                            pltpu.SemaphoreType.DMA(())]),
        out_shape=jax.ShapeDtypeStruct((n_out * SLAB, LANES), jnp.uint32),
        compiler_params=_cparams("arbitrary"),
        name="expert_ffn",
    )(tile_e, tile_v, src_tok, dst_tok, h2, wg, wu, wd)


def _final_kernel(x_ref, y0_ref, y1_ref, w_ref, mod_ref, lng_ref, lnb_ref, o_ref, f_ref, *, tm, alpha):
    half = SLAB * LANES
    w0 = w_ref[:, 0:1]
    w1 = w_ref[:, 1:2]
    for k in range(SLAB):
        rows = pl.ds(k, tm, stride=SLAB)
        lo0, hi0 = _load_slab_piece(y0_ref[rows, :])
        lo1, hi1 = _load_slab_piece(y1_ref[rows, :])
        f_ref[:, k * LANES:(k + 1) * LANES] = w0 * lo0 + w1 * lo1
        f_ref[:, half + k * LANES:half + (k + 1) * LANES] = w0 * hi0 + w1 * hi1
    g2 = mod_ref[0, 5:6, :]
    o_ref[...] = _ln(alpha * x_ref[...] + g2 * f_ref[...]) * lng_ref[...] + lnb_ref[...]


def _final(x1, yg, wcol, mod, ln_g, ln_b, *, n_rows, tm, n_lat, S, alpha):
    D = x1.shape[1]

    def midx(i):
        return jnp.where(i * tm < n_lat, 1 + (i * tm) // S, 0)

    nt = n_rows // tm
    vec = pl.BlockSpec((1, D), lambda i: (0, 0))
    return pl.pallas_call(
        functools.partial(_final_kernel, tm=tm, alpha=alpha),
        grid=(nt,),
        in_specs=[pl.BlockSpec((tm, D), lambda i: (i, 0)),
                  pl.BlockSpec((tm * SLAB, LANES), lambda i: (i, 0)),
                  pl.BlockSpec((tm * SLAB, LANES), lambda i: (nt + i, 0)),
                  pl.BlockSpec((tm, 2), lambda i: (i, 0)),
                  pl.BlockSpec((1, 6, D), lambda i: (midx(i), 0, 0)),
                  vec, vec],
        out_specs=pl.BlockSpec((tm, D), lambda i: (i, 0)),
        out_shape=jax.ShapeDtypeStruct((n_rows, D), F32),
        scratch_shapes=[pltpu.VMEM((tm, D), F32)],
        compiler_params=_cparams("parallel"),
        name="final",
    )(x1, yg, yg, wcol, mod, ln_g, ln_b)


def _rope_tables(B, S, CTX):
    t = jnp.arange(S)
    row = (t // GRID_W).astype(F32)
    col = (t % GRID_W).astype(F32)
    axis_dim = ATT_QK_DIM // 2
    freqs = ROPE_THETA ** (-jnp.arange(0, axis_dim, 2, dtype=F32) / axis_dim)
    ang = jnp.stack([row[:, None] * freqs, col[:, None] * freqs], axis=1)
    cos, sin = jnp.cos(ang), jnp.sin(ang)
    zero = jnp.zeros_like(sin)
    c64 = jnp.concatenate([cos, cos], axis=-1).reshape(S, ATT_QK_DIM)
    sa64 = jnp.concatenate([-sin, zero], axis=-1).reshape(S, ATT_QK_DIM)
    sb64 = jnp.concatenate([zero, sin], axis=-1).reshape(S, ATT_QK_DIM)

    def full(tab, ctx_val):
        lat = jnp.tile(jnp.concatenate([tab, tab], axis=-1), (B, 1))
        ctx = jnp.full((B * CTX, LANES), ctx_val, F32)
        return jnp.concatenate([lat, ctx], axis=0)

    return full(c64, 1.0), full(sa64, 0.0), full(sb64, 0.0)


def _moe_plan(ids, tile_counts, tm, tm_route):
    n_tok = ids.shape[1]
    experts = jnp.arange(N_EXPERTS, dtype=jnp.int32)
    counts = jnp.sum(tile_counts, axis=0)
    padded = ((counts + tm - 1) // tm) * tm
    ends = jnp.cumsum(padded)
    starts = ends - padded
    tile_base = starts[None, :] + jnp.cumsum(tile_counts, axis=0) - tile_counts
    base_tok = jnp.repeat(tile_base, tm_route, axis=0)
    pos = jnp.stack([jnp.sum(jnp.where(experts[None, :] == ids[k][:, None], base_tok, 0), axis=1)
                     + ids[2 + k] for k in range(2)]).reshape(-1).astype(jnp.int32)
    n_slots = 2 * n_tok + N_EXPERTS * tm
    slot_of = jnp.full((n_slots,), -1, jnp.int32).at[pos].set(
        jnp.arange(2 * n_tok, dtype=jnp.int32), unique_indices=True, mode="promise_in_bounds")
    real = slot_of >= 0
    src_tok = jnp.where(real, slot_of % n_tok, 0)
    dst_tok = jnp.where(real, slot_of, 2 * n_tok + jnp.arange(n_slots, dtype=jnp.int32) % tm)
    tile_start = jnp.arange(n_slots // tm, dtype=jnp.int32) * tm
    tile_e = jnp.minimum(jnp.sum((ends[None, :] <= tile_start[:, None]).astype(jnp.int32), axis=1),
                         N_EXPERTS - 1).astype(jnp.int32)
    tile_v = (tile_start < ends[-1]).astype(jnp.int32)
    return src_tok, dst_tok, tile_e, tile_v


def _row_tile(limit, *dims):
    t = limit
    while any(d % t for d in dims):
        t //= 2
    return t


def kernel(x, c, ctx, c_ctx, w_ada, b_ada, w_in, lru_conv_w, lru_conv_b, lru_gate_w, lru_gate_b, lru_lambda, ssd_conv_w, ssd_conv_b, ssd_a_log, ssd_dt_bias, ssd_d, ssd_norm_w, att_lambda, att_norm_w, w_branch, w_out, ln_g, ln_b, w_router, router_bias, w_gate_e, w_up_e, w_down_e):
    B, S, D = x.shape
    CTX = ctx.shape[1]
    depth = w_ada.shape[0]
    alpha = (2 * depth) ** 0.25
    n_lat = B * S
    n_ctx = B * CTX
    M = n_lat + n_ctx
    assert D == 2 * SLAB * LANES and CTX % SEQ_CHUNK == 0 and S % SEQ_CHUNK == 0 and S % CTX == 0
    ncc, nlc = CTX // SEQ_CHUNK, S // SEQ_CHUNK

    xs = jnp.concatenate([x.reshape(n_lat, D), ctx.reshape(n_ctx, D)], axis=0)
    cvec = jnp.concatenate([c_ctx[None, :], c, jnp.zeros((8 - 1 - B, D), F32)], axis=0)
    cos_t, sa_t, sb_t = _rope_tables(B, S, CTX)
    wr_t = w_router.T
    rbias = router_bias.reshape(N_EXPERTS, 1)

    tm_big = _row_tile(1024, S, n_ctx)
    tm_mid = _row_tile(512, S, n_ctx)
    tm_small = _row_tile(256, S, n_ctx)

    i0, i1, i2, i3, i4, i5, i6, i7, i8 = [int(v) for v in
                                          (0, 1024, 2048, 3072, 5120, 5152, 6176, 7200, 8224)]
    for l in range(depth):
        need_ctx = l < depth - 1
        lam_init = 0.8 - 0.6 * math.exp(-0.3 * l)
        n_rows = M if need_ctx else n_lat

        wl = w_in[l]
        w_main = jnp.concatenate([wl[:, i3:i4], wl[:, i0:i3], wl[:, i5:]], axis=1).astype(BF16)
        w_dt = jnp.pad(wl[:, i4:i5], ((0, 0), (0, LANES - (i5 - i4)))).astype(BF16)
        gw = lru_gate_w[l]
        wg_lru = jnp.concatenate([gw[:, 0], gw[:, 1]], axis=-1).astype(BF16)

        mod = _ada(cvec, w_ada, b_ada[l], l).reshape(8, 6, D)
        P, dtp = _inproj(xs, mod, w_main, w_dt, tm=tm_big, tn=1024, n_lat=n_lat, S=S)

        h_dirs, y_dirs = [], []
        for d in range(2):
            rev = d == 1
            h_dirs.append(_lru_dir(P, lru_conv_w[l], lru_conv_b[l].reshape(1, -1), wg_lru[d],
                                   lru_gate_b[l, d], lru_lambda[l, d].reshape(1, -1),
                                   rev=rev, B=B, ncc=ncc, nlc=nlc))
            pad = ((0, 0), (SSD_HEADS * d, LANES - SSD_HEADS * (d + 1)))
            sp = jnp.concatenate([jnp.pad(ssd_dt_bias[l, d][None, :], pad),
                                  jnp.pad(ssd_a_log[l, d][None, :], pad),
                                  jnp.pad(ssd_d[l][None, :], pad),
                                  jnp.zeros((5, LANES), F32)], axis=0)
            y_dirs.append(_ssd_dir(P, dtp, ssd_conv_w[l], ssd_conv_b[l].reshape(1, -1), sp,
                                   rev=rev, d=d, B=B, ncc=ncc, nlc=nlc, add_skip=(d == 0)))

        qr, kr = _rope(P, cos_t, sa_t, sb_t, tm=tm_mid)
        nw_att = att_norm_w[l].reshape(1, LANES)
        att = _attention(qr, kr, P, att_lambda[l], nw_att, None, B=B, S=S, CTX=CTX,
                         lam_init=lam_init, ctx_queries=False)
        if need_ctx:
            att = _attention(qr, kr, P, att_lambda[l], nw_att, att, B=B, S=S, CTX=CTX,
                             lam_init=lam_init, ctx_queries=True)

        m = _merge(h_dirs[0], h_dirs[1], y_dirs[0], y_dirs[1], att, P, w_branch[l].astype(BF16),
                   ssd_norm_w[l].reshape(1, -1), n_rows=n_rows, tm=tm_mid, tn=512)
        x1, h2, ids, wts, cnt = _outproj(m, w_out[l].astype(BF16), xs, mod, ln_g[l, 0].reshape(1, D),
                                         ln_b[l, 0].reshape(1, D), wr_t, rbias, n_rows=n_rows,
                                         tm=tm_small, n_lat=n_lat, S=S, alpha=alpha)

        tile_counts = cnt[:, ::LANES].T.astype(jnp.int32)
        src_tok, dst_tok, tile_e, tile_v = _moe_plan(ids[:4], tile_counts, MOE_TILE, tm_small)
        yg = _expert_ffn(h2, src_tok, dst_tok, tile_e, tile_v, _cast_bf16(w_gate_e, l),
                         _cast_bf16(w_up_e, l), _cast_bf16(w_down_e, l), tm=MOE_TILE,
                         n_out=2 * n_rows + MOE_TILE)
        xs = _final(x1, yg, wts[:2].T, mod, ln_g[l, 1].reshape(1, D), ln_b[l, 1].reshape(1, D),
                    n_rows=n_rows, tm=tm_small, n_lat=n_lat, S=S, alpha=alpha)

    return xs[:n_lat].reshape(B, S, D)
```
